```python
import math
import jax
import jax.numpy as jnp
from jax import lax
import numpy as np


D_MODEL = 2048
BATCH = 2
SEQ = 16384
DEPTH = 2

GRID_W = 64
CTX_LEN = 256
D_MIX = D_MODEL
S5_WIDTH = D_MIX // 4
POOL_WIDTH = D_MIX // 4
RWKV_WIDTH = D_MIX // 2
S5_GROUP = 16
S5_GROUPS = S5_WIDTH // S5_GROUP
S5_STATE = 64
POOL_WINDOWS = (2, 4, 8, 16)
POOL_GROUP = POOL_WIDTH // len(POOL_WINDOWS)
RWKV_HEAD = 64
RWKV_HEADS = RWKV_WIDTH // RWKV_HEAD
RWKV_LORA = 64
RWKV_CONV = 3
N_DIR = 2
IN_SIZES = (S5_WIDTH, S5_WIDTH, POOL_WIDTH, POOL_WIDTH, 3 * RWKV_WIDTH, RWKV_WIDTH, N_DIR * RWKV_LORA, N_DIR * RWKV_LORA)
IN_OFFSETS = tuple(int(s) for s in np.cumsum(IN_SIZES)[:-1])
D_IN = sum(IN_SIZES)
DEEPNORM_ALPHA = (2 * DEPTH) ** 0.25
DEEPNORM_BETA = (8 * DEPTH) ** -0.25
RWKV_DECAY_SCALE = 0.606531
S5_MAX_RE = -1e-4
ADALN_EPS = 1e-6
LN_EPS = 1e-5
GN_EPS = 64e-5
L2_EPS = 1e-12
F32 = jnp.float32

kernel_name = "hybrid_s5_pool_rwkv7_prefix_dit"


def _layernorm(x, eps):
    x = x.astype(F32)
    mu = jnp.mean(x, axis=-1, keepdims=True)
    var = jnp.mean(jnp.square(x - mu), axis=-1, keepdims=True)
    return (x - mu) * lax.rsqrt(var + eps)


def _modulation(cond, w_ada, b_ada):
    m = jax.nn.silu(cond.astype(F32)) @ w_ada + b_ada
    return jnp.split(m, 3, axis=-1)


def _split_in(z):
    return jnp.split(z, IN_OFFSETS, axis=-1)


def _short_conv(z, w):
    n = z.shape[1]
    pad = w.shape[0] // 2
    zp = jnp.pad(z.astype(F32), ((0, 0), (pad, pad), (0, 0)))
    out = zp[:, 0:n] * w[0]
    for i in range(1, w.shape[0]):
        out = out + zp[:, i:i + n] * w[i]
    return out


def _s5_discretize(lam_re, lam_im, log_step):
    lam = lax.complex(jnp.minimum(lam_re.astype(F32), S5_MAX_RE), lam_im.astype(F32))
    step = jnp.exp(log_step.astype(F32))[..., None]
    log_lam_bar = lam * step
    b_scale = (jnp.exp(log_lam_bar) - 1.0) / lam
    return log_lam_bar, b_scale


def _s5_scan(bu, log_lam_bar, x0, reverse):
    n = bu.shape[0]
    first = n - 1 if reverse else 0
    bu = bu.at[first].add(jnp.exp(log_lam_bar) * x0)
    counts = jnp.ones((n, 1, 1, 1), F32)

    def combine(earlier, later):
        n_e, b_e = earlier
        n_l, b_l = later
        return n_e + n_l, jnp.exp(log_lam_bar * n_l) * b_e + b_l

    _, states = lax.associative_scan(combine, (counts, bu), reverse=reverse)
    return states


def _s5_states(u, init_states, log_lam_bar, b_scale, b_re, b_im):
    bsz, n, _ = u.shape
    ug = u.astype(F32).reshape(bsz, n, S5_GROUPS, S5_GROUP)
    bu = lax.complex(jnp.einsum('bngi,gpi->nbgp', ug, b_re.astype(F32)),
                     jnp.einsum('bngi,gpi->nbgp', ug, b_im.astype(F32)))
    return [_s5_scan(b_scale[d] * bu, log_lam_bar[d], init_states[d], reverse=(d == 1)) for d in range(N_DIR)]


def _s5_readout(u, states, c_re, c_im, d_skip, w_glu, b_glu):
    u = u.astype(F32)
    bsz, n, _ = u.shape
    y = d_skip * u
    for d in range(N_DIR):
        y_d = (jnp.einsum('nbgp,gip->bngi', states[d].real, c_re[d])
               - jnp.einsum('nbgp,gip->bngi', states[d].imag, c_im[d]))
        y = y + y_d.reshape(bsz, n, S5_WIDTH)
    y = jax.nn.gelu(y)
    return y * jax.nn.sigmoid(y @ w_glu + b_glu)


def _box_mean(z, w, axis):
    n = z.shape[axis]
    lo_off = w // 2
    idx = jnp.arange(n)
    lo = jnp.clip(idx - lo_off, 0, n - 1)
    hi = jnp.clip(idx - lo_off + w - 1, 0, n - 1)
    cs = jnp.cumsum(z, axis=axis)
    cs = jnp.concatenate([jnp.zeros_like(lax.slice_in_dim(cs, 0, 1, axis=axis)), cs], axis=axis)
    total = jnp.take(cs, hi + 1, axis=axis) - jnp.take(cs, lo, axis=axis)
    shape = [1] * z.ndim
    shape[axis] = n
    cnt = (hi - lo + 1).astype(z.dtype).reshape(shape)
    return total / cnt


def _pool_branch(u, w_pool, pool_scale, on_grid):
    u = u.astype(F32)
    bsz, n, _ = u.shape
    outs = []
    for g, (ug, win) in enumerate(zip(jnp.split(u, len(POOL_WINDOWS), axis=-1), POOL_WINDOWS)):
        if on_grid:
            rows = n // GRID_W
            grid = ug.reshape(bsz, rows, GRID_W, POOL_GROUP)
            m = _box_mean(_box_mean(grid, win, 1), win, 2).reshape(bsz, n, POOL_GROUP)
        else:
            m = _box_mean(ug, win, 1)
        outs.append((m - ug) @ w_pool[g])
    return jnp.concatenate(outs, axis=-1) * pool_scale


def _rwkv_scan(s0, r, w, k, v, kk, kka, reverse, want_out):
    def step(s, inp):
        w_t, k_t, v_t, kk_t, kka_t = inp[:5]
        s_kk = jnp.einsum('bhvk,bhk->bhv', s, kk_t)
        s = (s * w_t[:, :, None, :] - s_kk[..., None] * kka_t[:, :, None, :]
             + v_t[..., None] * k_t[:, :, None, :])
        if want_out:
            return s, jnp.einsum('bhvk,bhk->bhv', s, inp[5])
        return s, None

    xs = (w, k, v, kk, kka) + ((r,) if want_out else ())
    s_fin, o = lax.scan(step, s0, tuple(jnp.moveaxis(t, 1, 0) for t in xs), reverse=reverse)
    return s_fin, (jnp.moveaxis(o, 0, 1) if want_out else None)


def _rwkv_sequence(rkv, w_codes, a_codes, init_states, w0, w2, a0, a2, k_k, k_a, r_k, gn_w, gn_b, want_out):
    bsz, n = rkv.shape[:2]

    def heads(t):
        return t.astype(F32).reshape(bsz, n, RWKV_HEADS, RWKV_HEAD)

    r, k, v = (heads(t) for t in jnp.split(rkv, 3, axis=-1))
    kk = k * k_k.reshape(RWKV_HEADS, RWKV_HEAD)
    kk = kk / jnp.maximum(jnp.sqrt(jnp.sum(jnp.square(kk), axis=-1, keepdims=True)), L2_EPS)
    k_a_h = k_a.reshape(RWKV_HEADS, RWKV_HEAD)
    w_codes = w_codes.astype(F32).reshape(bsz, n, N_DIR, RWKV_LORA)
    a_codes = a_codes.astype(F32).reshape(bsz, n, N_DIR, RWKV_LORA)
    finals = []
    o_sum = 0.0
    k_sum = 0.0
    for d in range(N_DIR):
        w = heads(jnp.exp(-RWKV_DECAY_SCALE * jax.nn.sigmoid(w0[d] + jnp.tanh(w_codes[:, :, d]) @ w2[d])))
        a = heads(jax.nn.sigmoid(a0[d] + a_codes[:, :, d] @ a2[d]))
        k_d = k * (1.0 + (a - 1.0) * k_a_h)
        s_fin, o = _rwkv_scan(init_states[d], r, w, k_d, v, kk, kk * a, reverse=(d == 1), want_out=want_out)
        finals.append(s_fin)
        if want_out:
            o_sum = o_sum + o
            k_sum = k_sum + k_d
    if not want_out:
        return None, finals
    mu = jnp.mean(o_sum, axis=-1, keepdims=True)
    var = jnp.mean(jnp.square(o_sum - mu), axis=-1, keepdims=True)
    on = ((o_sum - mu) * lax.rsqrt(var + GN_EPS)).reshape(bsz, n, RWKV_WIDTH) * gn_w + gn_b
    bonus = jnp.sum(r * k_sum * r_k, axis=-1, keepdims=True) * v
    return on + bonus.reshape(bsz, n, RWKV_WIDTH), finals


def _merge(ys, gates, w_out):
    return jnp.concatenate([y * jax.nn.silu(g.astype(F32)) for y, g in zip(ys, gates)], axis=-1) @ w_out


def _layer(x, xc, c, c_ctx, w_ada, b_ada, w_in, conv_rkv, s5_lam_re, s5_lam_im, s5_log_step,
           s5_b_re, s5_b_im, s5_c_re, s5_c_im, s5_d, w_glu, b_glu, w_pool, pool_scale,
           rwkv_w0, rwkv_w2, rwkv_a0, rwkv_a2, rwkv_k_k, rwkv_k_a, rwkv_r_k, gn_w, gn_b,
           w_out, ln_g, ln_b, ctx_out):
    bsz = x.shape[0]
    shift, scale, gate = _modulation(c, w_ada, b_ada)
    shift_c, scale_c, gate_c = _modulation(c_ctx, w_ada, b_ada)
    h = _layernorm(x, ADALN_EPS) * (1.0 + scale[:, None]) + shift[:, None]
    hc = _layernorm(xc, ADALN_EPS) * (1.0 + scale_c) + shift_c
    s5_u, s5_g, pool_u, pool_g, rkv, rwkv_g, w_codes, a_codes = _split_in(h @ w_in)
    s5_uc, s5_gc, pool_uc, pool_gc, rkv_c, rwkv_gc, w_codes_c, a_codes_c = _split_in(hc @ w_in)

    log_lam_bar, b_scale = _s5_discretize(s5_lam_re, s5_lam_im, s5_log_step)
    zero_s5 = jnp.zeros((bsz, S5_GROUPS, S5_STATE), jnp.complex64)
    st_c = _s5_states(s5_uc, (zero_s5, zero_s5), log_lam_bar, b_scale, s5_b_re, s5_b_im)
    st = _s5_states(s5_u, (st_c[0][-1], st_c[1][0]), log_lam_bar, b_scale, s5_b_re, s5_b_im)
    y_s5 = _s5_readout(s5_u, st, s5_c_re, s5_c_im, s5_d, w_glu, b_glu)

    y_pool = _pool_branch(pool_u, w_pool, pool_scale, on_grid=True)

    rwkv_p = (rwkv_w0, rwkv_w2, rwkv_a0, rwkv_a2, rwkv_k_k, rwkv_k_a, rwkv_r_k, gn_w, gn_b)
    zero_rw = jnp.zeros((bsz, RWKV_HEADS, RWKV_HEAD, RWKV_HEAD), F32)
    y_rwkv_c, fin_c = _rwkv_sequence(_short_conv(rkv_c, conv_rkv), w_codes_c, a_codes_c,
                                     (zero_rw, zero_rw), *rwkv_p, want_out=ctx_out)
    y_rwkv, _ = _rwkv_sequence(_short_conv(rkv, conv_rkv), w_codes, a_codes, fin_c, *rwkv_p, want_out=True)

    out = _merge((y_s5, y_pool, y_rwkv), (s5_g, pool_g, rwkv_g), w_out)
    x_new = _layernorm(DEEPNORM_ALPHA * x + gate[:, None] * out, LN_EPS) * ln_g + ln_b

    xc_new = None
    if ctx_out:
        y_s5_c = _s5_readout(s5_uc, st_c, s5_c_re, s5_c_im, s5_d, w_glu, b_glu)
        y_pool_c = _pool_branch(pool_uc, w_pool, pool_scale, on_grid=False)
        out_c = _merge((y_s5_c, y_pool_c, y_rwkv_c), (s5_gc, pool_gc, rwkv_gc), w_out)
        xc_new = _layernorm(DEEPNORM_ALPHA * xc + gate_c * out_c, LN_EPS) * ln_g + ln_b
    return x_new, xc_new


def setup_inputs(seed: int = 0) -> dict:
    key = jax.random.key(seed)
    ks = iter(jax.random.split(key, 40))
    L = DEPTH

    def nrm(shape, s):
        return jax.random.normal(next(ks), shape, F32) * s

    x = nrm((BATCH, SEQ, D_MODEL), 1.0)
    c = nrm((BATCH, D_MODEL), 1.0)
    ctx = nrm((BATCH, CTX_LEN, D_MODEL), 1.0)
    c_ctx = nrm((D_MODEL,), 1.0)
    w_ada = nrm((L, D_MODEL, 3 * D_MODEL), 0.5 * D_MODEL ** -0.5)
    b_ada = nrm((L, 3 * D_MODEL), 0.02)
    w_in = nrm((L, D_MODEL, D_IN), D_MODEL ** -0.5)
    conv_rkv = jnp.array([0.25, 0.5, 0.25], F32)[:, None] + nrm((L, RWKV_CONV, 3 * RWKV_WIDTH), 0.1)
    s5_lam_re = -0.5 + nrm((L, N_DIR, S5_GROUPS, S5_STATE), 0.01)
    s5_lam_im = jnp.pi * jnp.arange(S5_STATE, dtype=F32) + nrm((L, N_DIR, S5_GROUPS, S5_STATE), 0.01)
    s5_log_step = jax.random.uniform(next(ks), (L, N_DIR, S5_GROUPS), F32, math.log(1e-3), math.log(1e-1))
    s5_b_re = nrm((L, S5_GROUPS, S5_STATE, S5_GROUP), (2 * S5_GROUP) ** -0.5)
    s5_b_im = nrm((L, S5_GROUPS, S5_STATE, S5_GROUP), (2 * S5_GROUP) ** -0.5)
    s5_c_re = nrm((L, N_DIR, S5_GROUPS, S5_GROUP, S5_STATE), S5_STATE ** -0.5)
    s5_c_im = nrm((L, N_DIR, S5_GROUPS, S5_GROUP, S5_STATE), S5_STATE ** -0.5)
    s5_d = nrm((L, S5_WIDTH), 1.0)
    w_glu = nrm((L, S5_WIDTH, S5_WIDTH), S5_WIDTH ** -0.5)
    b_glu = nrm((L, S5_WIDTH), 0.02)
    w_pool = nrm((L, len(POOL_WINDOWS), POOL_GROUP, POOL_GROUP), POOL_GROUP ** -0.5)
    pool_scale = 1.0 + nrm((L, POOL_WIDTH), 0.1)
    rwkv_w0 = -0.5 + nrm((L, N_DIR, RWKV_WIDTH), 0.5)
    rwkv_w2 = nrm((L, N_DIR, RWKV_LORA, RWKV_WIDTH), 0.1)
    rwkv_a0 = nrm((L, N_DIR, RWKV_WIDTH), 0.1)
    rwkv_a2 = nrm((L, N_DIR, RWKV_LORA, RWKV_WIDTH), 0.1)
    rwkv_k_k = 0.85 + nrm((L, RWKV_WIDTH), 0.02)
    rwkv_k_a = 1.0 + nrm((L, RWKV_WIDTH), 0.02)
    rwkv_r_k = nrm((L, RWKV_HEADS, RWKV_HEAD), 0.1)
    gn_w = 1.0 + nrm((L, RWKV_WIDTH), 0.02)
    gn_b = nrm((L, RWKV_WIDTH), 0.02)
    w_out = nrm((L, D_MIX, D_MODEL), DEEPNORM_BETA * D_MIX ** -0.5)
    ln_g = 1.0 + nrm((L, D_MODEL), 0.02)
    ln_b = nrm((L, D_MODEL), 0.02)
    return {"x": x, "c": c, "ctx": ctx, "c_ctx": c_ctx, "w_ada": w_ada, "b_ada": b_ada,
            "w_in": w_in, "conv_rkv": conv_rkv, "s5_lam_re": s5_lam_re, "s5_lam_im": s5_lam_im,
            "s5_log_step": s5_log_step, "s5_b_re": s5_b_re, "s5_b_im": s5_b_im,
            "s5_c_re": s5_c_re, "s5_c_im": s5_c_im, "s5_d": s5_d, "w_glu": w_glu, "b_glu": b_glu,
            "w_pool": w_pool, "pool_scale": pool_scale, "rwkv_w0": rwkv_w0, "rwkv_w2": rwkv_w2,
            "rwkv_a0": rwkv_a0, "rwkv_a2": rwkv_a2, "rwkv_k_k": rwkv_k_k, "rwkv_k_a": rwkv_k_a,
            "rwkv_r_k": rwkv_r_k, "gn_w": gn_w, "gn_b": gn_b, "w_out": w_out,
            "ln_g": ln_g, "ln_b": ln_b}


def reference(x, c, ctx, c_ctx, w_ada, b_ada, w_in, conv_rkv, s5_lam_re, s5_lam_im, s5_log_step,
              s5_b_re, s5_b_im, s5_c_re, s5_c_im, s5_d, w_glu, b_glu, w_pool, pool_scale,
              rwkv_w0, rwkv_w2, rwkv_a0, rwkv_a2, rwkv_k_k, rwkv_k_a, rwkv_r_k, gn_w, gn_b,
              w_out, ln_g, ln_b):
    xc = ctx
    for l in range(DEPTH):
        x, xc = _layer(x, xc, c, c_ctx, w_ada[l], b_ada[l], w_in[l], conv_rkv[l],
                       s5_lam_re[l], s5_lam_im[l], s5_log_step[l], s5_b_re[l], s5_b_im[l],
                       s5_c_re[l], s5_c_im[l], s5_d[l], w_glu[l], b_glu[l], w_pool[l], pool_scale[l],
                       rwkv_w0[l], rwkv_w2[l], rwkv_a0[l], rwkv_a2[l], rwkv_k_k[l], rwkv_k_a[l],
                       rwkv_r_k[l], gn_w[l], gn_b[l], w_out[l], ln_g[l], ln_b[l],
                       ctx_out=(l < DEPTH - 1))
    return x
```

```python
import functools
import math

import numpy as np
import jax
import jax.numpy as jnp
from jax import lax
from jax.experimental import pallas as pl
from jax.experimental.pallas import tpu as pltpu

F32 = jnp.float32
BF16 = jnp.bfloat16

D_MODEL = 2048
GRID_W = 64
S5_WIDTH = 512
S5_GROUP = 16
S5_GROUPS = 32
S5_STATE = 64
POOL_WIDTH = 512
POOL_WINDOWS = (2, 4, 8, 16)
POOL_GROUP = 128
RWKV_WIDTH = 1024
RWKV_HEAD = 64
RWKV_HEADS = 16
RWKV_LORA = 64
D_IN = 6400
RWKV_DECAY_SCALE = 0.606531
S5_MAX_RE = -1e-4
ADALN_EPS = 1e-6
LN_EPS = 1e-5
GN_EPS = 64e-5
L2_EPS = 1e-12

ZB_RKV = 0
ZB_RWKV_G = 3
ZB_S5_U = 8
ZB_S5_G = 9
ZB_POOL_U = 10
ZB_POOL_G = 11
ZB_CODES = 24

LANES = 128
SUBLANES = 8
VMEM_LIMIT = 56 * 1024 * 1024

S5_HALF = 256
S5_HSTATE = 1024
RW_CHUNK = 64
RW_PAIRS = RWKV_HEADS // 2
POOL_ROWS = 8


def _cparams(*sem):
    return pltpu.CompilerParams(dimension_semantics=sem, vmem_limit_bytes=VMEM_LIMIT)


def _dot(a, b):
    return jnp.dot(a.astype(BF16), b.astype(BF16), preferred_element_type=F32)


def _dot_nt(a, b):
    return lax.dot_general(a.astype(BF16), b.astype(BF16), (((1,), (1,)), ((), ())),
                           preferred_element_type=F32)


def _dot_tn(a, b):
    return lax.dot_general(a.astype(BF16), b.astype(BF16), (((0,), (0,)), ((), ())),
                           preferred_element_type=F32)


def _split(x, parts):
    out = []
    rem = x
    for _ in range(parts):
        hi = rem.astype(BF16)
        out.append(hi)
        rem = rem - hi.astype(F32)
    return out


def _dot_xl(a, b, parts=2):
    acc = None
    for t in _split(a, parts):
        p = jnp.dot(t, b, preferred_element_type=F32)
        acc = p if acc is None else acc + p
    return acc


def _dot_xr(a, b, parts=2):
    acc = None
    for t in _split(b, parts):
        p = jnp.dot(a, t, preferred_element_type=F32)
        acc = p if acc is None else acc + p
    return acc


def _silu(x):
    return x * jax.nn.sigmoid(x)


def _gelu_tanh(x):
    return 0.5 * x * (1.0 + jnp.tanh(math.sqrt(2.0 / math.pi) * (x + 0.044715 * (x * x * x))))


def _mod_kernel(c_ref, w_ref, b_ref, o_ref):
    s = _silu(c_ref[...])
    o_ref[0] = jnp.dot(s, w_ref[0], preferred_element_type=F32,
                       precision=lax.Precision.HIGHEST) + b_ref[0]


def _modulation(cond, w_ada, b_ada):
    depth, d, d3 = w_ada.shape
    tn = 512
    return pl.pallas_call(
        _mod_kernel,
        grid=(depth, d3 // tn),
        in_specs=[pl.BlockSpec((SUBLANES, d), lambda l, j: (0, 0)),
                  pl.BlockSpec((1, d, tn), lambda l, j: (l, 0, j)),
                  pl.BlockSpec((1, 1, tn), lambda l, j: (l, 0, j))],
        out_specs=pl.BlockSpec((1, SUBLANES, tn), lambda l, j: (l, 0, j)),
        out_shape=jax.ShapeDtypeStruct((depth, SUBLANES, d3), F32),
        compiler_params=_cparams("parallel", "parallel"),
    )(cond, w_ada, b_ada.reshape(depth, 1, d3))


def _in_kernel(x_ref, sh_ref, sc_ref, w_ref, o_ref, h_ref):
    @pl.when(pl.program_id(2) == 0)
    def _():
        x = x_ref[0]
        mu = jnp.mean(x, axis=-1, keepdims=True)
        xc = x - mu
        var = jnp.mean(xc * xc, axis=-1, keepdims=True)
        h = xc * lax.rsqrt(var + ADALN_EPS) * (1.0 + sc_ref[0]) + sh_ref[0]
        h_ref[...] = h.astype(BF16)

    o_ref[0] = jnp.dot(h_ref[...], w_ref[...], preferred_element_type=F32)


def _in_proj(x, shift, scale, w_in_bf16):
    bsz, n, d = x.shape
    tm = min(1024, n)
    tn = 640
    return pl.pallas_call(
        _in_kernel,
        grid=(bsz, n // tm, D_IN // tn),
        in_specs=[pl.BlockSpec((1, tm, d), lambda b, i, j: (b, i, 0)),
                  pl.BlockSpec((1, 1, d), lambda b, i, j: (b, 0, 0)),
                  pl.BlockSpec((1, 1, d), lambda b, i, j: (b, 0, 0)),
                  pl.BlockSpec((d, tn), lambda b, i, j: (0, j))],
        out_specs=pl.BlockSpec((1, tm, tn), lambda b, i, j: (b, i, j)),
        out_shape=jax.ShapeDtypeStruct((bsz, n, D_IN), F32),
        scratch_shapes=[pltpu.VMEM((tm, d), BF16)],
        compiler_params=_cparams("parallel", "parallel", "arbitrary"),
    )(x, shift, scale, w_in_bf16)


def _s5_tables(lam_re, lam_im, log_step, b_re, b_im, c_re, c_im, seg_len):
    lr = jnp.minimum(lam_re.astype(F32), S5_MAX_RE)
    li = lam_im.astype(F32)
    step = jnp.exp(log_step.astype(F32))[..., None]
    ar, ai = lr * step, li * step
    mag = jnp.exp(ar)
    abr, abi = mag * jnp.cos(ai), mag * jnp.sin(ai)
    den = lr * lr + li * li
    bsr = ((abr - 1.0) * lr + abi * li) / den
    bsi = (abi * lr - (abr - 1.0) * li) / den
    btr = bsr[..., None] * b_re[None] - bsi[..., None] * b_im[None]
    bti = bsr[..., None] * b_im[None] + bsi[..., None] * b_re[None]
    eye = jnp.eye(S5_GROUPS // 2, dtype=F32)
    wb, wc, lam, pw = [], [], [], []
    cnt = jnp.arange(1, seg_len + 1, dtype=F32)[:, None, None]
    for d in range(2):
        for hf in range(2):
            gs = slice(hf * 16, hf * 16 + 16)
            wbr = jnp.einsum('gpj,gh->gjhp', btr[d, gs], eye).reshape(S5_HALF, S5_HSTATE)
            wbi = jnp.einsum('gpj,gh->gjhp', bti[d, gs], eye).reshape(S5_HALF, S5_HSTATE)
            wb.append(jnp.concatenate([wbr, wbi], axis=1))
            wcr = jnp.einsum('gip,gh->gphi', c_re[d, gs], eye).reshape(S5_HSTATE, S5_HALF)
            wci = -jnp.einsum('gip,gh->gphi', c_im[d, gs], eye).reshape(S5_HSTATE, S5_HALF)
            wc.append(jnp.concatenate([wcr, wci], axis=0))
            lam.append(jnp.concatenate([abr[d, gs].reshape(1, -1), abi[d, gs].reshape(1, -1)], axis=1))
            pm = jnp.exp(cnt * ar[d, gs][None])
            pa = cnt * ai[d, gs][None]
            pw.append(jnp.concatenate([(pm * jnp.cos(pa)).reshape(seg_len, -1),
                                       (pm * jnp.sin(pa)).reshape(seg_len, -1)], axis=1))
    return (jnp.stack(wb).astype(BF16), jnp.stack(wc).astype(BF16), jnp.stack(lam), jnp.stack(pw))


def _s5_perms(tile, seg_len):
    rows = np.arange(tile)
    tok = (rows % SUBLANES) * seg_len + rows // SUBLANES
    pf = np.zeros((tile, tile), np.float32)
    pf[rows, tok] = 1.0
    pb = np.zeros((tile, tile), np.float32)
    pb[rows, tile - 1 - tok] = 1.0
    perm = np.stack([pf, pb])
    return jnp.asarray(perm, BF16), jnp.asarray(perm.transpose(0, 2, 1), BF16)


def _s5_kernel(uf_ref, ub_ref, p_ref, pt_ref, wb_ref, wc_ref, lam_ref, pw_ref, x0_ref,
               yf_ref, yb_ref, xfin_ref, bu_ref, carry_ref, cin_ref, *, seg_len):
    hs = S5_HSTATE
    lane_chunk = 512

    @pl.when(pl.program_id(1) == 0)
    def _():
        carry_ref[...] = x0_ref[0]

    for d in range(2):
        u_ref = uf_ref if d == 0 else ub_ref
        y_ref = yf_ref if d == 0 else yb_ref
        for hf in range(2):
            q = d * 2 + hf
            ub = u_ref[0, :, hf * S5_HALF:(hf + 1) * S5_HALF].astype(BF16)
            up = jnp.dot(p_ref[d], ub, preferred_element_type=F32).astype(BF16)
            bu_ref[...] = jnp.dot(up, wb_ref[q], preferred_element_type=F32)

            ends_r, ends_i = [], []
            for lc in range(hs // lane_chunk):
                re_l = slice(lc * lane_chunk, (lc + 1) * lane_chunk)
                im_l = slice(hs + lc * lane_chunk, hs + (lc + 1) * lane_chunk)
                lam_r = jnp.broadcast_to(lam_ref[q, :, re_l], (SUBLANES, lane_chunk))
                lam_i = jnp.broadcast_to(lam_ref[q, :, im_l], (SUBLANES, lane_chunk))

                def scan_body(i, x, re_l=re_l, im_l=im_l, lam_r=lam_r, lam_i=lam_i):
                    xr, xi = x
                    r0 = pl.multiple_of(i * SUBLANES, SUBLANES)
                    nr = lam_r * xr - lam_i * xi + bu_ref[pl.ds(r0, SUBLANES), re_l]
                    ni = lam_r * xi + lam_i * xr + bu_ref[pl.ds(r0, SUBLANES), im_l]
                    bu_ref[pl.ds(r0, SUBLANES), re_l] = nr
                    bu_ref[pl.ds(r0, SUBLANES), im_l] = ni
                    return nr, ni

                zero = jnp.zeros((SUBLANES, lane_chunk), F32)
                er, ei = lax.fori_loop(0, seg_len, scan_body, (zero, zero), unroll=8)
                ends_r.append(er)
                ends_i.append(ei)
            end_r = jnp.concatenate(ends_r, axis=1)
            end_i = jnp.concatenate(ends_i, axis=1)

            lm_r = pw_ref[q, seg_len - 1:seg_len, 0:hs]
            lm_i = pw_ref[q, seg_len - 1:seg_len, hs:2 * hs]
            cr = carry_ref[q, :, 0:hs]
            ci = carry_ref[q, :, hs:2 * hs]
            for s in range(SUBLANES):
                cin_ref[s:s + 1, 0:hs] = cr
                cin_ref[s:s + 1, hs:2 * hs] = ci
                nr = lm_r * cr - lm_i * ci + end_r[s:s + 1]
                ni = lm_r * ci + lm_i * cr + end_i[s:s + 1]
                cr, ci = nr, ni
            carry_ref[q, :, 0:hs] = cr
            carry_ref[q, :, hs:2 * hs] = ci

            for lc in range(hs // lane_chunk):
                re_l = slice(lc * lane_chunk, (lc + 1) * lane_chunk)
                im_l = slice(hs + lc * lane_chunk, hs + (lc + 1) * lane_chunk)
                cin_r = cin_ref[:, re_l]
                cin_i = cin_ref[:, im_l]

                def fix_body(i8, carry, re_l=re_l, im_l=im_l, cin_r=cin_r, cin_i=cin_i):
                    p0 = pl.multiple_of(i8 * SUBLANES, SUBLANES)
                    pr8 = pw_ref[q, pl.ds(p0, SUBLANES), re_l]
                    pi8 = pw_ref[q, pl.ds(p0, SUBLANES), im_l]
                    for j in range(SUBLANES):
                        r0 = pl.multiple_of((i8 * SUBLANES + j) * SUBLANES, SUBLANES)
                        pr = pr8[j:j + 1]
                        pi = pi8[j:j + 1]
                        bu_ref[pl.ds(r0, SUBLANES), re_l] += pr * cin_r - pi * cin_i
                        bu_ref[pl.ds(r0, SUBLANES), im_l] += pr * cin_i + pi * cin_r
                    return carry

                lax.fori_loop(0, seg_len // SUBLANES, fix_body, 0)

            yp = jnp.dot(bu_ref[...].astype(BF16), wc_ref[q], preferred_element_type=F32)
            y_ref[0, :, hf * S5_HALF:(hf + 1) * S5_HALF] = _dot_xr(pt_ref[d], yp)

    xfin_ref[0] = carry_ref[...]


def _s5_scan(z, tabs, x0):
    bsz, n, _ = z.shape
    tile = min(512, n)
    seg_len = tile // SUBLANES
    nk = n // tile
    wb, wc, lam, pw = tabs[seg_len]
    perm, perm_t = _s5_perms(tile, seg_len)
    const4 = lambda b, k: (0, 0, 0)
    return pl.pallas_call(
        functools.partial(_s5_kernel, seg_len=seg_len),
        grid=(bsz, nk),
        in_specs=[pl.BlockSpec((1, tile, S5_WIDTH), lambda b, k: (b, k, ZB_S5_U)),
                  pl.BlockSpec((1, tile, S5_WIDTH), lambda b, k: (b, nk - 1 - k, ZB_S5_U)),
                  pl.BlockSpec((2, tile, tile), const4),
                  pl.BlockSpec((2, tile, tile), const4),
                  pl.BlockSpec((4, S5_HALF, 2 * S5_HSTATE), const4),
                  pl.BlockSpec((4, 2 * S5_HSTATE, S5_HALF), const4),
                  pl.BlockSpec((4, 1, 2 * S5_HSTATE), const4),
                  pl.BlockSpec((4, seg_len, 2 * S5_HSTATE), const4),
                  pl.BlockSpec((1, 4, 1, 2 * S5_HSTATE), lambda b, k: (b, 0, 0, 0))],
        out_specs=[pl.BlockSpec((1, tile, S5_WIDTH), lambda b, k: (b, k, 0)),
                   pl.BlockSpec((1, tile, S5_WIDTH), lambda b, k: (b, nk - 1 - k, 0)),
                   pl.BlockSpec((1, 4, 1, 2 * S5_HSTATE), lambda b, k: (b, 0, 0, 0))],
        out_shape=[jax.ShapeDtypeStruct((bsz, n, S5_WIDTH), F32),
                   jax.ShapeDtypeStruct((bsz, n, S5_WIDTH), F32),
                   jax.ShapeDtypeStruct((bsz, 4, 1, 2 * S5_HSTATE), F32)],
        scratch_shapes=[pltpu.VMEM((tile, 2 * S5_HSTATE), F32),
                        pltpu.VMEM((4, 1, 2 * S5_HSTATE), F32),
                        pltpu.VMEM((SUBLANES, 2 * S5_HSTATE), F32)],
        compiler_params=_cparams("parallel", "arbitrary"),
    )(z, z, perm, perm_t, wb, wc, lam, pw, x0)


def _pool_matrices(on_grid, tile):
    mats = []
    for w in POOL_WINDOWS:
        lo = w // 2
        if on_grid:
            rows = tile // GRID_W
            ext_rows = 3 * rows
            m = np.zeros((rows, GRID_W, ext_rows, GRID_W), np.float32)
            for r in range(rows):
                for j in range(GRID_W):
                    r_lo = r + rows - lo
                    j_lo, j_hi = max(j - lo, 0), min(j - lo + w - 1, GRID_W - 1)
                    m[r, j, r_lo:r_lo + w, j_lo:j_hi + 1] = 1.0
            mats.append(m.reshape(tile, ext_rows * GRID_W))
        else:
            m = np.zeros((tile, tile), np.float32)
            for j in range(tile):
                m[j, max(j - lo, 0):min(j - lo + w - 1, tile - 1) + 1] = 1.0
            mats.append(m)
    return jnp.asarray(np.stack(mats), BF16)


def _window_count(idx, n, w):
    lo = w // 2
    hi_i = jnp.minimum(idx - lo + w - 1, n - 1)
    lo_i = jnp.maximum(idx - lo, 0)
    return (hi_i - lo_i + 1).astype(F32)


def _pool_kernel(*refs, on_grid, tile, n_rows):
    if on_grid:
        u_ref, p_ref, n_ref, g_ref, m_ref, wp_ref, sc_ref, o_ref = refs
    else:
        u_ref, g_ref, m_ref, wp_ref, sc_ref, o_ref = refs
    k = pl.program_id(1)
    cur = u_ref[0]
    if on_grid:
        prev = jnp.where(k > 0, p_ref[0], 0.0)
        nxt = jnp.where(k < pl.num_programs(1) - 1, n_ref[0], 0.0)
    tok = k * tile + lax.broadcasted_iota(jnp.int32, (tile, POOL_GROUP), 0)
    for g, w in enumerate(POOL_WINDOWS):
        gl = slice(g * POOL_GROUP, (g + 1) * POOL_GROUP)
        if on_grid:
            ext = jnp.concatenate([prev[:, gl], cur[:, gl], nxt[:, gl]], axis=0)
            row = jnp.right_shift(tok, int(math.log2(GRID_W)))
            col = jnp.bitwise_and(tok, GRID_W - 1)
            cnt = _window_count(row, n_rows, w) * _window_count(col, GRID_W, w)
        else:
            ext = cur[:, gl]
            cnt = _window_count(tok, tile, w)
        box = _dot_xr(m_ref[g], ext)
        diff = box / cnt - cur[:, gl]
        y = _dot(diff, wp_ref[g]) * sc_ref[:, gl]
        o_ref[0, :, gl] = y * _silu(g_ref[0, :, gl])


def _pool_branch(z, w_pool, pool_scale, on_grid):
    bsz, n, _ = z.shape
    sc = pool_scale.reshape(1, POOL_WIDTH)
    if on_grid:
        tile = POOL_ROWS * GRID_W
        nk = n // tile
        mats = _pool_matrices(True, tile)
        in_specs = [pl.BlockSpec((1, tile, POOL_WIDTH), lambda b, k: (b, k, ZB_POOL_U)),
                    pl.BlockSpec((1, tile, POOL_WIDTH), lambda b, k: (b, jnp.maximum(k - 1, 0), ZB_POOL_U)),
                    pl.BlockSpec((1, tile, POOL_WIDTH), lambda b, k: (b, jnp.minimum(k + 1, nk - 1), ZB_POOL_U)),
                    pl.BlockSpec((1, tile, POOL_WIDTH), lambda b, k: (b, k, ZB_POOL_G))]
        args = (z, z, z, z)
    else:
        tile = n
        nk = 1
        mats = _pool_matrices(False, tile)
        in_specs = [pl.BlockSpec((1, tile, POOL_WIDTH), lambda b, k: (b, k, ZB_POOL_U)),
                    pl.BlockSpec((1, tile, POOL_WIDTH), lambda b, k: (b, k, ZB_POOL_G))]
        args = (z, z)
    in_specs += [pl.BlockSpec(mats.shape, lambda b, k: (0, 0, 0)),
                 pl.BlockSpec(w_pool.shape, lambda b, k: (0, 0, 0)),
                 pl.BlockSpec((1, POOL_WIDTH), lambda b, k: (0, 0))]
    return pl.pallas_call(
        functools.partial(_pool_kernel, on_grid=on_grid, tile=tile, n_rows=n // GRID_W),
        grid=(bsz, nk),
        in_specs=in_specs,
        out_specs=pl.BlockSpec((1, tile, POOL_WIDTH), lambda b, k: (b, k, 0)),
        out_shape=jax.ShapeDtypeStruct((bsz, n, POOL_WIDTH), F32),
        compiler_params=_cparams("parallel", "parallel"),
    )(*args, mats, w_pool, sc)


def _rw_consts(tile):
    ch = np.arange(RWKV_WIDTH)
    e1 = (ch[:, None] // RWKV_HEAD == np.arange(LANES)[None, :]).astype(np.float32)
    t = np.arange(tile)
    same = (t[:, None] // RW_CHUNK) == (t[None, :] // RW_CHUNK)
    tri = np.stack([same & (t[None, :] <= t[:, None]), same & (t[None, :] >= t[:, None])]).astype(np.float32)
    return jnp.asarray(e1, BF16), jnp.asarray(e1.T, BF16), jnp.asarray(tri, BF16)


def _rw_kernel(zf_ref, zfp_ref, zfn_ref, cf_ref, zb_ref, zbp_ref, zbn_ref, cb_ref,
               conv_ref, w0_ref, w2_ref, a0_ref, a2_ref, kk_ref, ka_ref, rk_ref,
               e1_ref, e2_ref, tri_ref, s0_ref,
               of_ref, ob_ref, bonus_ref, sfin_ref,
               s_ref, rt_ref, at_ref, bt_ref, kt_ref, v_ref, g_ref, *, tile):
    k = pl.program_id(1)
    nk = pl.num_programs(1)
    n_chunks = tile // RW_CHUNK
    w = RWKV_WIDTH

    @pl.when(k == 0)
    def _():
        s_ref[...] = s0_ref[0]

    row = lax.broadcasted_iota(jnp.int32, (tile, 1), 0)
    st_row = lax.broadcasted_iota(jnp.int32, (2 * RW_CHUNK, LANES), 0)
    st_lane = lax.broadcasted_iota(jnp.int32, (2 * RW_CHUNK, LANES), 1)
    chunk_bits = int(math.log2(RW_CHUNK))
    head_mask = (jnp.right_shift(st_row, chunk_bits)
                 == jnp.right_shift(st_lane, chunk_bits)).astype(F32)
    t_row = jnp.bitwise_and(st_row, RW_CHUNK - 1)
    t_col = jnp.bitwise_and(st_lane, RW_CHUNK - 1)
    eye = (st_row == st_lane).astype(F32)

    def seg_sum(x):
        return _dot_xl(_dot_xl(x, e1_ref[...]), e2_ref[...])

    for d in range(2):
        z_ref, zp_ref, zn_ref, c_ref = ((zf_ref, zfp_ref, zfn_ref, cf_ref) if d == 0
                                        else (zb_ref, zbp_ref, zbn_ref, cb_ref))
        o_ref = of_ref if d == 0 else ob_ref
        tidx = k if d == 0 else nk - 1 - k

        z = z_ref[0]
        prev_row = jnp.where(tidx > 0, zp_ref[0, SUBLANES - 1:SUBLANES, :], 0.0)
        next_row = jnp.where(tidx < nk - 1, zn_ref[0, 0:1, :], 0.0)
        z_prev = jnp.where(row == 0, prev_row, pltpu.roll(z, 1, axis=0))
        z_next = jnp.where(row == tile - 1, next_row, pltpu.roll(z, tile - 1, axis=0))
        conv = conv_ref[0:1, :] * z_prev + conv_ref[1:2, :] * z + conv_ref[2:3, :] * z_next
        r = conv[:, 0:w]
        kx = conv[:, w:2 * w]
        v = conv[:, 2 * w:3 * w]

        kk = kx * kk_ref[...]
        ss = _dot_xl(kk * kk, e1_ref[...])
        inv = 1.0 / jnp.maximum(jnp.sqrt(ss), L2_EPS)
        kappa = kk * _dot_xl(inv, e2_ref[...])

        codes = c_ref[0]
        tw = jnp.tanh(codes[:, 0:LANES])
        ac = codes[:, LANES:2 * LANES]
        lw = -RWKV_DECAY_SCALE * jax.nn.sigmoid(w0_ref[d] + _dot(tw, w2_ref[d]))
        a = jax.nn.sigmoid(a0_ref[d] + _dot(ac, a2_ref[d]))
        kd = kx * (1.0 + (a - 1.0) * ka_ref[...])
        alpha = kappa * a

        if d == 0:
            a_other = jax.nn.sigmoid(a0_ref[1] + _dot(ac, a2_ref[1]))
            k_sum = kx * (2.0 + (a + a_other - 2.0) * ka_ref[...])
            bonus_ref[0] = seg_sum(r * k_sum * rk_ref[...]) * v

        cl = _dot_xr(tri_ref[d], lw, parts=3)
        g = jnp.exp(cl)
        gi = jnp.exp(-cl)
        rt_ref[...] = r * g
        at_ref[...] = -kappa * jnp.exp(cl - lw)
        bt_ref[...] = alpha * gi
        kt_ref[...] = kd * gi
        v_ref[...] = v
        g_ref[...] = g

        if d == 0:
            tri_s = (t_col < t_row).astype(F32)
            tri_i = (t_col <= t_row).astype(F32)
        else:
            tri_s = (t_col > t_row).astype(F32)
            tri_i = (t_col >= t_row).astype(F32)

        def chunk_body(ci, carry, d=d, o_ref=o_ref, tri_s=tri_s, tri_i=tri_i):
            c = ci if d == 0 else n_chunks - 1 - ci
            r0 = pl.multiple_of(c * RW_CHUNK, RW_CHUNK)
            rows = pl.ds(r0, RW_CHUNK)
            g_rows = pl.ds(pl.multiple_of(r0 + (RW_CHUNK - SUBLANES if d == 0 else 0), SUBLANES), SUBLANES)
            g_pick = SUBLANES - 1 if d == 0 else 0
            for p in range(RW_PAIRS):
                ln = slice(p * LANES, (p + 1) * LANES)

                def stack(ref):
                    x = ref[rows, ln]
                    return (jnp.concatenate([x, x], axis=0) * head_mask).astype(BF16)

                ar = jnp.concatenate([stack(at_ref), stack(rt_ref)], axis=0)
                bk = jnp.concatenate([stack(bt_ref), stack(kt_ref)], axis=0)
                v_st = stack(v_ref)
                gram = _dot_nt(ar, bk)
                half = 2 * RW_CHUNK
                l_ab = gram[0:half, 0:half] * tri_s
                l_ak = gram[0:half, half:2 * half] * tri_s
                m_rb = gram[half:2 * half, 0:half] * tri_i
                m_rk = gram[half:2 * half, half:2 * half] * tri_i
                inv_m = eye + l_ab
                pw = l_ab
                for _ in range(int(math.log2(RW_CHUNK)) - 1):
                    pw = _dot(pw, pw)
                    inv_m = inv_m + _dot(inv_m, pw)
                s_old = s_ref[d, p]
                ar_s = _dot_nt(ar, s_old)
                u_st = _dot(inv_m, ar_s[0:half] + _dot(l_ak, v_st))
                uv = jnp.concatenate([u_st.astype(BF16), v_st], axis=0)
                o_st = ar_s[half:2 * half] + _dot(jnp.concatenate([m_rb, m_rk], axis=1), uv)
                o_ref[0, rows, ln] = o_st[0:RW_CHUNK] + o_st[RW_CHUNK:2 * RW_CHUNK]
                ds = _dot_tn(uv, bk)
                s_ref[d, p] = (s_old + ds) * g_ref[g_rows, ln][g_pick:g_pick + 1]
            return carry

        lax.fori_loop(0, n_chunks, chunk_body, 0)

    sfin_ref[0] = s_ref[...]


def _rwkv_scan(z, conv_rkv, prm, s0):
    bsz, n, _ = z.shape
    tile = min(256, n)
    nk = n // tile
    hb = tile // SUBLANES
    nh = n // SUBLANES
    e1, e2, tri = _rw_consts(tile)
    w3 = 3 * RWKV_WIDTH

    def zspecs(tmap):
        return [pl.BlockSpec((1, tile, w3), lambda b, k: (b, tmap(k), ZB_RKV)),
                pl.BlockSpec((1, SUBLANES, w3), lambda b, k: (b, jnp.maximum(tmap(k) * hb - 1, 0), ZB_RKV)),
                pl.BlockSpec((1, SUBLANES, w3), lambda b, k: (b, jnp.minimum((tmap(k) + 1) * hb, nh - 1), ZB_RKV)),
                pl.BlockSpec((1, tile, 2 * LANES), lambda b, k: (b, tmap(k), ZB_CODES))]

    fwd = lambda k: k
    bwd = lambda k: nk - 1 - k
    c2 = lambda b, k: (0, 0)
    c3 = lambda b, k: (0, 0, 0)
    param_specs = [pl.BlockSpec((3, w3), c2),
                   pl.BlockSpec((2, 1, RWKV_WIDTH), c3), pl.BlockSpec((2, LANES, RWKV_WIDTH), c3),
                   pl.BlockSpec((2, 1, RWKV_WIDTH), c3), pl.BlockSpec((2, LANES, RWKV_WIDTH), c3),
                   pl.BlockSpec((1, RWKV_WIDTH), c2), pl.BlockSpec((1, RWKV_WIDTH), c2),
                   pl.BlockSpec((1, RWKV_WIDTH), c2),
                   pl.BlockSpec(e1.shape, c2), pl.BlockSpec(e2.shape, c2), pl.BlockSpec(tri.shape, c3),
                   pl.BlockSpec((1, 2, RW_PAIRS, LANES, LANES), lambda b, k: (b, 0, 0, 0, 0))]
    tok = jax.ShapeDtypeStruct((bsz, n, RWKV_WIDTH), F32)
    scr = pltpu.VMEM((tile, RWKV_WIDTH), F32)
    return pl.pallas_call(
        functools.partial(_rw_kernel, tile=tile),
        grid=(bsz, nk),
        in_specs=zspecs(fwd) + zspecs(bwd) + param_specs,
        out_specs=[pl.BlockSpec((1, tile, RWKV_WIDTH), lambda b, k: (b, k, 0)),
                   pl.BlockSpec((1, tile, RWKV_WIDTH), lambda b, k: (b, nk - 1 - k, 0)),
                   pl.BlockSpec((1, tile, RWKV_WIDTH), lambda b, k: (b, k, 0)),
                   pl.BlockSpec((1, 2, RW_PAIRS, LANES, LANES), lambda b, k: (b, 0, 0, 0, 0))],
        out_shape=[tok, tok, tok, jax.ShapeDtypeStruct((bsz, 2, RW_PAIRS, LANES, LANES), F32)],
        scratch_shapes=[pltpu.VMEM((2, RW_PAIRS, LANES, LANES), F32), scr, scr, scr, scr, scr, scr],
        compiler_params=_cparams("parallel", "arbitrary"),
    )(z, z, z, z, z, z, z, z, conv_rkv, *prm, e1, e2, tri, s0)


def _rw_params(w0, w2, a0, a2, k_k, k_a, r_k):
    zeros = jnp.zeros((RWKV_LORA, RWKV_WIDTH), F32)
    w2p = jnp.stack([jnp.concatenate([w2[0], zeros]), jnp.concatenate([zeros, w2[1]])])
    a2p = jnp.stack([jnp.concatenate([a2[0], zeros]), jnp.concatenate([zeros, a2[1]])])
    return (w0.reshape(2, 1, RWKV_WIDTH), w2p, a0.reshape(2, 1, RWKV_WIDTH), a2p,
            k_k.reshape(1, RWKV_WIDTH), k_a.reshape(1, RWKV_WIDTH), r_k.reshape(1, RWKV_WIDTH))


def _out_kernel(x_ref, su_ref, sg_ref, rg_ref, yf_ref, yb_ref, yp_ref, of_ref, ob_ref, bn_ref,
                gate_ref, sd_ref, wg_ref, bg_ref, gw_ref, gb_ref, e1_ref, e2_ref, wo_ref,
                lg_ref, lb_ref, o_ref, *, alpha):
    y = sd_ref[...] * su_ref[0] + yf_ref[0] + yb_ref[0]
    y = _gelu_tanh(y)
    y = y * jax.nn.sigmoid(_dot(y, wg_ref[...]) + bg_ref[...])
    m1 = y * _silu(sg_ref[0])
    o_sum = of_ref[0] + ob_ref[0]

    def seg_mean(t):
        return _dot_xl(_dot_xl(t, e1_ref[...]), e2_ref[...]) * (1.0 / RWKV_HEAD)

    dev = o_sum - seg_mean(o_sum)
    var = seg_mean(dev * dev)
    on = dev * lax.rsqrt(var + GN_EPS) * gw_ref[...] + gb_ref[...]
    m3 = (on + bn_ref[0]) * _silu(rg_ref[0])
    out = (_dot(m1, wo_ref[0:S5_WIDTH, :]) + _dot(yp_ref[0], wo_ref[S5_WIDTH:S5_WIDTH + POOL_WIDTH, :])
           + _dot(m3, wo_ref[S5_WIDTH + POOL_WIDTH:, :]))
    t = alpha * x_ref[0] + gate_ref[0] * out
    mu = jnp.mean(t, axis=-1, keepdims=True)
    tc = t - mu
    var_t = jnp.mean(tc * tc, axis=-1, keepdims=True)
    o_ref[0] = tc * lax.rsqrt(var_t + LN_EPS) * lg_ref[...] + lb_ref[...]


def _out_proj(x, z, yf, yb, ypool, o_f, o_b, bonus, gate, s5_d, w_glu, b_glu, gn_w, gn_b,
              w_out_bf16, ln_g, ln_b, alpha):
    bsz, n, d = x.shape
    tm = min(256, n)
    e1, e2, _ = _rw_consts(RW_CHUNK)
    tok = lambda b, i: (b, i, 0)
    c2 = lambda b, i: (0, 0)

    def zs(width, blk):
        return pl.BlockSpec((1, tm, width), lambda b, i: (b, i, blk))

    return pl.pallas_call(
        functools.partial(_out_kernel, alpha=alpha),
        grid=(bsz, n // tm),
        in_specs=[pl.BlockSpec((1, tm, d), tok),
                  zs(S5_WIDTH, ZB_S5_U), zs(S5_WIDTH, ZB_S5_G), zs(RWKV_WIDTH, ZB_RWKV_G),
                  pl.BlockSpec((1, tm, S5_WIDTH), tok), pl.BlockSpec((1, tm, S5_WIDTH), tok),
                  pl.BlockSpec((1, tm, POOL_WIDTH), tok),
                  pl.BlockSpec((1, tm, RWKV_WIDTH), tok), pl.BlockSpec((1, tm, RWKV_WIDTH), tok),
                  pl.BlockSpec((1, tm, RWKV_WIDTH), tok),
                  pl.BlockSpec((1, 1, d), lambda b, i: (b, 0, 0)),
                  pl.BlockSpec((1, S5_WIDTH), c2), pl.BlockSpec((S5_WIDTH, S5_WIDTH), c2),
                  pl.BlockSpec((1, S5_WIDTH), c2),
                  pl.BlockSpec((1, RWKV_WIDTH), c2), pl.BlockSpec((1, RWKV_WIDTH), c2),
                  pl.BlockSpec(e1.shape, c2), pl.BlockSpec(e2.shape, c2),
                  pl.BlockSpec((d, d), c2),
                  pl.BlockSpec((1, d), c2), pl.BlockSpec((1, d), c2)],
        out_specs=pl.BlockSpec((1, tm, d), tok),
        out_shape=jax.ShapeDtypeStruct((bsz, n, d), F32),
        compiler_params=_cparams("parallel", "parallel"),
    )(x, z, z, z, yf, yb, ypool, o_f, o_b, bonus, gate,
      s5_d.reshape(1, -1), w_glu, b_glu.reshape(1, -1), gn_w.reshape(1, -1), gn_b.reshape(1, -1),
      e1, e2, w_out_bf16, ln_g.reshape(1, -1), ln_b.reshape(1, -1))


def _permute_w_in(w):
    return jnp.concatenate([w[:, 2048:6144], w[:, 0:2048], w[:, 6144:6400]], axis=1).astype(BF16)


def kernel(x, c, ctx, c_ctx, w_ada, b_ada, w_in, conv_rkv, s5_lam_re, s5_lam_im, s5_log_step,
           s5_b_re, s5_b_im, s5_c_re, s5_c_im, s5_d, w_glu, b_glu, w_pool, pool_scale,
           rwkv_w0, rwkv_w2, rwkv_a0, rwkv_a2, rwkv_k_k, rwkv_k_a, rwkv_r_k, gn_w, gn_b,
           w_out, ln_g, ln_b):
    bsz, n, d = x.shape
    n_ctx = ctx.shape[1]
    depth = w_ada.shape[0]
    alpha = (2 * depth) ** 0.25
    assert bsz + 1 <= SUBLANES and n % 512 == 0 and n_ctx % 256 == 0 and n_ctx <= 512

    cond = jnp.zeros((SUBLANES, d), F32).at[0:bsz].set(c.astype(F32)).at[bsz].set(c_ctx.astype(F32))
    mod = _modulation(cond, w_ada, b_ada)

    s5_zero = jnp.zeros((bsz, 4, 1, 2 * S5_HSTATE), F32)
    rw_zero = jnp.zeros((bsz, 2, RW_PAIRS, LANES, LANES), F32)
    seg_lens = sorted({min(512, n) // SUBLANES, min(512, n_ctx) // SUBLANES})

    xc = ctx
    for l in range(depth):
        ctx_out = l < depth - 1
        shift, scale, gate = (mod[l, 0:bsz, i * d:(i + 1) * d].reshape(bsz, 1, d) for i in range(3))
        shift_c, scale_c, gate_c = (jnp.broadcast_to(mod[l, bsz, i * d:(i + 1) * d], (bsz, 1, d))
                                    for i in range(3))
        w_in_l = _permute_w_in(w_in[l])
        w_out_l = w_out[l].astype(BF16)
        s5_tabs = {m: _s5_tables(s5_lam_re[l], s5_lam_im[l], s5_log_step[l], s5_b_re[l], s5_b_im[l],
                                 s5_c_re[l], s5_c_im[l], m) for m in seg_lens}
        rw_prm = _rw_params(rwkv_w0[l], rwkv_w2[l], rwkv_a0[l], rwkv_a2[l],
                            rwkv_k_k[l], rwkv_k_a[l], rwkv_r_k[l])

        zc = _in_proj(xc, shift_c, scale_c, w_in_l)
        z = _in_proj(x, shift, scale, w_in_l)

        yf_c, yb_c, s5_fin = _s5_scan(zc, s5_tabs, s5_zero)
        yf, yb, _ = _s5_scan(z, s5_tabs, s5_fin)

        of_c, ob_c, bonus_c, rw_fin = _rwkv_scan(zc, conv_rkv[l], rw_prm, rw_zero)
        o_f, o_b, bonus, _ = _rwkv_scan(z, conv_rkv[l], rw_prm, rw_fin)

        ypool = _pool_branch(z, w_pool[l], pool_scale[l], on_grid=True)
        tail = (s5_d[l], w_glu[l], b_glu[l], gn_w[l], gn_b[l], w_out_l, ln_g[l], ln_b[l], alpha)
        x_new = _out_proj(x, z, yf, yb, ypool, o_f, o_b, bonus, gate, *tail)
        if ctx_out:
            ypool_c = _pool_branch(zc, w_pool[l], pool_scale[l], on_grid=False)
            xc = _out_proj(xc, zc, yf_c, yb_c, ypool_c, of_c, ob_c, bonus_c, gate_c, *tail)
        x = x_new
    return x
```

```python
import functools
import math

import numpy as np
import jax
import jax.numpy as jnp
from jax import lax
from jax.experimental import pallas as pl
from jax.experimental.pallas import tpu as pltpu

F32 = jnp.float32
BF16 = jnp.bfloat16

D_MODEL = 2048
GRID_W = 64
S5_WIDTH = 512
S5_GROUP = 16
S5_GROUPS = 32
S5_STATE = 64
POOL_WIDTH = 512
POOL_WINDOWS = (2, 4, 8, 16)
POOL_GROUP = 128
RWKV_WIDTH = 1024
RWKV_HEAD = 64
RWKV_HEADS = 16
RWKV_LORA = 64
D_IN = 6400
RWKV_DECAY_SCALE = 0.606531
S5_MAX_RE = -1e-4
ADALN_EPS = 1e-6
LN_EPS = 1e-5
GN_EPS = 64e-5
L2_EPS = 1e-12

ZB_RKV = 0
ZB_RWKV_G = 3
ZB_S5_U = 8
ZB_S5_G = 9
ZB_POOL_U = 10
ZB_POOL_G = 11
ZB_CODES = 24

LANES = 128
SUBLANES = 8
VMEM_LIMIT = 56 * 1024 * 1024

S5_HALF = 256
S5_HSTATE = 1024
RW_CHUNK = 64
RW_PAIRS = RWKV_HEADS // 2
POOL_ROWS = 8


def _cparams(*sem):
    return pltpu.CompilerParams(dimension_semantics=sem, vmem_limit_bytes=VMEM_LIMIT)


def _dot(a, b):
    return jnp.dot(a.astype(BF16), b.astype(BF16), preferred_element_type=F32)


def _dot_nt(a, b):
    return lax.dot_general(a.astype(BF16), b.astype(BF16), (((1,), (1,)), ((), ())),
                           preferred_element_type=F32)


def _dot_tn(a, b):
    return lax.dot_general(a.astype(BF16), b.astype(BF16), (((0,), (0,)), ((), ())),
                           preferred_element_type=F32)


def _split(x, parts):
    out = []
    rem = x
    for _ in range(parts):
        hi = rem.astype(BF16)
        out.append(hi)
        rem = rem - hi.astype(F32)
    return out


def _dot_xl(a, b, parts=2):
    acc = None
    for t in _split(a, parts):
        p = jnp.dot(t, b, preferred_element_type=F32)
        acc = p if acc is None else acc + p
    return acc


def _dot_xr(a, b, parts=2):
    acc = None
    for t in _split(b, parts):
        p = jnp.dot(a, t, preferred_element_type=F32)
        acc = p if acc is None else acc + p
    return acc


def _silu(x):
    return x * jax.nn.sigmoid(x)


def _gelu_tanh(x):
    return 0.5 * x * (1.0 + jnp.tanh(math.sqrt(2.0 / math.pi) * (x + 0.044715 * (x * x * x))))


def _mod_kernel(c_ref, w_ref, b_ref, o_ref):
    s = _silu(c_ref[...])
    o_ref[0] = jnp.dot(s, w_ref[0], preferred_element_type=F32,
                       precision=lax.Precision.HIGHEST) + b_ref[0]


def _modulation(cond, w_ada, b_ada):
    depth, d, d3 = w_ada.shape
    tn = 512
    return pl.pallas_call(
        _mod_kernel,
        grid=(depth, d3 // tn),
        in_specs=[pl.BlockSpec((SUBLANES, d), lambda l, j: (0, 0)),
                  pl.BlockSpec((1, d, tn), lambda l, j: (l, 0, j)),
                  pl.BlockSpec((1, 1, tn), lambda l, j: (l, 0, j))],
        out_specs=pl.BlockSpec((1, SUBLANES, tn), lambda l, j: (l, 0, j)),
        out_shape=jax.ShapeDtypeStruct((depth, SUBLANES, d3), F32),
        compiler_params=_cparams("parallel", "parallel"),
    )(cond, w_ada, b_ada.reshape(depth, 1, d3))


def _in_kernel(x_ref, sh_ref, sc_ref, w_ref, o_ref, h_ref):
    @pl.when(pl.program_id(2) == 0)
    def _():
        x = x_ref[0]
        mu = jnp.mean(x, axis=-1, keepdims=True)
        xc = x - mu
        var = jnp.mean(xc * xc, axis=-1, keepdims=True)
        h = xc * lax.rsqrt(var + ADALN_EPS) * (1.0 + sc_ref[0]) + sh_ref[0]
        h_ref[...] = h.astype(BF16)

    o_ref[0] = jnp.dot(h_ref[...], w_ref[...], preferred_element_type=F32)


def _in_proj(x, shift, scale, w_in_bf16):
    bsz, n, d = x.shape
    tm = min(1024, n)
    tn = 640
    return pl.pallas_call(
        _in_kernel,
        grid=(bsz, n // tm, D_IN // tn),
        in_specs=[pl.BlockSpec((1, tm, d), lambda b, i, j: (b, i, 0)),
                  pl.BlockSpec((1, 1, d), lambda b, i, j: (b, 0, 0)),
                  pl.BlockSpec((1, 1, d), lambda b, i, j: (b, 0, 0)),
                  pl.BlockSpec((d, tn), lambda b, i, j: (0, j))],
        out_specs=pl.BlockSpec((1, tm, tn), lambda b, i, j: (b, i, j)),
        out_shape=jax.ShapeDtypeStruct((bsz, n, D_IN), F32),
        scratch_shapes=[pltpu.VMEM((tm, d), BF16)],
        compiler_params=_cparams("parallel", "parallel", "arbitrary"),
    )(x, shift, scale, w_in_bf16)


def _s5_tables(lam_re, lam_im, log_step, b_re, b_im, c_re, c_im, seg_len):
    lr = jnp.minimum(lam_re.astype(F32), S5_MAX_RE)
    li = lam_im.astype(F32)
    step = jnp.exp(log_step.astype(F32))[..., None]
    ar, ai = lr * step, li * step
    mag = jnp.exp(ar)
    abr, abi = mag * jnp.cos(ai), mag * jnp.sin(ai)
    den = lr * lr + li * li
    bsr = ((abr - 1.0) * lr + abi * li) / den
    bsi = (abi * lr - (abr - 1.0) * li) / den
    btr = bsr[..., None] * b_re[None] - bsi[..., None] * b_im[None]
    bti = bsr[..., None] * b_im[None] + bsi[..., None] * b_re[None]
    eye = jnp.eye(S5_GROUPS // 2, dtype=F32)
    wb, wc, lam, pw = [], [], [], []
    cnt = jnp.arange(1, seg_len + 1, dtype=F32)[:, None, None]
    for d in range(2):
        for hf in range(2):
            gs = slice(hf * 16, hf * 16 + 16)
            wbr = jnp.einsum('gpj,gh->gjhp', btr[d, gs], eye).reshape(S5_HALF, S5_HSTATE)
            wbi = jnp.einsum('gpj,gh->gjhp', bti[d, gs], eye).reshape(S5_HALF, S5_HSTATE)
            wb.append(jnp.concatenate([wbr, wbi], axis=1))
            wcr = jnp.einsum('gip,gh->gphi', c_re[d, gs], eye).reshape(S5_HSTATE, S5_HALF)
            wci = -jnp.einsum('gip,gh->gphi', c_im[d, gs], eye).reshape(S5_HSTATE, S5_HALF)
            wc.append(jnp.concatenate([wcr, wci], axis=0))
            lam.append(jnp.concatenate([abr[d, gs].reshape(1, -1), abi[d, gs].reshape(1, -1)], axis=1))
            pm = jnp.exp(cnt * ar[d, gs][None])
            pa = cnt * ai[d, gs][None]
            pw.append(jnp.concatenate([(pm * jnp.cos(pa)).reshape(seg_len, -1),
                                       (pm * jnp.sin(pa)).reshape(seg_len, -1)], axis=1))
    return (jnp.stack(wb).astype(BF16), jnp.stack(wc).astype(BF16), jnp.stack(lam), jnp.stack(pw))


def _s5_perms(tile, seg_len):
    rows = np.arange(tile)
    tok = (rows % SUBLANES) * seg_len + rows // SUBLANES
    pf = np.zeros((tile, tile), np.float32)
    pf[rows, tok] = 1.0
    pb = np.zeros((tile, tile), np.float32)
    pb[rows, tile - 1 - tok] = 1.0
    perm = np.stack([pf, pb])
    return jnp.asarray(perm, BF16), jnp.asarray(perm.transpose(0, 2, 1), BF16)


def _s5_kernel(uf_ref, ub_ref, p_ref, pt_ref, wb_ref, wc_ref, lam_ref, pw_ref, x0_ref,
               yf_ref, yb_ref, xfin_ref, bu_ref, carry_ref, cin_ref, *, seg_len):
    hs = S5_HSTATE
    lane_chunk = 512

    @pl.when(pl.program_id(1) == 0)
    def _():
        carry_ref[...] = x0_ref[0]

    for d in range(2):
        u_ref = uf_ref if d == 0 else ub_ref
        y_ref = yf_ref if d == 0 else yb_ref
        for hf in range(2):
            q = d * 2 + hf
            ub = u_ref[0, :, hf * S5_HALF:(hf + 1) * S5_HALF].astype(BF16)
            up = jnp.dot(p_ref[d], ub, preferred_element_type=F32).astype(BF16)
            bu_ref[...] = jnp.dot(up, wb_ref[q], preferred_element_type=F32)

            ends_r, ends_i = [], []
            for lc in range(hs // lane_chunk):
                re_l = slice(lc * lane_chunk, (lc + 1) * lane_chunk)
                im_l = slice(hs + lc * lane_chunk, hs + (lc + 1) * lane_chunk)
                lam_r = jnp.broadcast_to(lam_ref[q, :, re_l], (SUBLANES, lane_chunk))
                lam_i = jnp.broadcast_to(lam_ref[q, :, im_l], (SUBLANES, lane_chunk))

                def scan_body(i, x, re_l=re_l, im_l=im_l, lam_r=lam_r, lam_i=lam_i):
                    xr, xi = x
                    r0 = pl.multiple_of(i * SUBLANES, SUBLANES)
                    nr = lam_r * xr - lam_i * xi + bu_ref[pl.ds(r0, SUBLANES), re_l]
                    ni = lam_r * xi + lam_i * xr + bu_ref[pl.ds(r0, SUBLANES), im_l]
                    bu_ref[pl.ds(r0, SUBLANES), re_l] = nr
                    bu_ref[pl.ds(r0, SUBLANES), im_l] = ni
                    return nr, ni

                zero = jnp.zeros((SUBLANES, lane_chunk), F32)
                er, ei = lax.fori_loop(0, seg_len, scan_body, (zero, zero), unroll=8)
                ends_r.append(er)
                ends_i.append(ei)
            end_r = jnp.concatenate(ends_r, axis=1)
            end_i = jnp.concatenate(ends_i, axis=1)

            lm_r = pw_ref[q, seg_len - 1:seg_len, 0:hs]
            lm_i = pw_ref[q, seg_len - 1:seg_len, hs:2 * hs]
            cr = carry_ref[q, :, 0:hs]
            ci = carry_ref[q, :, hs:2 * hs]
            for s in range(SUBLANES):
                cin_ref[s:s + 1, 0:hs] = cr
                cin_ref[s:s + 1, hs:2 * hs] = ci
                nr = lm_r * cr - lm_i * ci + end_r[s:s + 1]
                ni = lm_r * ci + lm_i * cr + end_i[s:s + 1]
                cr, ci = nr, ni
            carry_ref[q, :, 0:hs] = cr
            carry_ref[q, :, hs:2 * hs] = ci

            for lc in range(hs // lane_chunk):
                re_l = slice(lc * lane_chunk, (lc + 1) * lane_chunk)
                im_l = slice(hs + lc * lane_chunk, hs + (lc + 1) * lane_chunk)
                cin_r = cin_ref[:, re_l]
                cin_i = cin_ref[:, im_l]

                def fix_body(i8, carry, re_l=re_l, im_l=im_l, cin_r=cin_r, cin_i=cin_i):
                    p0 = pl.multiple_of(i8 * SUBLANES, SUBLANES)
                    pr8 = pw_ref[q, pl.ds(p0, SUBLANES), re_l]
                    pi8 = pw_ref[q, pl.ds(p0, SUBLANES), im_l]
                    for j in range(SUBLANES):
                        r0 = pl.multiple_of((i8 * SUBLANES + j) * SUBLANES, SUBLANES)
                        pr = pr8[j:j + 1]
                        pi = pi8[j:j + 1]
                        bu_ref[pl.ds(r0, SUBLANES), re_l] += pr * cin_r - pi * cin_i
                        bu_ref[pl.ds(r0, SUBLANES), im_l] += pr * cin_i + pi * cin_r
                    return carry

                lax.fori_loop(0, seg_len // SUBLANES, fix_body, 0)

            yp = jnp.dot(bu_ref[...].astype(BF16), wc_ref[q], preferred_element_type=F32)
            y_ref[0, :, hf * S5_HALF:(hf + 1) * S5_HALF] = _dot_xr(pt_ref[d], yp)

    xfin_ref[0] = carry_ref[...]


def _s5_scan(z, tabs, x0):
    bsz, n, _ = z.shape
    tile = min(512, n)
    seg_len = tile // SUBLANES
    nk = n // tile
    wb, wc, lam, pw = tabs[seg_len]
    perm, perm_t = _s5_perms(tile, seg_len)
    const4 = lambda b, k: (0, 0, 0)
    return pl.pallas_call(
        functools.partial(_s5_kernel, seg_len=seg_len),
        grid=(bsz, nk),
        in_specs=[pl.BlockSpec((1, tile, S5_WIDTH), lambda b, k: (b, k, ZB_S5_U)),
                  pl.BlockSpec((1, tile, S5_WIDTH), lambda b, k: (b, nk - 1 - k, ZB_S5_U)),
                  pl.BlockSpec((2, tile, tile), const4),
                  pl.BlockSpec((2, tile, tile), const4),
                  pl.BlockSpec((4, S5_HALF, 2 * S5_HSTATE), const4),
                  pl.BlockSpec((4, 2 * S5_HSTATE, S5_HALF), const4),
                  pl.BlockSpec((4, 1, 2 * S5_HSTATE), const4),
                  pl.BlockSpec((4, seg_len, 2 * S5_HSTATE), const4),
                  pl.BlockSpec((1, 4, 1, 2 * S5_HSTATE), lambda b, k: (b, 0, 0, 0))],
        out_specs=[pl.BlockSpec((1, tile, S5_WIDTH), lambda b, k: (b, k, 0)),
                   pl.BlockSpec((1, tile, S5_WIDTH), lambda b, k: (b, nk - 1 - k, 0)),
                   pl.BlockSpec((1, 4, 1, 2 * S5_HSTATE), lambda b, k: (b, 0, 0, 0))],
        out_shape=[jax.ShapeDtypeStruct((bsz, n, S5_WIDTH), F32),
                   jax.ShapeDtypeStruct((bsz, n, S5_WIDTH), F32),
                   jax.ShapeDtypeStruct((bsz, 4, 1, 2 * S5_HSTATE), F32)],
        scratch_shapes=[pltpu.VMEM((tile, 2 * S5_HSTATE), F32),
                        pltpu.VMEM((4, 1, 2 * S5_HSTATE), F32),
                        pltpu.VMEM((SUBLANES, 2 * S5_HSTATE), F32)],
        compiler_params=_cparams("parallel", "arbitrary"),
    )(z, z, perm, perm_t, wb, wc, lam, pw, x0)


def _pool_matrices(on_grid, tile):
    mats = []
    for w in POOL_WINDOWS:
        lo = w // 2
        if on_grid:
            rows = tile // GRID_W
            ext_rows = 3 * rows
            m = np.zeros((rows, GRID_W, ext_rows, GRID_W), np.float32)
            for r in range(rows):
                for j in range(GRID_W):
                    r_lo = r + rows - lo
                    j_lo, j_hi = max(j - lo, 0), min(j - lo + w - 1, GRID_W - 1)
                    m[r, j, r_lo:r_lo + w, j_lo:j_hi + 1] = 1.0
            mats.append(m.reshape(tile, ext_rows * GRID_W))
        else:
            m = np.zeros((tile, tile), np.float32)
            for j in range(tile):
                m[j, max(j - lo, 0):min(j - lo + w - 1, tile - 1) + 1] = 1.0
            mats.append(m)
    return jnp.asarray(np.stack(mats), BF16)


def _window_count(idx, n, w):
    lo = w // 2
    hi_i = jnp.minimum(idx - lo + w - 1, n - 1)
    lo_i = jnp.maximum(idx - lo, 0)
    return (hi_i - lo_i + 1).astype(F32)


def _pool_kernel(*refs, on_grid, tile, n_rows):
    if on_grid:
        u_ref, p_ref, n_ref, g_ref, m_ref, wp_ref, sc_ref, o_ref = refs
    else:
        u_ref, g_ref, m_ref, wp_ref, sc_ref, o_ref = refs
    k = pl.program_id(1)
    cur = u_ref[0]
    if on_grid:
        prev = jnp.where(k > 0, p_ref[0], 0.0)
        nxt = jnp.where(k < pl.num_programs(1) - 1, n_ref[0], 0.0)
    tok = k * tile + lax.broadcasted_iota(jnp.int32, (tile, POOL_GROUP), 0)
    for g, w in enumerate(POOL_WINDOWS):
        gl = slice(g * POOL_GROUP, (g + 1) * POOL_GROUP)
        if on_grid:
            ext = jnp.concatenate([prev[:, gl], cur[:, gl], nxt[:, gl]], axis=0)
            row = jnp.right_shift(tok, int(math.log2(GRID_W)))
            col = jnp.bitwise_and(tok, GRID_W - 1)
            cnt = _window_count(row, n_rows, w) * _window_count(col, GRID_W, w)
        else:
            ext = cur[:, gl]
            cnt = _window_count(tok, tile, w)
        box = _dot_xr(m_ref[g], ext)
        diff = box / cnt - cur[:, gl]
        y = _dot(diff, wp_ref[g]) * sc_ref[:, gl]
        o_ref[0, :, gl] = y * _silu(g_ref[0, :, gl])


def _pool_branch(z, w_pool, pool_scale, on_grid):
    bsz, n, _ = z.shape
    sc = pool_scale.reshape(1, POOL_WIDTH)
    if on_grid:
        tile = POOL_ROWS * GRID_W
        nk = n // tile
        mats = _pool_matrices(True, tile)
        in_specs = [pl.BlockSpec((1, tile, POOL_WIDTH), lambda b, k: (b, k, ZB_POOL_U)),
                    pl.BlockSpec((1, tile, POOL_WIDTH), lambda b, k: (b, jnp.maximum(k - 1, 0), ZB_POOL_U)),
                    pl.BlockSpec((1, tile, POOL_WIDTH), lambda b, k: (b, jnp.minimum(k + 1, nk - 1), ZB_POOL_U)),
                    pl.BlockSpec((1, tile, POOL_WIDTH), lambda b, k: (b, k, ZB_POOL_G))]
        args = (z, z, z, z)
    else:
        tile = n
        nk = 1
        mats = _pool_matrices(False, tile)
        in_specs = [pl.BlockSpec((1, tile, POOL_WIDTH), lambda b, k: (b, k, ZB_POOL_U)),
                    pl.BlockSpec((1, tile, POOL_WIDTH), lambda b, k: (b, k, ZB_POOL_G))]
        args = (z, z)
    in_specs += [pl.BlockSpec(mats.shape, lambda b, k: (0, 0, 0)),
                 pl.BlockSpec(w_pool.shape, lambda b, k: (0, 0, 0)),
                 pl.BlockSpec((1, POOL_WIDTH), lambda b, k: (0, 0))]
    return pl.pallas_call(
        functools.partial(_pool_kernel, on_grid=on_grid, tile=tile, n_rows=n // GRID_W),
        grid=(bsz, nk),
        in_specs=in_specs,
        out_specs=pl.BlockSpec((1, tile, POOL_WIDTH), lambda b, k: (b, k, 0)),
        out_shape=jax.ShapeDtypeStruct((bsz, n, POOL_WIDTH), F32),
        compiler_params=_cparams("parallel", "parallel"),
    )(*args, mats, w_pool, sc)


def _rw_consts(tile):
    ch = np.arange(RWKV_WIDTH)
    e1 = (ch[:, None] // RWKV_HEAD == np.arange(LANES)[None, :]).astype(np.float32)
    t = np.arange(tile)
    same = (t[:, None] // RW_CHUNK) == (t[None, :] // RW_CHUNK)
    tri = np.stack([same & (t[None, :] <= t[:, None]), same & (t[None, :] >= t[:, None])]).astype(np.float32)
    return jnp.asarray(e1, BF16), jnp.asarray(e1.T, BF16), jnp.asarray(tri, BF16)


def _rw_kernel(zf_ref, zfp_ref, zfn_ref, cf_ref, zb_ref, zbp_ref, zbn_ref, cb_ref,
               conv_ref, w0_ref, w2_ref, a0_ref, a2_ref, kk_ref, ka_ref, rk_ref,
               e1_ref, e2_ref, tri_ref, s0_ref,
               of_ref, ob_ref, bonus_ref, sfin_ref,
               s_ref, rt_ref, at_ref, bt_ref, kt_ref, v_ref, g_ref, *, tile):
    k = pl.program_id(1)
    nk = pl.num_programs(1)
    n_chunks = tile // RW_CHUNK
    w = RWKV_WIDTH

    @pl.when(k == 0)
    def _():
        s_ref[...] = s0_ref[0]

    row = lax.broadcasted_iota(jnp.int32, (tile, 1), 0)
    st_row = lax.broadcasted_iota(jnp.int32, (2 * RW_CHUNK, LANES), 0)
    st_lane = lax.broadcasted_iota(jnp.int32, (2 * RW_CHUNK, LANES), 1)
    chunk_bits = int(math.log2(RW_CHUNK))
    head_mask = (jnp.right_shift(st_row, chunk_bits)
                 == jnp.right_shift(st_lane, chunk_bits)).astype(F32)
    t_row = jnp.bitwise_and(st_row, RW_CHUNK - 1)
    t_col = jnp.bitwise_and(st_lane, RW_CHUNK - 1)
    eye = (st_row == st_lane).astype(F32)

    def seg_sum(x):
        return _dot_xl(_dot_xl(x, e1_ref[...]), e2_ref[...])

    for d in range(2):
        z_ref, zp_ref, zn_ref, c_ref = ((zf_ref, zfp_ref, zfn_ref, cf_ref) if d == 0
                                        else (zb_ref, zbp_ref, zbn_ref, cb_ref))
        tidx = k if d == 0 else nk - 1 - k

        z = z_ref[0]
        prev_row = jnp.where(tidx > 0, zp_ref[0, SUBLANES - 1:SUBLANES, :], 0.0)
        next_row = jnp.where(tidx < nk - 1, zn_ref[0, 0:1, :], 0.0)
        z_prev = jnp.where(row == 0, prev_row, pltpu.roll(z, 1, axis=0))
        z_next = jnp.where(row == tile - 1, next_row, pltpu.roll(z, tile - 1, axis=0))
        conv = conv_ref[0:1, :] * z_prev + conv_ref[1:2, :] * z + conv_ref[2:3, :] * z_next
        r = conv[:, 0:w]
        kx = conv[:, w:2 * w]
        v = conv[:, 2 * w:3 * w]

        kk = kx * kk_ref[...]
        ss = _dot_xl(kk * kk, e1_ref[...])
        inv = 1.0 / jnp.maximum(jnp.sqrt(ss), L2_EPS)
        kappa = kk * _dot_xl(inv, e2_ref[...])

        codes = c_ref[0]
        tw = jnp.tanh(codes[:, 0:LANES])
        ac = codes[:, LANES:2 * LANES]
        lw = -RWKV_DECAY_SCALE * jax.nn.sigmoid(w0_ref[d] + _dot(tw, w2_ref[d]))
        a = jax.nn.sigmoid(a0_ref[d] + _dot(ac, a2_ref[d]))
        kd = kx * (1.0 + (a - 1.0) * ka_ref[...])
        alpha = kappa * a

        if d == 0:
            a_other = jax.nn.sigmoid(a0_ref[1] + _dot(ac, a2_ref[1]))
            k_sum = kx * (2.0 + (a + a_other - 2.0) * ka_ref[...])
            bonus_ref[0] = seg_sum(r * k_sum * rk_ref[...]) * v

        cl = _dot_xr(tri_ref[d], lw, parts=3)
        g = jnp.exp(cl)
        gi = jnp.exp(-cl)
        rt_ref[d] = r * g
        at_ref[d] = -kappa * jnp.exp(cl - lw)
        bt_ref[d] = alpha * gi
        kt_ref[d] = kd * gi
        v_ref[d] = v
        g_ref[d] = g

    tri_s = [(t_col < t_row).astype(F32), (t_col > t_row).astype(F32)]
    tri_i = [(t_col <= t_row).astype(F32), (t_col >= t_row).astype(F32)]
    o_refs = (of_ref, ob_ref)
    half = 2 * RW_CHUNK
    chains = [(d, p) for p in range(RW_PAIRS) for d in range(2)]

    def chunk_body(ci, carry):
        r0 = (pl.multiple_of(ci * RW_CHUNK, RW_CHUNK),
              pl.multiple_of((n_chunks - 1 - ci) * RW_CHUNK, RW_CHUNK))

        def stack(ref, d, p):
            x = ref[d, pl.ds(r0[d], RW_CHUNK), p * LANES:(p + 1) * LANES]
            return (jnp.concatenate([x, x], axis=0) * head_mask).astype(BF16)

        ar = {c: jnp.concatenate([stack(at_ref, *c), stack(rt_ref, *c)], axis=0) for c in chains}
        bk = {c: jnp.concatenate([stack(bt_ref, *c), stack(kt_ref, *c)], axis=0) for c in chains}
        v_st = {c: stack(v_ref, *c) for c in chains}
        gram = {c: _dot_nt(ar[c], bk[c]) for c in chains}
        ar_s = {c: _dot_nt(ar[c], s_ref[c[0], c[1]]) for c in chains}
        l_ab = {c: gram[c][0:half, 0:half] * tri_s[c[0]] for c in chains}
        l_ak = {c: (gram[c][0:half, half:2 * half] * tri_s[c[0]]).astype(BF16) for c in chains}
        m_rbk = {c: jnp.concatenate([gram[c][half:2 * half, 0:half] * tri_i[c[0]],
                                     gram[c][half:2 * half, half:2 * half] * tri_i[c[0]]],
                                    axis=1).astype(BF16) for c in chains}
        rhs = {c: ar_s[c][0:half] + _dot(l_ak[c], v_st[c]) for c in chains}
        inv_m = {c: eye + l_ab[c] for c in chains}
        pw = {c: l_ab[c].astype(BF16) for c in chains}
        for _ in range(int(math.log2(RW_CHUNK)) - 1):
            pw = {c: _dot(pw[c], pw[c]).astype(BF16) for c in chains}
            inv_m = {c: inv_m[c] + _dot(inv_m[c], pw[c]) for c in chains}
        uv = {c: jnp.concatenate([_dot(inv_m[c], rhs[c]).astype(BF16), v_st[c]], axis=0)
              for c in chains}
        for c in chains:
            d, p = c
            ln = slice(p * LANES, (p + 1) * LANES)
            o_st = ar_s[c][half:2 * half] + _dot(m_rbk[c], uv[c])
            o_refs[d][0, pl.ds(r0[d], RW_CHUNK), ln] = o_st[0:RW_CHUNK] + o_st[RW_CHUNK:2 * RW_CHUNK]
        for c in chains:
            d, p = c
            ln = slice(p * LANES, (p + 1) * LANES)
            g_rows = pl.ds(pl.multiple_of(r0[d] + (RW_CHUNK - SUBLANES if d == 0 else 0), SUBLANES),
                           SUBLANES)
            g_pick = SUBLANES - 1 if d == 0 else 0
            ds = _dot_tn(uv[c], bk[c])
            s_ref[d, p] = (s_ref[d, p] + ds) * g_ref[d, g_rows, ln][g_pick:g_pick + 1]
        return carry

    lax.fori_loop(0, n_chunks, chunk_body, 0)

    sfin_ref[0] = s_ref[...]


def _rwkv_scan(z, conv_rkv, prm, s0):
    bsz, n, _ = z.shape
    tile = min(256, n)
    nk = n // tile
    hb = tile // SUBLANES
    nh = n // SUBLANES
    e1, e2, tri = _rw_consts(tile)
    w3 = 3 * RWKV_WIDTH

    def zspecs(tmap):
        return [pl.BlockSpec((1, tile, w3), lambda b, k: (b, tmap(k), ZB_RKV)),
                pl.BlockSpec((1, SUBLANES, w3), lambda b, k: (b, jnp.maximum(tmap(k) * hb - 1, 0), ZB_RKV)),
                pl.BlockSpec((1, SUBLANES, w3), lambda b, k: (b, jnp.minimum((tmap(k) + 1) * hb, nh - 1), ZB_RKV)),
                pl.BlockSpec((1, tile, 2 * LANES), lambda b, k: (b, tmap(k), ZB_CODES))]

    fwd = lambda k: k
    bwd = lambda k: nk - 1 - k
    c2 = lambda b, k: (0, 0)
    c3 = lambda b, k: (0, 0, 0)
    param_specs = [pl.BlockSpec((3, w3), c2),
                   pl.BlockSpec((2, 1, RWKV_WIDTH), c3), pl.BlockSpec((2, LANES, RWKV_WIDTH), c3),
                   pl.BlockSpec((2, 1, RWKV_WIDTH), c3), pl.BlockSpec((2, LANES, RWKV_WIDTH), c3),
                   pl.BlockSpec((1, RWKV_WIDTH), c2), pl.BlockSpec((1, RWKV_WIDTH), c2),
                   pl.BlockSpec((1, RWKV_WIDTH), c2),
                   pl.BlockSpec(e1.shape, c2), pl.BlockSpec(e2.shape, c2), pl.BlockSpec(tri.shape, c3),
                   pl.BlockSpec((1, 2, RW_PAIRS, LANES, LANES), lambda b, k: (b, 0, 0, 0, 0))]
    tok = jax.ShapeDtypeStruct((bsz, n, RWKV_WIDTH), F32)
    scr = pltpu.VMEM((2, tile, RWKV_WIDTH), F32)
    return pl.pallas_call(
        functools.partial(_rw_kernel, tile=tile),
        grid=(bsz, nk),
        in_specs=zspecs(fwd) + zspecs(bwd) + param_specs,
        out_specs=[pl.BlockSpec((1, tile, RWKV_WIDTH), lambda b, k: (b, k, 0)),
                   pl.BlockSpec((1, tile, RWKV_WIDTH), lambda b, k: (b, nk - 1 - k, 0)),
                   pl.BlockSpec((1, tile, RWKV_WIDTH), lambda b, k: (b, k, 0)),
                   pl.BlockSpec((1, 2, RW_PAIRS, LANES, LANES), lambda b, k: (b, 0, 0, 0, 0))],
        out_shape=[tok, tok, tok, jax.ShapeDtypeStruct((bsz, 2, RW_PAIRS, LANES, LANES), F32)],
        scratch_shapes=[pltpu.VMEM((2, RW_PAIRS, LANES, LANES), F32), scr, scr, scr, scr, scr, scr],
        compiler_params=_cparams("parallel", "arbitrary"),
    )(z, z, z, z, z, z, z, z, conv_rkv, *prm, e1, e2, tri, s0)


def _rw_params(w0, w2, a0, a2, k_k, k_a, r_k):
    zeros = jnp.zeros((RWKV_LORA, RWKV_WIDTH), F32)
    w2p = jnp.stack([jnp.concatenate([w2[0], zeros]), jnp.concatenate([zeros, w2[1]])])
    a2p = jnp.stack([jnp.concatenate([a2[0], zeros]), jnp.concatenate([zeros, a2[1]])])
    return (w0.reshape(2, 1, RWKV_WIDTH), w2p, a0.reshape(2, 1, RWKV_WIDTH), a2p,
            k_k.reshape(1, RWKV_WIDTH), k_a.reshape(1, RWKV_WIDTH), r_k.reshape(1, RWKV_WIDTH))


def _out_kernel(x_ref, su_ref, sg_ref, rg_ref, yf_ref, yb_ref, yp_ref, of_ref, ob_ref, bn_ref,
                gate_ref, sd_ref, wg_ref, bg_ref, gw_ref, gb_ref, e1_ref, e2_ref, wo_ref,
                lg_ref, lb_ref, o_ref, *, alpha):
    y = sd_ref[...] * su_ref[0] + yf_ref[0] + yb_ref[0]
    y = _gelu_tanh(y)
    y = y * jax.nn.sigmoid(_dot(y, wg_ref[...]) + bg_ref[...])
    m1 = y * _silu(sg_ref[0])
    o_sum = of_ref[0] + ob_ref[0]

    def seg_mean(t):
        return _dot_xl(_dot_xl(t, e1_ref[...]), e2_ref[...]) * (1.0 / RWKV_HEAD)

    dev = o_sum - seg_mean(o_sum)
    var = seg_mean(dev * dev)
    on = dev * lax.rsqrt(var + GN_EPS) * gw_ref[...] + gb_ref[...]
    m3 = (on + bn_ref[0]) * _silu(rg_ref[0])
    out = (_dot(m1, wo_ref[0:S5_WIDTH, :]) + _dot(yp_ref[0], wo_ref[S5_WIDTH:S5_WIDTH + POOL_WIDTH, :])
           + _dot(m3, wo_ref[S5_WIDTH + POOL_WIDTH:, :]))
    t = alpha * x_ref[0] + gate_ref[0] * out
    mu = jnp.mean(t, axis=-1, keepdims=True)
    tc = t - mu
    var_t = jnp.mean(tc * tc, axis=-1, keepdims=True)
    o_ref[0] = tc * lax.rsqrt(var_t + LN_EPS) * lg_ref[...] + lb_ref[...]


def _out_proj(x, z, yf, yb, ypool, o_f, o_b, bonus, gate, s5_d, w_glu, b_glu, gn_w, gn_b,
              w_out_bf16, ln_g, ln_b, alpha):
    bsz, n, d = x.shape
    tm = min(256, n)
    e1, e2, _ = _rw_consts(RW_CHUNK)
    tok = lambda b, i: (b, i, 0)
    c2 = lambda b, i: (0, 0)

    def zs(width, blk):
        return pl.BlockSpec((1, tm, width), lambda b, i: (b, i, blk))

    return pl.pallas_call(
        functools.partial(_out_kernel, alpha=alpha),
        grid=(bsz, n // tm),
        in_specs=[pl.BlockSpec((1, tm, d), tok),
                  zs(S5_WIDTH, ZB_S5_U), zs(S5_WIDTH, ZB_S5_G), zs(RWKV_WIDTH, ZB_RWKV_G),
                  pl.BlockSpec((1, tm, S5_WIDTH), tok), pl.BlockSpec((1, tm, S5_WIDTH), tok),
                  pl.BlockSpec((1, tm, POOL_WIDTH), tok),
                  pl.BlockSpec((1, tm, RWKV_WIDTH), tok), pl.BlockSpec((1, tm, RWKV_WIDTH), tok),
                  pl.BlockSpec((1, tm, RWKV_WIDTH), tok),
                  pl.BlockSpec((1, 1, d), lambda b, i: (b, 0, 0)),
                  pl.BlockSpec((1, S5_WIDTH), c2), pl.BlockSpec((S5_WIDTH, S5_WIDTH), c2),
                  pl.BlockSpec((1, S5_WIDTH), c2),
                  pl.BlockSpec((1, RWKV_WIDTH), c2), pl.BlockSpec((1, RWKV_WIDTH), c2),
                  pl.BlockSpec(e1.shape, c2), pl.BlockSpec(e2.shape, c2),
                  pl.BlockSpec((d, d), c2),
                  pl.BlockSpec((1, d), c2), pl.BlockSpec((1, d), c2)],
        out_specs=pl.BlockSpec((1, tm, d), tok),
        out_shape=jax.ShapeDtypeStruct((bsz, n, d), F32),
        compiler_params=_cparams("parallel", "parallel"),
    )(x, z, z, z, yf, yb, ypool, o_f, o_b, bonus, gate,
      s5_d.reshape(1, -1), w_glu, b_glu.reshape(1, -1), gn_w.reshape(1, -1), gn_b.reshape(1, -1),
      e1, e2, w_out_bf16, ln_g.reshape(1, -1), ln_b.reshape(1, -1))


def _permute_w_in(w):
    return jnp.concatenate([w[:, 2048:6144], w[:, 0:2048], w[:, 6144:6400]], axis=1).astype(BF16)


def kernel(x, c, ctx, c_ctx, w_ada, b_ada, w_in, conv_rkv, s5_lam_re, s5_lam_im, s5_log_step,
           s5_b_re, s5_b_im, s5_c_re, s5_c_im, s5_d, w_glu, b_glu, w_pool, pool_scale,
           rwkv_w0, rwkv_w2, rwkv_a0, rwkv_a2, rwkv_k_k, rwkv_k_a, rwkv_r_k, gn_w, gn_b,
           w_out, ln_g, ln_b):
    bsz, n, d = x.shape
    n_ctx = ctx.shape[1]
    depth = w_ada.shape[0]
    alpha = (2 * depth) ** 0.25
    assert bsz + 1 <= SUBLANES and n % 512 == 0 and n_ctx % 256 == 0 and n_ctx <= 512

    cond = jnp.zeros((SUBLANES, d), F32).at[0:bsz].set(c.astype(F32)).at[bsz].set(c_ctx.astype(F32))
    mod = _modulation(cond, w_ada, b_ada)

    s5_zero = jnp.zeros((bsz, 4, 1, 2 * S5_HSTATE), F32)
    rw_zero = jnp.zeros((bsz, 2, RW_PAIRS, LANES, LANES), F32)
    seg_lens = sorted({min(512, n) // SUBLANES, min(512, n_ctx) // SUBLANES})

    xc = ctx
    for l in range(depth):
        ctx_out = l < depth - 1
        shift, scale, gate = (mod[l, 0:bsz, i * d:(i + 1) * d].reshape(bsz, 1, d) for i in range(3))
        shift_c, scale_c, gate_c = (jnp.broadcast_to(mod[l, bsz, i * d:(i + 1) * d], (bsz, 1, d))
                                    for i in range(3))
        w_in_l = _permute_w_in(w_in[l])
        w_out_l = w_out[l].astype(BF16)
        s5_tabs = {m: _s5_tables(s5_lam_re[l], s5_lam_im[l], s5_log_step[l], s5_b_re[l], s5_b_im[l],
                                 s5_c_re[l], s5_c_im[l], m) for m in seg_lens}
        rw_prm = _rw_params(rwkv_w0[l], rwkv_w2[l], rwkv_a0[l], rwkv_a2[l],
                            rwkv_k_k[l], rwkv_k_a[l], rwkv_r_k[l])

        zc = _in_proj(xc, shift_c, scale_c, w_in_l)
        z = _in_proj(x, shift, scale, w_in_l)

        yf_c, yb_c, s5_fin = _s5_scan(zc, s5_tabs, s5_zero)
        yf, yb, _ = _s5_scan(z, s5_tabs, s5_fin)

        of_c, ob_c, bonus_c, rw_fin = _rwkv_scan(zc, conv_rkv[l], rw_prm, rw_zero)
        o_f, o_b, bonus, _ = _rwkv_scan(z, conv_rkv[l], rw_prm, rw_fin)

        ypool = _pool_branch(z, w_pool[l], pool_scale[l], on_grid=True)
        tail = (s5_d[l], w_glu[l], b_glu[l], gn_w[l], gn_b[l], w_out_l, ln_g[l], ln_b[l], alpha)
        x_new = _out_proj(x, z, yf, yb, ypool, o_f, o_b, bonus, gate, *tail)
        if ctx_out:
            ypool_c = _pool_branch(zc, w_pool[l], pool_scale[l], on_grid=False)
            xc = _out_proj(xc, zc, yf_c, yb_c, ypool_c, of_c, ob_c, bonus_c, gate_c, *tail)
        x = x_new
    return x
```

```python
import functools
import math

import numpy as np
import jax
import jax.numpy as jnp
from jax import lax
from jax.experimental import pallas as pl
from jax.experimental.pallas import tpu as pltpu

F32 = jnp.float32
BF16 = jnp.bfloat16

D_MODEL = 2048
GRID_W = 64
S5_WIDTH = 512
S5_GROUP = 16
S5_GROUPS = 32
S5_STATE = 64
POOL_WIDTH = 512
POOL_WINDOWS = (2, 4, 8, 16)
POOL_GROUP = 128
RWKV_WIDTH = 1024
RWKV_HEAD = 64
RWKV_HEADS = 16
RWKV_LORA = 64
D_IN = 6400
RWKV_DECAY_SCALE = 0.606531
S5_MAX_RE = -1e-4
ADALN_EPS = 1e-6
LN_EPS = 1e-5
GN_EPS = 64e-5
L2_EPS = 1e-12

ZB_RKV = 0
ZB_RWKV_G = 3
ZB_S5_U = 8
ZB_S5_G = 9
ZB_POOL_U = 10
ZB_POOL_G = 11
ZB_CODES = 24

LANES = 128
SUBLANES = 8
VMEM_LIMIT = 56 * 1024 * 1024

S5_HALF = 256
S5_HSTATE = 1024
RW_CHUNK = 64
RW_PAIRS = RWKV_HEADS // 2
POOL_ROWS = 8


def _cparams(*sem):
    return pltpu.CompilerParams(dimension_semantics=sem, vmem_limit_bytes=VMEM_LIMIT)


def _dot(a, b):
    return jnp.dot(a.astype(BF16), b.astype(BF16), preferred_element_type=F32)


def _dot_nt(a, b):
    return lax.dot_general(a.astype(BF16), b.astype(BF16), (((1,), (1,)), ((), ())),
                           preferred_element_type=F32)


def _dot_tn(a, b):
    return lax.dot_general(a.astype(BF16), b.astype(BF16), (((0,), (0,)), ((), ())),
                           preferred_element_type=F32)


def _split(x, parts):
    out = []
    rem = x
    for _ in range(parts):
        hi = rem.astype(BF16)
        out.append(hi)
        rem = rem - hi.astype(F32)
    return out


def _dot_xl(a, b, parts=2):
    acc = None
    for t in _split(a, parts):
        p = jnp.dot(t, b, preferred_element_type=F32)
        acc = p if acc is None else acc + p
    return acc


def _dot_xr(a, b, parts=2):
    acc = None
    for t in _split(b, parts):
        p = jnp.dot(a, t, preferred_element_type=F32)
        acc = p if acc is None else acc + p
    return acc


def _silu(x):
    return x * jax.nn.sigmoid(x)


def _gelu_tanh(x):
    return 0.5 * x * (1.0 + jnp.tanh(math.sqrt(2.0 / math.pi) * (x + 0.044715 * (x * x * x))))


def _mod_kernel(c_ref, w_ref, b_ref, o_ref):
    s = _silu(c_ref[...])
    o_ref[0] = jnp.dot(s, w_ref[0], preferred_element_type=F32,
                       precision=lax.Precision.HIGHEST) + b_ref[0]


def _modulation(cond, w_ada, b_ada):
    depth, d, d3 = w_ada.shape
    tn = 512
    return pl.pallas_call(
        _mod_kernel,
        grid=(depth, d3 // tn),
        in_specs=[pl.BlockSpec((SUBLANES, d), lambda l, j: (0, 0)),
                  pl.BlockSpec((1, d, tn), lambda l, j: (l, 0, j)),
                  pl.BlockSpec((1, 1, tn), lambda l, j: (l, 0, j))],
        out_specs=pl.BlockSpec((1, SUBLANES, tn), lambda l, j: (l, 0, j)),
        out_shape=jax.ShapeDtypeStruct((depth, SUBLANES, d3), F32),
        compiler_params=_cparams("parallel", "parallel"),
    )(cond, w_ada, b_ada.reshape(depth, 1, d3))


def _in_kernel(x_ref, sh_ref, sc_ref, w_ref, o_ref, h_ref):
    @pl.when(pl.program_id(2) == 0)
    def _():
        x = x_ref[0]
        mu = jnp.mean(x, axis=-1, keepdims=True)
        xc = x - mu
        var = jnp.mean(xc * xc, axis=-1, keepdims=True)
        h = xc * lax.rsqrt(var + ADALN_EPS) * (1.0 + sc_ref[0]) + sh_ref[0]
        h_ref[...] = h.astype(BF16)

    o_ref[0] = jnp.dot(h_ref[...], w_ref[...], preferred_element_type=F32)


def _in_proj(x, shift, scale, w_in_bf16):
    bsz, n, d = x.shape
    tm = min(1024, n)
    tn = 1280
    return pl.pallas_call(
        _in_kernel,
        grid=(bsz, n // tm, D_IN // tn),
        in_specs=[pl.BlockSpec((1, tm, d), lambda b, i, j: (b, i, 0)),
                  pl.BlockSpec((1, 1, d), lambda b, i, j: (b, 0, 0)),
                  pl.BlockSpec((1, 1, d), lambda b, i, j: (b, 0, 0)),
                  pl.BlockSpec((d, tn), lambda b, i, j: (0, j))],
        out_specs=pl.BlockSpec((1, tm, tn), lambda b, i, j: (b, i, j)),
        out_shape=jax.ShapeDtypeStruct((bsz, n, D_IN), F32),
        scratch_shapes=[pltpu.VMEM((tm, d), BF16)],
        compiler_params=_cparams("parallel", "parallel", "arbitrary"),
    )(x, shift, scale, w_in_bf16)


def _s5_tables(lam_re, lam_im, log_step, b_re, b_im, c_re, c_im, seg_len):
    lr = jnp.minimum(lam_re.astype(F32), S5_MAX_RE)
    li = lam_im.astype(F32)
    step = jnp.exp(log_step.astype(F32))[..., None]
    ar, ai = lr * step, li * step
    mag = jnp.exp(ar)
    abr, abi = mag * jnp.cos(ai), mag * jnp.sin(ai)
    den = lr * lr + li * li
    bsr = ((abr - 1.0) * lr + abi * li) / den
    bsi = (abi * lr - (abr - 1.0) * li) / den
    btr = bsr[..., None] * b_re[None] - bsi[..., None] * b_im[None]
    bti = bsr[..., None] * b_im[None] + bsi[..., None] * b_re[None]
    eye = jnp.eye(S5_GROUPS // 2, dtype=F32)
    wb, wc, lam, pw = [], [], [], []
    cnt = jnp.arange(1, seg_len + 1, dtype=F32)[:, None, None]
    for d in range(2):
        for hf in range(2):
            gs = slice(hf * 16, hf * 16 + 16)
            wbr = jnp.einsum('gpj,gh->gjhp', btr[d, gs], eye).reshape(S5_HALF, S5_HSTATE)
            wbi = jnp.einsum('gpj,gh->gjhp', bti[d, gs], eye).reshape(S5_HALF, S5_HSTATE)
            wb.append(jnp.concatenate([wbr, wbi], axis=1))
            wcr = jnp.einsum('gip,gh->gphi', c_re[d, gs], eye).reshape(S5_HSTATE, S5_HALF)
            wci = -jnp.einsum('gip,gh->gphi', c_im[d, gs], eye).reshape(S5_HSTATE, S5_HALF)
            wc.append(jnp.concatenate([wcr, wci], axis=0))
            lam.append(jnp.concatenate([abr[d, gs].reshape(1, -1), abi[d, gs].reshape(1, -1)], axis=1))
            pm = jnp.exp(cnt * ar[d, gs][None])
            pa = cnt * ai[d, gs][None]
            pw.append(jnp.concatenate([(pm * jnp.cos(pa)).reshape(seg_len, -1),
                                       (pm * jnp.sin(pa)).reshape(seg_len, -1)], axis=1))
    return (jnp.stack(wb).astype(BF16), jnp.stack(wc).astype(BF16), jnp.stack(lam), jnp.stack(pw))


def _s5_perms(tile, seg_len):
    rows = np.arange(tile)
    tok = (rows % SUBLANES) * seg_len + rows // SUBLANES
    pf = np.zeros((tile, tile), np.float32)
    pf[rows, tok] = 1.0
    pb = np.zeros((tile, tile), np.float32)
    pb[rows, tile - 1 - tok] = 1.0
    perm = np.stack([pf, pb])
    return jnp.asarray(perm, BF16), jnp.asarray(perm.transpose(0, 2, 1), BF16)


def _s5_kernel(uf_ref, ub_ref, p_ref, pt_ref, wb_ref, wc_ref, lam_ref, pw_ref, x0_ref,
               yf_ref, yb_ref, xfin_ref, bu_ref, carry_ref, *, seg_len):
    hs = S5_HSTATE
    lane_chunk = 512

    @pl.when(pl.program_id(1) == 0)
    def _():
        carry_ref[...] = x0_ref[0]

    u_refs = (uf_ref, ub_ref)
    y_refs = (yf_ref, yb_ref)
    units = [(d, hf) for d in range(2) for hf in range(2)]
    n_lc = hs // lane_chunk

    for d, hf in units:
        q = d * 2 + hf
        ub = u_refs[d][0, :, hf * S5_HALF:(hf + 1) * S5_HALF].astype(BF16)
        up = jnp.dot(p_ref[d], ub, preferred_element_type=F32).astype(BF16)
        bu_ref[q] = jnp.dot(up, wb_ref[q], preferred_element_type=F32)

    for d, hf in units:
        q = d * 2 + hf
        xr = [jnp.zeros((SUBLANES, lane_chunk), F32) for _ in range(n_lc)]
        xi = [jnp.zeros((SUBLANES, lane_chunk), F32) for _ in range(n_lc)]
        lam_r = [jnp.broadcast_to(lam_ref[q, :, lc * lane_chunk:(lc + 1) * lane_chunk],
                                  (SUBLANES, lane_chunk)) for lc in range(n_lc)]
        lam_i = [jnp.broadcast_to(lam_ref[q, :, hs + lc * lane_chunk:hs + (lc + 1) * lane_chunk],
                                  (SUBLANES, lane_chunk)) for lc in range(n_lc)]
        for i in range(seg_len):
            rows = slice(i * SUBLANES, (i + 1) * SUBLANES)
            for lc in range(n_lc):
                re_l = slice(lc * lane_chunk, (lc + 1) * lane_chunk)
                im_l = slice(hs + lc * lane_chunk, hs + (lc + 1) * lane_chunk)
                nr = lam_r[lc] * xr[lc] - lam_i[lc] * xi[lc] + bu_ref[q, rows, re_l]
                ni = lam_r[lc] * xi[lc] + lam_i[lc] * xr[lc] + bu_ref[q, rows, im_l]
                bu_ref[q, rows, re_l] = nr
                bu_ref[q, rows, im_l] = ni
                xr[lc], xi[lc] = nr, ni
        end_r = jnp.concatenate(xr, axis=1)
        end_i = jnp.concatenate(xi, axis=1)

        lm_r = pw_ref[q, seg_len - 1:seg_len, 0:hs]
        lm_i = pw_ref[q, seg_len - 1:seg_len, hs:2 * hs]
        cr = carry_ref[q, :, 0:hs]
        ci = carry_ref[q, :, hs:2 * hs]
        cin_rows_r, cin_rows_i = [], []
        for s in range(SUBLANES):
            cin_rows_r.append(cr)
            cin_rows_i.append(ci)
            nr = lm_r * cr - lm_i * ci + end_r[s:s + 1]
            ni = lm_r * ci + lm_i * cr + end_i[s:s + 1]
            cr, ci = nr, ni
        carry_ref[q, :, 0:hs] = cr
        carry_ref[q, :, hs:2 * hs] = ci
        cin_r = jnp.concatenate(cin_rows_r, axis=0)
        cin_i = jnp.concatenate(cin_rows_i, axis=0)

        for i in range(seg_len):
            rows = slice(i * SUBLANES, (i + 1) * SUBLANES)
            for lc in range(n_lc):
                re_l = slice(lc * lane_chunk, (lc + 1) * lane_chunk)
                im_l = slice(hs + lc * lane_chunk, hs + (lc + 1) * lane_chunk)
                pr = pw_ref[q, i:i + 1, re_l]
                pi = pw_ref[q, i:i + 1, im_l]
                bu_ref[q, rows, re_l] += pr * cin_r[:, re_l] - pi * cin_i[:, re_l]
                bu_ref[q, rows, im_l] += pr * cin_i[:, re_l] + pi * cin_r[:, re_l]

    for d, hf in units:
        q = d * 2 + hf
        yp = jnp.dot(bu_ref[q].astype(BF16), wc_ref[q], preferred_element_type=F32)
        y_refs[d][0, :, hf * S5_HALF:(hf + 1) * S5_HALF] = _dot_xr(pt_ref[d], yp)

    xfin_ref[0] = carry_ref[...]


def _s5_scan(z, tabs, x0):
    bsz, n, _ = z.shape
    tile = min(512, n)
    seg_len = tile // SUBLANES
    nk = n // tile
    wb, wc, lam, pw = tabs[seg_len]
    perm, perm_t = _s5_perms(tile, seg_len)
    const4 = lambda b, k: (0, 0, 0)
    return pl.pallas_call(
        functools.partial(_s5_kernel, seg_len=seg_len),
        grid=(bsz, nk),
        in_specs=[pl.BlockSpec((1, tile, S5_WIDTH), lambda b, k: (b, k, ZB_S5_U)),
                  pl.BlockSpec((1, tile, S5_WIDTH), lambda b, k: (b, nk - 1 - k, ZB_S5_U)),
                  pl.BlockSpec((2, tile, tile), const4),
                  pl.BlockSpec((2, tile, tile), const4),
                  pl.BlockSpec((4, S5_HALF, 2 * S5_HSTATE), const4),
                  pl.BlockSpec((4, 2 * S5_HSTATE, S5_HALF), const4),
                  pl.BlockSpec((4, 1, 2 * S5_HSTATE), const4),
                  pl.BlockSpec((4, seg_len, 2 * S5_HSTATE), const4),
                  pl.BlockSpec((1, 4, 1, 2 * S5_HSTATE), lambda b, k: (b, 0, 0, 0))],
        out_specs=[pl.BlockSpec((1, tile, S5_WIDTH), lambda b, k: (b, k, 0)),
                   pl.BlockSpec((1, tile, S5_WIDTH), lambda b, k: (b, nk - 1 - k, 0)),
                   pl.BlockSpec((1, 4, 1, 2 * S5_HSTATE), lambda b, k: (b, 0, 0, 0))],
        out_shape=[jax.ShapeDtypeStruct((bsz, n, S5_WIDTH), F32),
                   jax.ShapeDtypeStruct((bsz, n, S5_WIDTH), F32),
                   jax.ShapeDtypeStruct((bsz, 4, 1, 2 * S5_HSTATE), F32)],
        scratch_shapes=[pltpu.VMEM((4, tile, 2 * S5_HSTATE), F32),
                        pltpu.VMEM((4, 1, 2 * S5_HSTATE), F32)],
        compiler_params=_cparams("parallel", "arbitrary"),
    )(z, z, perm, perm_t, wb, wc, lam, pw, x0)


def _pool_matrices(on_grid, tile):
    mats = []
    for w in POOL_WINDOWS:
        lo = w // 2
        if on_grid:
            rows = tile // GRID_W
            ext_rows = 3 * rows
            m = np.zeros((rows, GRID_W, ext_rows, GRID_W), np.float32)
            for r in range(rows):
                for j in range(GRID_W):
                    r_lo = r + rows - lo
                    j_lo, j_hi = max(j - lo, 0), min(j - lo + w - 1, GRID_W - 1)
                    m[r, j, r_lo:r_lo + w, j_lo:j_hi + 1] = 1.0
            mats.append(m.reshape(tile, ext_rows * GRID_W))
        else:
            m = np.zeros((tile, tile), np.float32)
            for j in range(tile):
                m[j, max(j - lo, 0):min(j - lo + w - 1, tile - 1) + 1] = 1.0
            mats.append(m)
    return jnp.asarray(np.stack(mats), BF16)


def _window_count(idx, n, w):
    lo = w // 2
    hi_i = jnp.minimum(idx - lo + w - 1, n - 1)
    lo_i = jnp.maximum(idx - lo, 0)
    return (hi_i - lo_i + 1).astype(F32)


def _pool_kernel(*refs, on_grid, tile, n_rows):
    if on_grid:
        u_ref, p_ref, n_ref, g_ref, m_ref, wp_ref, sc_ref, o_ref = refs
    else:
        u_ref, g_ref, m_ref, wp_ref, sc_ref, o_ref = refs
    k = pl.program_id(1)
    cur = u_ref[0]
    if on_grid:
        prev = jnp.where(k > 0, p_ref[0], 0.0)
        nxt = jnp.where(k < pl.num_programs(1) - 1, n_ref[0], 0.0)
    tok = k * tile + lax.broadcasted_iota(jnp.int32, (tile, POOL_GROUP), 0)
    for g, w in enumerate(POOL_WINDOWS):
        gl = slice(g * POOL_GROUP, (g + 1) * POOL_GROUP)
        if on_grid:
            ext = jnp.concatenate([prev[:, gl], cur[:, gl], nxt[:, gl]], axis=0)
            row = jnp.right_shift(tok, int(math.log2(GRID_W)))
            col = jnp.bitwise_and(tok, GRID_W - 1)
            cnt = _window_count(row, n_rows, w) * _window_count(col, GRID_W, w)
        else:
            ext = cur[:, gl]
            cnt = _window_count(tok, tile, w)
        hi, lo = _split(ext, 2)
        box2 = jnp.dot(m_ref[g], jnp.concatenate([hi, lo], axis=1), preferred_element_type=F32)
        box = box2[:, 0:POOL_GROUP] + box2[:, POOL_GROUP:2 * POOL_GROUP]
        diff = box / cnt - cur[:, gl]
        y = _dot(diff, wp_ref[g]) * sc_ref[:, gl]
        o_ref[0, :, gl] = y * _silu(g_ref[0, :, gl])


def _pool_branch(z, w_pool, pool_scale, on_grid):
    bsz, n, _ = z.shape
    sc = pool_scale.reshape(1, POOL_WIDTH)
    if on_grid:
        tile = POOL_ROWS * GRID_W
        nk = n // tile
        mats = _pool_matrices(True, tile)
        in_specs = [pl.BlockSpec((1, tile, POOL_WIDTH), lambda b, k: (b, k, ZB_POOL_U)),
                    pl.BlockSpec((1, tile, POOL_WIDTH), lambda b, k: (b, jnp.maximum(k - 1, 0), ZB_POOL_U)),
                    pl.BlockSpec((1, tile, POOL_WIDTH), lambda b, k: (b, jnp.minimum(k + 1, nk - 1), ZB_POOL_U)),
                    pl.BlockSpec((1, tile, POOL_WIDTH), lambda b, k: (b, k, ZB_POOL_G))]
        args = (z, z, z, z)
    else:
        tile = n
        nk = 1
        mats = _pool_matrices(False, tile)
        in_specs = [pl.BlockSpec((1, tile, POOL_WIDTH), lambda b, k: (b, k, ZB_POOL_U)),
                    pl.BlockSpec((1, tile, POOL_WIDTH), lambda b, k: (b, k, ZB_POOL_G))]
        args = (z, z)
    in_specs += [pl.BlockSpec(mats.shape, lambda b, k: (0, 0, 0)),
                 pl.BlockSpec(w_pool.shape, lambda b, k: (0, 0, 0)),
                 pl.BlockSpec((1, POOL_WIDTH), lambda b, k: (0, 0))]
    return pl.pallas_call(
        functools.partial(_pool_kernel, on_grid=on_grid, tile=tile, n_rows=n // GRID_W),
        grid=(bsz, nk),
        in_specs=in_specs,
        out_specs=pl.BlockSpec((1, tile, POOL_WIDTH), lambda b, k: (b, k, 0)),
        out_shape=jax.ShapeDtypeStruct((bsz, n, POOL_WIDTH), F32),
        compiler_params=_cparams("parallel", "parallel"),
    )(*args, mats, w_pool, sc)


def _rw_consts(tile):
    ch = np.arange(RWKV_WIDTH)
    e1 = (ch[:, None] // RWKV_HEAD == np.arange(LANES)[None, :]).astype(np.float32)
    t = np.arange(tile)
    same = (t[:, None] // RW_CHUNK) == (t[None, :] // RW_CHUNK)
    tri = np.stack([same & (t[None, :] <= t[:, None]), same & (t[None, :] >= t[:, None])]).astype(np.float32)
    return jnp.asarray(e1, BF16), jnp.asarray(e1.T, BF16), jnp.asarray(tri, BF16)


def _rw_kernel(zf_ref, zfp_ref, zfn_ref, cf_ref, zb_ref, zbp_ref, zbn_ref, cb_ref,
               conv_ref, w0_ref, w2_ref, a0_ref, a2_ref, kk_ref, ka_ref, rk_ref,
               e1_ref, e2_ref, tri_ref, s0_ref,
               of_ref, ob_ref, bonus_ref, sfin_ref,
               s_ref, rt_ref, at_ref, bt_ref, kt_ref, v_ref, g_ref, *, tile):
    k = pl.program_id(1)
    nk = pl.num_programs(1)
    n_chunks = tile // RW_CHUNK
    w = RWKV_WIDTH

    @pl.when(k == 0)
    def _():
        s_ref[...] = s0_ref[0]

    st_row = lax.broadcasted_iota(jnp.int32, (2 * RW_CHUNK, LANES), 0)
    st_lane = lax.broadcasted_iota(jnp.int32, (2 * RW_CHUNK, LANES), 1)
    chunk_bits = int(math.log2(RW_CHUNK))
    head_mask = (jnp.right_shift(st_row, chunk_bits)
                 == jnp.right_shift(st_lane, chunk_bits)).astype(F32)
    t_row = jnp.bitwise_and(st_row, RW_CHUNK - 1)
    t_col = jnp.bitwise_and(st_lane, RW_CHUNK - 1)
    eye = (st_row == st_lane).astype(F32)

    def seg_sum(x):
        return _dot_xl(_dot_xl(x, e1_ref[...]), e2_ref[...])

    for d in range(2):
        z_ref, zp_ref, zn_ref, c_ref = ((zf_ref, zfp_ref, zfn_ref, cf_ref) if d == 0
                                        else (zb_ref, zbp_ref, zbn_ref, cb_ref))
        tidx = k if d == 0 else nk - 1 - k

        z = z_ref[0]
        halo = jnp.concatenate([jnp.where(tidx > 0, zp_ref[0], 0.0), z,
                                jnp.where(tidx < nk - 1, zn_ref[0], 0.0)], axis=0)
        ext_rows = tile + 2 * SUBLANES
        z_prev = pltpu.roll(halo, 1, axis=0)[SUBLANES:SUBLANES + tile]
        z_next = pltpu.roll(halo, ext_rows - 1, axis=0)[SUBLANES:SUBLANES + tile]
        conv = conv_ref[0:1, :] * z_prev + conv_ref[1:2, :] * z + conv_ref[2:3, :] * z_next
        r = conv[:, 0:w]
        kx = conv[:, w:2 * w]
        v = conv[:, 2 * w:3 * w]

        kk = kx * kk_ref[...]
        ss = _dot_xl(kk * kk, e1_ref[...])
        inv = 1.0 / jnp.maximum(jnp.sqrt(ss), L2_EPS)
        kappa = kk * _dot_xl(inv, e2_ref[...])

        codes = c_ref[0]
        tw = jnp.tanh(codes[:, 0:LANES])
        ac = codes[:, LANES:2 * LANES]
        lw = -RWKV_DECAY_SCALE * jax.nn.sigmoid(w0_ref[d] + _dot(tw, w2_ref[d]))
        a = jax.nn.sigmoid(a0_ref[d] + _dot(ac, a2_ref[d]))
        kd = kx * (1.0 + (a - 1.0) * ka_ref[...])
        alpha = kappa * a

        if d == 0:
            a_other = jax.nn.sigmoid(a0_ref[1] + _dot(ac, a2_ref[1]))
            k_sum = kx * (2.0 + (a + a_other - 2.0) * ka_ref[...])
            bonus_ref[0] = seg_sum(r * k_sum * rk_ref[...]) * v

        cl = _dot_xr(tri_ref[d], lw)
        g = jnp.exp(cl)
        gi = jnp.exp(-cl)
        rt_ref[d] = r * g
        at_ref[d] = -kappa * jnp.exp(cl - lw)
        bt_ref[d] = alpha * gi
        kt_ref[d] = kd * gi
        v_ref[d] = v
        g_ref[d] = g

    tri_s = [(t_col < t_row).astype(F32), (t_col > t_row).astype(F32)]
    tri_i = [(t_col <= t_row).astype(F32), (t_col >= t_row).astype(F32)]
    o_refs = (of_ref, ob_ref)
    half = 2 * RW_CHUNK
    chains = [(d, p) for p in range(RW_PAIRS) for d in range(2)]

    def chunk_body(ci, carry):
        r0 = (pl.multiple_of(ci * RW_CHUNK, RW_CHUNK),
              pl.multiple_of((n_chunks - 1 - ci) * RW_CHUNK, RW_CHUNK))

        def stack(ref, d, p):
            x = ref[d, pl.ds(r0[d], RW_CHUNK), p * LANES:(p + 1) * LANES]
            return (jnp.concatenate([x, x], axis=0) * head_mask).astype(BF16)

        ar = {c: jnp.concatenate([stack(at_ref, *c), stack(rt_ref, *c)], axis=0) for c in chains}
        bk = {c: jnp.concatenate([stack(bt_ref, *c), stack(kt_ref, *c)], axis=0) for c in chains}
        v_st = {c: stack(v_ref, *c) for c in chains}
        gram = {c: _dot_nt(ar[c], bk[c]) for c in chains}
        ar_s = {c: _dot_nt(ar[c], s_ref[c[0], c[1]]) for c in chains}
        l_ab = {c: gram[c][0:half, 0:half] * tri_s[c[0]] for c in chains}
        l_ak = {c: (gram[c][0:half, half:2 * half] * tri_s[c[0]]).astype(BF16) for c in chains}
        m_rbk = {c: jnp.concatenate([gram[c][half:2 * half, 0:half] * tri_i[c[0]],
                                     gram[c][half:2 * half, half:2 * half] * tri_i[c[0]]],
                                    axis=1).astype(BF16) for c in chains}
        rhs = {c: ar_s[c][0:half] + _dot(l_ak[c], v_st[c]) for c in chains}
        inv_m = {c: eye + l_ab[c] for c in chains}
        pw = {c: l_ab[c].astype(BF16) for c in chains}
        for _ in range(int(math.log2(RW_CHUNK)) - 1):
            pw = {c: _dot(pw[c], pw[c]).astype(BF16) for c in chains}
            inv_m = {c: inv_m[c] + _dot(inv_m[c], pw[c]) for c in chains}
        uv = {c: jnp.concatenate([_dot(inv_m[c], rhs[c]).astype(BF16), v_st[c]], axis=0)
              for c in chains}
        for c in chains:
            d, p = c
            ln = slice(p * LANES, (p + 1) * LANES)
            o_st = ar_s[c][half:2 * half] + _dot(m_rbk[c], uv[c])
            o_refs[d][0, pl.ds(r0[d], RW_CHUNK), ln] = o_st[0:RW_CHUNK] + o_st[RW_CHUNK:2 * RW_CHUNK]
        for c in chains:
            d, p = c
            ln = slice(p * LANES, (p + 1) * LANES)
            g_rows = pl.ds(pl.multiple_of(r0[d] + (RW_CHUNK - SUBLANES if d == 0 else 0), SUBLANES),
                           SUBLANES)
            g_pick = SUBLANES - 1 if d == 0 else 0
            ds = _dot_tn(uv[c], bk[c])
            s_ref[d, p] = (s_ref[d, p] + ds) * g_ref[d, g_rows, ln][g_pick:g_pick + 1]
        return carry

    lax.fori_loop(0, n_chunks, chunk_body, 0)

    sfin_ref[0] = s_ref[...]


def _rwkv_scan(z, conv_rkv, prm, s0):
    bsz, n, _ = z.shape
    tile = min(256, n)
    nk = n // tile
    hb = tile // SUBLANES
    nh = n // SUBLANES
    e1, e2, tri = _rw_consts(tile)
    w3 = 3 * RWKV_WIDTH

    def zspecs(tmap):
        return [pl.BlockSpec((1, tile, w3), lambda b, k: (b, tmap(k), ZB_RKV)),
                pl.BlockSpec((1, SUBLANES, w3), lambda b, k: (b, jnp.maximum(tmap(k) * hb - 1, 0), ZB_RKV)),
                pl.BlockSpec((1, SUBLANES, w3), lambda b, k: (b, jnp.minimum((tmap(k) + 1) * hb, nh - 1), ZB_RKV)),
                pl.BlockSpec((1, tile, 2 * LANES), lambda b, k: (b, tmap(k), ZB_CODES))]

    fwd = lambda k: k
    bwd = lambda k: nk - 1 - k
    c2 = lambda b, k: (0, 0)
    c3 = lambda b, k: (0, 0, 0)
    param_specs = [pl.BlockSpec((3, w3), c2),
                   pl.BlockSpec((2, 1, RWKV_WIDTH), c3), pl.BlockSpec((2, LANES, RWKV_WIDTH), c3),
                   pl.BlockSpec((2, 1, RWKV_WIDTH), c3), pl.BlockSpec((2, LANES, RWKV_WIDTH), c3),
                   pl.BlockSpec((1, RWKV_WIDTH), c2), pl.BlockSpec((1, RWKV_WIDTH), c2),
                   pl.BlockSpec((1, RWKV_WIDTH), c2),
                   pl.BlockSpec(e1.shape, c2), pl.BlockSpec(e2.shape, c2), pl.BlockSpec(tri.shape, c3),
                   pl.BlockSpec((1, 2, RW_PAIRS, LANES, LANES), lambda b, k: (b, 0, 0, 0, 0))]
    tok = jax.ShapeDtypeStruct((bsz, n, RWKV_WIDTH), F32)
    scr = pltpu.VMEM((2, tile, RWKV_WIDTH), F32)
    return pl.pallas_call(
        functools.partial(_rw_kernel, tile=tile),
        grid=(bsz, nk),
        in_specs=zspecs(fwd) + zspecs(bwd) + param_specs,
        out_specs=[pl.BlockSpec((1, tile, RWKV_WIDTH), lambda b, k: (b, k, 0)),
                   pl.BlockSpec((1, tile, RWKV_WIDTH), lambda b, k: (b, nk - 1 - k, 0)),
                   pl.BlockSpec((1, tile, RWKV_WIDTH), lambda b, k: (b, k, 0)),
                   pl.BlockSpec((1, 2, RW_PAIRS, LANES, LANES), lambda b, k: (b, 0, 0, 0, 0))],
        out_shape=[tok, tok, tok, jax.ShapeDtypeStruct((bsz, 2, RW_PAIRS, LANES, LANES), F32)],
        scratch_shapes=[pltpu.VMEM((2, RW_PAIRS, LANES, LANES), F32), scr, scr, scr, scr, scr, scr],
        compiler_params=_cparams("parallel", "arbitrary"),
    )(z, z, z, z, z, z, z, z, conv_rkv, *prm, e1, e2, tri, s0)


def _rw_params(w0, w2, a0, a2, k_k, k_a, r_k):
    zeros = jnp.zeros((RWKV_LORA, RWKV_WIDTH), F32)
    w2p = jnp.stack([jnp.concatenate([w2[0], zeros]), jnp.concatenate([zeros, w2[1]])])
    a2p = jnp.stack([jnp.concatenate([a2[0], zeros]), jnp.concatenate([zeros, a2[1]])])
    return (w0.reshape(2, 1, RWKV_WIDTH), w2p, a0.reshape(2, 1, RWKV_WIDTH), a2p,
            k_k.reshape(1, RWKV_WIDTH), k_a.reshape(1, RWKV_WIDTH), r_k.reshape(1, RWKV_WIDTH))


def _out_kernel(x_ref, su_ref, sg_ref, rg_ref, yf_ref, yb_ref, yp_ref, of_ref, ob_ref, bn_ref,
                gate_ref, sd_ref, wg_ref, bg_ref, gw_ref, gb_ref, e1_ref, e2_ref, wo_ref,
                lg_ref, lb_ref, o_ref, *, alpha):
    y = sd_ref[...] * su_ref[0] + yf_ref[0] + yb_ref[0]
    y = _gelu_tanh(y)
    y = y * jax.nn.sigmoid(_dot(y, wg_ref[...]) + bg_ref[...])
    m1 = y * _silu(sg_ref[0])
    o_sum = of_ref[0] + ob_ref[0]

    def seg_mean(t):
        return _dot_xl(_dot_xl(t, e1_ref[...]), e2_ref[...]) * (1.0 / RWKV_HEAD)

    dev = o_sum - seg_mean(o_sum)
    var = seg_mean(dev * dev)
    on = dev * lax.rsqrt(var + GN_EPS) * gw_ref[...] + gb_ref[...]
    m3 = (on + bn_ref[0]) * _silu(rg_ref[0])
    out = (_dot(m1, wo_ref[0:S5_WIDTH, :]) + _dot(yp_ref[0], wo_ref[S5_WIDTH:S5_WIDTH + POOL_WIDTH, :])
           + _dot(m3, wo_ref[S5_WIDTH + POOL_WIDTH:, :]))
    t = alpha * x_ref[0] + gate_ref[0] * out
    mu = jnp.mean(t, axis=-1, keepdims=True)
    tc = t - mu
    var_t = jnp.mean(tc * tc, axis=-1, keepdims=True)
    o_ref[0] = tc * lax.rsqrt(var_t + LN_EPS) * lg_ref[...] + lb_ref[...]


def _out_proj(x, z, yf, yb, ypool, o_f, o_b, bonus, gate, s5_d, w_glu, b_glu, gn_w, gn_b,
              w_out_bf16, ln_g, ln_b, alpha):
    bsz, n, d = x.shape
    tm = min(256, n)
    e1, e2, _ = _rw_consts(RW_CHUNK)
    tok = lambda b, i: (b, i, 0)
    c2 = lambda b, i: (0, 0)

    def zs(width, blk):
        return pl.BlockSpec((1, tm, width), lambda b, i: (b, i, blk))

    return pl.pallas_call(
        functools.partial(_out_kernel, alpha=alpha),
        grid=(bsz, n // tm),
        in_specs=[pl.BlockSpec((1, tm, d), tok),
                  zs(S5_WIDTH, ZB_S5_U), zs(S5_WIDTH, ZB_S5_G), zs(RWKV_WIDTH, ZB_RWKV_G),
                  pl.BlockSpec((1, tm, S5_WIDTH), tok), pl.BlockSpec((1, tm, S5_WIDTH), tok),
                  pl.BlockSpec((1, tm, POOL_WIDTH), tok),
                  pl.BlockSpec((1, tm, RWKV_WIDTH), tok), pl.BlockSpec((1, tm, RWKV_WIDTH), tok),
                  pl.BlockSpec((1, tm, RWKV_WIDTH), tok),
                  pl.BlockSpec((1, 1, d), lambda b, i: (b, 0, 0)),
                  pl.BlockSpec((1, S5_WIDTH), c2), pl.BlockSpec((S5_WIDTH, S5_WIDTH), c2),
                  pl.BlockSpec((1, S5_WIDTH), c2),
                  pl.BlockSpec((1, RWKV_WIDTH), c2), pl.BlockSpec((1, RWKV_WIDTH), c2),
                  pl.BlockSpec(e1.shape, c2), pl.BlockSpec(e2.shape, c2),
                  pl.BlockSpec((d, d), c2),
                  pl.BlockSpec((1, d), c2), pl.BlockSpec((1, d), c2)],
        out_specs=pl.BlockSpec((1, tm, d), tok),
        out_shape=jax.ShapeDtypeStruct((bsz, n, d), F32),
        compiler_params=_cparams("parallel", "parallel"),
    )(x, z, z, z, yf, yb, ypool, o_f, o_b, bonus, gate,
      s5_d.reshape(1, -1), w_glu, b_glu.reshape(1, -1), gn_w.reshape(1, -1), gn_b.reshape(1, -1),
      e1, e2, w_out_bf16, ln_g.reshape(1, -1), ln_b.reshape(1, -1))


def _permute_w_in(w):
    return jnp.concatenate([w[:, 2048:6144], w[:, 0:2048], w[:, 6144:6400]], axis=1).astype(BF16)


def kernel(x, c, ctx, c_ctx, w_ada, b_ada, w_in, conv_rkv, s5_lam_re, s5_lam_im, s5_log_step,
           s5_b_re, s5_b_im, s5_c_re, s5_c_im, s5_d, w_glu, b_glu, w_pool, pool_scale,
           rwkv_w0, rwkv_w2, rwkv_a0, rwkv_a2, rwkv_k_k, rwkv_k_a, rwkv_r_k, gn_w, gn_b,
           w_out, ln_g, ln_b):
    bsz, n, d = x.shape
    n_ctx = ctx.shape[1]
    depth = w_ada.shape[0]
    alpha = (2 * depth) ** 0.25
    assert bsz + 1 <= SUBLANES and n % 512 == 0 and n_ctx % 256 == 0 and n_ctx <= 512

    cond = jnp.zeros((SUBLANES, d), F32).at[0:bsz].set(c.astype(F32)).at[bsz].set(c_ctx.astype(F32))
    mod = _modulation(cond, w_ada, b_ada)

    s5_zero = jnp.zeros((bsz, 4, 1, 2 * S5_HSTATE), F32)
    rw_zero = jnp.zeros((bsz, 2, RW_PAIRS, LANES, LANES), F32)
    seg_lens = sorted({min(512, n) // SUBLANES, min(512, n_ctx) // SUBLANES})

    xc = ctx
    for l in range(depth):
        ctx_out = l < depth - 1
        shift, scale, gate = (mod[l, 0:bsz, i * d:(i + 1) * d].reshape(bsz, 1, d) for i in range(3))
        shift_c, scale_c, gate_c = (jnp.broadcast_to(mod[l, bsz, i * d:(i + 1) * d], (bsz, 1, d))
                                    for i in range(3))
        w_in_l = _permute_w_in(w_in[l])
        w_out_l = w_out[l].astype(BF16)
        s5_tabs = {m: _s5_tables(s5_lam_re[l], s5_lam_im[l], s5_log_step[l], s5_b_re[l], s5_b_im[l],
                                 s5_c_re[l], s5_c_im[l], m) for m in seg_lens}
        rw_prm = _rw_params(rwkv_w0[l], rwkv_w2[l], rwkv_a0[l], rwkv_a2[l],
                            rwkv_k_k[l], rwkv_k_a[l], rwkv_r_k[l])

        zc = _in_proj(xc, shift_c, scale_c, w_in_l)
        z = _in_proj(x, shift, scale, w_in_l)

        yf_c, yb_c, s5_fin = _s5_scan(zc, s5_tabs, s5_zero)
        yf, yb, _ = _s5_scan(z, s5_tabs, s5_fin)

        of_c, ob_c, bonus_c, rw_fin = _rwkv_scan(zc, conv_rkv[l], rw_prm, rw_zero)
        o_f, o_b, bonus, _ = _rwkv_scan(z, conv_rkv[l], rw_prm, rw_fin)

        ypool = _pool_branch(z, w_pool[l], pool_scale[l], on_grid=True)
        tail = (s5_d[l], w_glu[l], b_glu[l], gn_w[l], gn_b[l], w_out_l, ln_g[l], ln_b[l], alpha)
        x_new = _out_proj(x, z, yf, yb, ypool, o_f, o_b, bonus, gate, *tail)
        if ctx_out:
            ypool_c = _pool_branch(zc, w_pool[l], pool_scale[l], on_grid=False)
            xc = _out_proj(xc, zc, yf_c, yb_c, ypool_c, of_c, ob_c, bonus_c, gate_c, *tail)
        x = x_new
    return x
```

```python
import functools
import math

import numpy as np
import jax
import jax.numpy as jnp
from jax import lax
from jax.experimental import pallas as pl
from jax.experimental.pallas import tpu as pltpu

F32 = jnp.float32
BF16 = jnp.bfloat16

D_MODEL = 2048
GRID_W = 64
S5_WIDTH = 512
S5_GROUP = 16
S5_GROUPS = 32
S5_STATE = 64
POOL_WIDTH = 512
POOL_WINDOWS = (2, 4, 8, 16)
POOL_GROUP = 128
RWKV_WIDTH = 1024
RWKV_HEAD = 64
RWKV_HEADS = 16
RWKV_LORA = 64
D_IN = 6400
RWKV_DECAY_SCALE = 0.606531
S5_MAX_RE = -1e-4
ADALN_EPS = 1e-6
LN_EPS = 1e-5
GN_EPS = 64e-5
L2_EPS = 1e-12

ZB_RKV = 0
ZB_RWKV_G = 3
ZB_S5_U = 8
ZB_S5_G = 9
ZB_POOL_U = 10
ZB_POOL_G = 11
ZB_CODES = 24

LANES = 128
SUBLANES = 8
VMEM_LIMIT = 56 * 1024 * 1024

S5_HALF = 256
S5_HSTATE = 1024
RW_CHUNK = 64
RW_PAIRS = RWKV_HEADS // 2
POOL_ROWS = 8


def _cparams(*sem):
    return pltpu.CompilerParams(dimension_semantics=sem, vmem_limit_bytes=VMEM_LIMIT)


def _dot(a, b):
    return jnp.dot(a.astype(BF16), b.astype(BF16), preferred_element_type=F32)


def _dot_nt(a, b):
    return lax.dot_general(a.astype(BF16), b.astype(BF16), (((1,), (1,)), ((), ())),
                           preferred_element_type=F32)


def _dot_tn(a, b):
    return lax.dot_general(a.astype(BF16), b.astype(BF16), (((0,), (0,)), ((), ())),
                           preferred_element_type=F32)


def _split(x, parts):
    out = []
    rem = x
    for _ in range(parts):
        hi = rem.astype(BF16)
        out.append(hi)
        rem = rem - hi.astype(F32)
    return out


def _dot_xl(a, b, parts=2):
    acc = None
    for t in _split(a, parts):
        p = jnp.dot(t, b, preferred_element_type=F32)
        acc = p if acc is None else acc + p
    return acc


def _dot_xr(a, b, parts=2):
    acc = None
    for t in _split(b, parts):
        p = jnp.dot(a, t, preferred_element_type=F32)
        acc = p if acc is None else acc + p
    return acc


def _silu(x):
    return x * jax.nn.sigmoid(x)


def _gelu_tanh(x):
    return 0.5 * x * (1.0 + jnp.tanh(math.sqrt(2.0 / math.pi) * (x + 0.044715 * (x * x * x))))


def _mod_kernel(c_ref, w_ref, b_ref, o_ref):
    s = _silu(c_ref[...])
    o_ref[0] = jnp.dot(s, w_ref[0], preferred_element_type=F32,
                       precision=lax.Precision.HIGHEST) + b_ref[0]


def _modulation(cond, w_ada, b_ada):
    depth, d, d3 = w_ada.shape
    tn = 512
    return pl.pallas_call(
        _mod_kernel,
        grid=(depth, d3 // tn),
        in_specs=[pl.BlockSpec((SUBLANES, d), lambda l, j: (0, 0)),
                  pl.BlockSpec((1, d, tn), lambda l, j: (l, 0, j)),
                  pl.BlockSpec((1, 1, tn), lambda l, j: (l, 0, j))],
        out_specs=pl.BlockSpec((1, SUBLANES, tn), lambda l, j: (l, 0, j)),
        out_shape=jax.ShapeDtypeStruct((depth, SUBLANES, d3), F32),
        compiler_params=_cparams("parallel", "parallel"),
    )(cond, w_ada, b_ada.reshape(depth, 1, d3))


def _in_kernel(x_ref, sh_ref, sc_ref, w_ref, o_ref, h_ref):
    @pl.when(pl.program_id(2) == 0)
    def _():
        x = x_ref[0]
        mu = jnp.mean(x, axis=-1, keepdims=True)
        xc = x - mu
        var = jnp.mean(xc * xc, axis=-1, keepdims=True)
        h = xc * lax.rsqrt(var + ADALN_EPS) * (1.0 + sc_ref[0]) + sh_ref[0]
        h_ref[...] = h.astype(BF16)

    o_ref[0] = jnp.dot(h_ref[...], w_ref[...], preferred_element_type=F32)


def _in_proj(x, shift, scale, w_in_bf16):
    bsz, n, d = x.shape
    tm = min(1024, n)
    tn = 1280
    return pl.pallas_call(
        _in_kernel,
        grid=(bsz, n // tm, D_IN // tn),
        in_specs=[pl.BlockSpec((1, tm, d), lambda b, i, j: (b, i, 0)),
                  pl.BlockSpec((1, 1, d), lambda b, i, j: (b, 0, 0)),
                  pl.BlockSpec((1, 1, d), lambda b, i, j: (b, 0, 0)),
                  pl.BlockSpec((d, tn), lambda b, i, j: (0, j))],
        out_specs=pl.BlockSpec((1, tm, tn), lambda b, i, j: (b, i, j)),
        out_shape=jax.ShapeDtypeStruct((bsz, n, D_IN), F32),
        scratch_shapes=[pltpu.VMEM((tm, d), BF16)],
        compiler_params=_cparams("parallel", "parallel", "arbitrary"),
    )(x, shift, scale, w_in_bf16)


def _s5_tables(lam_re, lam_im, log_step, b_re, b_im, c_re, c_im, seg_len):
    lr = jnp.minimum(lam_re.astype(F32), S5_MAX_RE)
    li = lam_im.astype(F32)
    step = jnp.exp(log_step.astype(F32))[..., None]
    ar, ai = lr * step, li * step
    mag = jnp.exp(ar)
    abr, abi = mag * jnp.cos(ai), mag * jnp.sin(ai)
    den = lr * lr + li * li
    bsr = ((abr - 1.0) * lr + abi * li) / den
    bsi = (abi * lr - (abr - 1.0) * li) / den
    btr = bsr[..., None] * b_re[None] - bsi[..., None] * b_im[None]
    bti = bsr[..., None] * b_im[None] + bsi[..., None] * b_re[None]
    eye = jnp.eye(S5_GROUPS // 2, dtype=F32)
    wb, wc, lam, pw = [], [], [], []
    cnt = jnp.arange(1, seg_len + 1, dtype=F32)[:, None, None]
    for d in range(2):
        for hf in range(2):
            gs = slice(hf * 16, hf * 16 + 16)
            wbr = jnp.einsum('gpj,gh->gjhp', btr[d, gs], eye).reshape(S5_HALF, S5_HSTATE)
            wbi = jnp.einsum('gpj,gh->gjhp', bti[d, gs], eye).reshape(S5_HALF, S5_HSTATE)
            wb.append(jnp.concatenate([wbr, wbi], axis=1))
            wcr = jnp.einsum('gip,gh->gphi', c_re[d, gs], eye).reshape(S5_HSTATE, S5_HALF)
            wci = -jnp.einsum('gip,gh->gphi', c_im[d, gs], eye).reshape(S5_HSTATE, S5_HALF)
            wc.append(jnp.concatenate([wcr, wci], axis=0))
            lam.append(jnp.concatenate([abr[d, gs].reshape(1, -1), abi[d, gs].reshape(1, -1)], axis=1))
            pm = jnp.exp(cnt * ar[d, gs][None])
            pa = cnt * ai[d, gs][None]
            pw.append(jnp.concatenate([(pm * jnp.cos(pa)).reshape(seg_len, -1),
                                       (pm * jnp.sin(pa)).reshape(seg_len, -1)], axis=1))
    return (jnp.stack(wb).astype(BF16), jnp.stack(wc).astype(BF16), jnp.stack(lam), jnp.stack(pw))


def _s5_perms(tile, seg_len):
    rows = np.arange(tile)
    tok = (rows % SUBLANES) * seg_len + rows // SUBLANES
    pf = np.zeros((tile, tile), np.float32)
    pf[rows, tok] = 1.0
    pb = np.zeros((tile, tile), np.float32)
    pb[rows, tile - 1 - tok] = 1.0
    perm = np.stack([pf, pb])
    return jnp.asarray(perm, BF16), jnp.asarray(perm.transpose(0, 2, 1), BF16)


def _s5_kernel(uf_ref, ub_ref, p_ref, pt_ref, wb_ref, wc_ref, lam_ref, pw_ref, x0_ref,
               yf_ref, yb_ref, xfin_ref, bu_ref, carry_ref, *, seg_len):
    hs = S5_HSTATE
    lane_chunk = 512

    @pl.when(pl.program_id(1) == 0)
    def _():
        carry_ref[...] = x0_ref[0]

    u_refs = (uf_ref, ub_ref)
    y_refs = (yf_ref, yb_ref)
    units = [(d, hf) for d in range(2) for hf in range(2)]
    n_lc = hs // lane_chunk

    for d, hf in units:
        q = d * 2 + hf
        ub = u_refs[d][0, :, hf * S5_HALF:(hf + 1) * S5_HALF].astype(BF16)
        up = jnp.dot(p_ref[d], ub, preferred_element_type=F32).astype(BF16)
        bu_ref[q] = jnp.dot(up, wb_ref[q], preferred_element_type=F32)

    for d, hf in units:
        q = d * 2 + hf
        xr = [jnp.zeros((SUBLANES, lane_chunk), F32) for _ in range(n_lc)]
        xi = [jnp.zeros((SUBLANES, lane_chunk), F32) for _ in range(n_lc)]
        lam_r = [jnp.broadcast_to(lam_ref[q, :, lc * lane_chunk:(lc + 1) * lane_chunk],
                                  (SUBLANES, lane_chunk)) for lc in range(n_lc)]
        lam_i = [jnp.broadcast_to(lam_ref[q, :, hs + lc * lane_chunk:hs + (lc + 1) * lane_chunk],
                                  (SUBLANES, lane_chunk)) for lc in range(n_lc)]
        for i in range(seg_len):
            rows = slice(i * SUBLANES, (i + 1) * SUBLANES)
            for lc in range(n_lc):
                re_l = slice(lc * lane_chunk, (lc + 1) * lane_chunk)
                im_l = slice(hs + lc * lane_chunk, hs + (lc + 1) * lane_chunk)
                nr = lam_r[lc] * xr[lc] - lam_i[lc] * xi[lc] + bu_ref[q, rows, re_l]
                ni = lam_r[lc] * xi[lc] + lam_i[lc] * xr[lc] + bu_ref[q, rows, im_l]
                bu_ref[q, rows, re_l] = nr
                bu_ref[q, rows, im_l] = ni
                xr[lc], xi[lc] = nr, ni
        end_r = jnp.concatenate(xr, axis=1)
        end_i = jnp.concatenate(xi, axis=1)

        lm_r = pw_ref[q, seg_len - 1:seg_len, 0:hs]
        lm_i = pw_ref[q, seg_len - 1:seg_len, hs:2 * hs]
        cr = carry_ref[q, :, 0:hs]
        ci = carry_ref[q, :, hs:2 * hs]
        cin_rows_r, cin_rows_i = [], []
        for s in range(SUBLANES):
            cin_rows_r.append(cr)
            cin_rows_i.append(ci)
            nr = lm_r * cr - lm_i * ci + end_r[s:s + 1]
            ni = lm_r * ci + lm_i * cr + end_i[s:s + 1]
            cr, ci = nr, ni
        carry_ref[q, :, 0:hs] = cr
        carry_ref[q, :, hs:2 * hs] = ci
        cin_r = jnp.concatenate(cin_rows_r, axis=0)
        cin_i = jnp.concatenate(cin_rows_i, axis=0)

        for i in range(seg_len):
            rows = slice(i * SUBLANES, (i + 1) * SUBLANES)
            for lc in range(n_lc):
                re_l = slice(lc * lane_chunk, (lc + 1) * lane_chunk)
                im_l = slice(hs + lc * lane_chunk, hs + (lc + 1) * lane_chunk)
                pr = pw_ref[q, i:i + 1, re_l]
                pi = pw_ref[q, i:i + 1, im_l]
                bu_ref[q, rows, re_l] += pr * cin_r[:, re_l] - pi * cin_i[:, re_l]
                bu_ref[q, rows, im_l] += pr * cin_i[:, re_l] + pi * cin_r[:, re_l]

    for d, hf in units:
        q = d * 2 + hf
        yp = jnp.dot(bu_ref[q].astype(BF16), wc_ref[q], preferred_element_type=F32)
        y_refs[d][0, :, hf * S5_HALF:(hf + 1) * S5_HALF] = _dot_xr(pt_ref[d], yp)

    xfin_ref[0] = carry_ref[...]


def _s5_scan(z, tabs, x0):
    bsz, n, _ = z.shape
    tile = min(512, n)
    seg_len = tile // SUBLANES
    nk = n // tile
    wb, wc, lam, pw = tabs[seg_len]
    perm, perm_t = _s5_perms(tile, seg_len)
    const4 = lambda b, k: (0, 0, 0)
    return pl.pallas_call(
        functools.partial(_s5_kernel, seg_len=seg_len),
        grid=(bsz, nk),
        in_specs=[pl.BlockSpec((1, tile, S5_WIDTH), lambda b, k: (b, k, ZB_S5_U)),
                  pl.BlockSpec((1, tile, S5_WIDTH), lambda b, k: (b, nk - 1 - k, ZB_S5_U)),
                  pl.BlockSpec((2, tile, tile), const4),
                  pl.BlockSpec((2, tile, tile), const4),
                  pl.BlockSpec((4, S5_HALF, 2 * S5_HSTATE), const4),
                  pl.BlockSpec((4, 2 * S5_HSTATE, S5_HALF), const4),
                  pl.BlockSpec((4, 1, 2 * S5_HSTATE), const4),
                  pl.BlockSpec((4, seg_len, 2 * S5_HSTATE), const4),
                  pl.BlockSpec((1, 4, 1, 2 * S5_HSTATE), lambda b, k: (b, 0, 0, 0))],
        out_specs=[pl.BlockSpec((1, tile, S5_WIDTH), lambda b, k: (b, k, 0)),
                   pl.BlockSpec((1, tile, S5_WIDTH), lambda b, k: (b, nk - 1 - k, 0)),
                   pl.BlockSpec((1, 4, 1, 2 * S5_HSTATE), lambda b, k: (b, 0, 0, 0))],
        out_shape=[jax.ShapeDtypeStruct((bsz, n, S5_WIDTH), F32),
                   jax.ShapeDtypeStruct((bsz, n, S5_WIDTH), F32),
                   jax.ShapeDtypeStruct((bsz, 4, 1, 2 * S5_HSTATE), F32)],
        scratch_shapes=[pltpu.VMEM((4, tile, 2 * S5_HSTATE), F32),
                        pltpu.VMEM((4, 1, 2 * S5_HSTATE), F32)],
        compiler_params=_cparams("parallel", "arbitrary"),
    )(z, z, perm, perm_t, wb, wc, lam, pw, x0)


def _pool_matrices(on_grid, tile):
    mats = []
    for w in POOL_WINDOWS:
        lo = w // 2
        if on_grid:
            rows = tile // GRID_W
            ext_rows = 3 * rows
            m = np.zeros((rows, GRID_W, ext_rows, GRID_W), np.float32)
            for r in range(rows):
                for j in range(GRID_W):
                    r_lo = r + rows - lo
                    j_lo, j_hi = max(j - lo, 0), min(j - lo + w - 1, GRID_W - 1)
                    m[r, j, r_lo:r_lo + w, j_lo:j_hi + 1] = 1.0
            mats.append(m.reshape(tile, ext_rows * GRID_W))
        else:
            m = np.zeros((tile, tile), np.float32)
            for j in range(tile):
                m[j, max(j - lo, 0):min(j - lo + w - 1, tile - 1) + 1] = 1.0
            mats.append(m)
    return jnp.asarray(np.stack(mats), BF16)


def _window_count(idx, n, w):
    lo = w // 2
    hi_i = jnp.minimum(idx - lo + w - 1, n - 1)
    lo_i = jnp.maximum(idx - lo, 0)
    return (hi_i - lo_i + 1).astype(F32)


def _pool_kernel(*refs, on_grid, tile, n_rows):
    if on_grid:
        u_ref, p_ref, n_ref, g_ref, m_ref, wp_ref, sc_ref, o_ref = refs
    else:
        u_ref, g_ref, m_ref, wp_ref, sc_ref, o_ref = refs
    k = pl.program_id(1)
    cur = u_ref[0]
    if on_grid:
        prev = jnp.where(k > 0, p_ref[0], 0.0)
        nxt = jnp.where(k < pl.num_programs(1) - 1, n_ref[0], 0.0)
    tok = k * tile + lax.broadcasted_iota(jnp.int32, (tile, POOL_GROUP), 0)
    for g, w in enumerate(POOL_WINDOWS):
        gl = slice(g * POOL_GROUP, (g + 1) * POOL_GROUP)
        if on_grid:
            ext = jnp.concatenate([prev[:, gl], cur[:, gl], nxt[:, gl]], axis=0)
            row = jnp.right_shift(tok, int(math.log2(GRID_W)))
            col = jnp.bitwise_and(tok, GRID_W - 1)
            cnt = _window_count(row, n_rows, w) * _window_count(col, GRID_W, w)
        else:
            ext = cur[:, gl]
            cnt = _window_count(tok, tile, w)
        hi, lo = _split(ext, 2)
        box2 = jnp.dot(m_ref[g], jnp.concatenate([hi, lo], axis=1), preferred_element_type=F32)
        box = box2[:, 0:POOL_GROUP] + box2[:, POOL_GROUP:2 * POOL_GROUP]
        diff = box / cnt - cur[:, gl]
        y = _dot(diff, wp_ref[g]) * sc_ref[:, gl]
        o_ref[0, :, gl] = y * _silu(g_ref[0, :, gl])


def _pool_branch(z, w_pool, pool_scale, on_grid):
    bsz, n, _ = z.shape
    sc = pool_scale.reshape(1, POOL_WIDTH)
    if on_grid:
        tile = POOL_ROWS * GRID_W
        nk = n // tile
        mats = _pool_matrices(True, tile)
        in_specs = [pl.BlockSpec((1, tile, POOL_WIDTH), lambda b, k: (b, k, ZB_POOL_U)),
                    pl.BlockSpec((1, tile, POOL_WIDTH), lambda b, k: (b, jnp.maximum(k - 1, 0), ZB_POOL_U)),
                    pl.BlockSpec((1, tile, POOL_WIDTH), lambda b, k: (b, jnp.minimum(k + 1, nk - 1), ZB_POOL_U)),
                    pl.BlockSpec((1, tile, POOL_WIDTH), lambda b, k: (b, k, ZB_POOL_G))]
        args = (z, z, z, z)
    else:
        tile = n
        nk = 1
        mats = _pool_matrices(False, tile)
        in_specs = [pl.BlockSpec((1, tile, POOL_WIDTH), lambda b, k: (b, k, ZB_POOL_U)),
                    pl.BlockSpec((1, tile, POOL_WIDTH), lambda b, k: (b, k, ZB_POOL_G))]
        args = (z, z)
    in_specs += [pl.BlockSpec(mats.shape, lambda b, k: (0, 0, 0)),
                 pl.BlockSpec(w_pool.shape, lambda b, k: (0, 0, 0)),
                 pl.BlockSpec((1, POOL_WIDTH), lambda b, k: (0, 0))]
    return pl.pallas_call(
        functools.partial(_pool_kernel, on_grid=on_grid, tile=tile, n_rows=n // GRID_W),
        grid=(bsz, nk),
        in_specs=in_specs,
        out_specs=pl.BlockSpec((1, tile, POOL_WIDTH), lambda b, k: (b, k, 0)),
        out_shape=jax.ShapeDtypeStruct((bsz, n, POOL_WIDTH), F32),
        compiler_params=_cparams("parallel", "parallel"),
    )(*args, mats, w_pool, sc)


def _rw_consts(tile):
    ch = np.arange(RWKV_WIDTH)
    e1 = (ch[:, None] // RWKV_HEAD == np.arange(LANES)[None, :]).astype(np.float32)
    t = np.arange(tile)
    same = (t[:, None] // RW_CHUNK) == (t[None, :] // RW_CHUNK)
    tri = np.stack([same & (t[None, :] <= t[:, None]), same & (t[None, :] >= t[:, None])]).astype(np.float32)
    return jnp.asarray(e1, BF16), jnp.asarray(e1.T, BF16), jnp.asarray(tri, BF16)


def _rw_kernel(zf_ref, zfp_ref, zfn_ref, cf_ref, zb_ref, zbp_ref, zbn_ref, cb_ref,
               conv_ref, w0_ref, w2_ref, a0_ref, a2_ref, kk_ref, ka_ref, rk_ref,
               e1_ref, e2_ref, tri_ref, s0_ref,
               of_ref, ob_ref, bonus_ref, sfin_ref,
               s_ref, ar_ref, bk_ref, v_ref, g_ref, *, tile, nk):
    k = pl.program_id(1)
    n_chunks = tile // RW_CHUNK
    w = RWKV_WIDTH
    slot = lax.rem(k, 2)
    cslot = 1 - slot
    k_in = jnp.minimum(k, nk - 1)

    @pl.when(k == 0)
    def _():
        ar_ref[1] = jnp.zeros(ar_ref.shape[1:], BF16)
        bk_ref[1] = jnp.zeros(bk_ref.shape[1:], BF16)
        v_ref[1] = jnp.zeros(v_ref.shape[1:], BF16)
        g_ref[1] = jnp.zeros(g_ref.shape[1:], F32)

    @pl.when(k <= 1)
    def _():
        s_ref[...] = s0_ref[0]

    st_row = lax.broadcasted_iota(jnp.int32, (2 * RW_CHUNK, LANES), 0)
    st_lane = lax.broadcasted_iota(jnp.int32, (2 * RW_CHUNK, LANES), 1)
    chunk_bits = int(math.log2(RW_CHUNK))
    head_mask = (jnp.right_shift(st_row, chunk_bits)
                 == jnp.right_shift(st_lane, chunk_bits)).astype(F32).astype(BF16)
    t_row = lax.broadcasted_iota(jnp.int32, (RW_CHUNK, LANES), 0)
    s_col = jnp.bitwise_and(lax.broadcasted_iota(jnp.int32, (RW_CHUNK, LANES), 1), RW_CHUNK - 1)
    eye = (t_row == s_col).astype(F32)

    def prepare(d):
        z_ref, zp_ref, zn_ref, c_ref = ((zf_ref, zfp_ref, zfn_ref, cf_ref) if d == 0
                                        else (zb_ref, zbp_ref, zbn_ref, cb_ref))
        tidx = k_in if d == 0 else nk - 1 - k_in

        z = z_ref[0]
        halo = jnp.concatenate([jnp.where(tidx > 0, zp_ref[0], 0.0), z,
                                jnp.where(tidx < nk - 1, zn_ref[0], 0.0)], axis=0)
        ext_rows = tile + 2 * SUBLANES
        z_prev = pltpu.roll(halo, 1, axis=0)[SUBLANES:SUBLANES + tile]
        z_next = pltpu.roll(halo, ext_rows - 1, axis=0)[SUBLANES:SUBLANES + tile]
        conv = conv_ref[0:1, :] * z_prev + conv_ref[1:2, :] * z + conv_ref[2:3, :] * z_next
        r = conv[:, 0:w]
        kx = conv[:, w:2 * w]
        v = conv[:, 2 * w:3 * w]

        kk = kx * kk_ref[...]
        ss = _dot_xl(kk * kk, e1_ref[...])
        codes = c_ref[0]
        tw = jnp.tanh(codes[:, 0:LANES])
        ac = codes[:, LANES:2 * LANES]
        w_pre = _dot(tw, w2_ref[d])
        a_pre = _dot(ac, a2_ref[d])
        if d == 0:
            a_other_pre = _dot(ac, a2_ref[1])
        yield

        inv = 1.0 / jnp.maximum(jnp.sqrt(ss), L2_EPS)
        kappa_scale = _dot_xl(inv, e2_ref[...])
        lw = -RWKV_DECAY_SCALE * jax.nn.sigmoid(w0_ref[d] + w_pre)
        cl = _dot_xr(tri_ref[d], lw)
        a = jax.nn.sigmoid(a0_ref[d] + a_pre)
        if d == 0:
            a_other = jax.nn.sigmoid(a0_ref[1] + a_other_pre)
            k_sum = kx * (2.0 + (a + a_other - 2.0) * ka_ref[...])
            bonus_heads = _dot_xl(r * k_sum * rk_ref[...], e1_ref[...])
        yield

        kappa = kk * kappa_scale
        kd = kx * (1.0 + (a - 1.0) * ka_ref[...])
        alpha = kappa * a
        if d == 0:
            bonus_ref[0] = _dot_xl(bonus_heads, e2_ref[...]) * v
        g = jnp.exp(cl)
        gi = jnp.exp(-cl)
        a_t = (-kappa * jnp.exp(cl - lw)).astype(BF16)
        r_t = (r * g).astype(BF16)
        b_t = (alpha * gi).astype(BF16)
        k_t = (kd * gi).astype(BF16)
        v_t = v.astype(BF16)
        yield

        for c in range(n_chunks):
            rs = slice(c * RW_CHUNK, (c + 1) * RW_CHUNK)
            ar_ref[slot, d, c, 0:RW_CHUNK, :] = a_t[rs]
            ar_ref[slot, d, c, RW_CHUNK:2 * RW_CHUNK, :] = r_t[rs]
            bk_ref[slot, d, c, 0:RW_CHUNK, :] = b_t[rs]
            bk_ref[slot, d, c, RW_CHUNK:2 * RW_CHUNK, :] = k_t[rs]
            g0 = c * RW_CHUNK + (RW_CHUNK - SUBLANES if d == 0 else 0)
            g_ref[slot, d, c * SUBLANES:(c + 1) * SUBLANES, :] = g[g0:g0 + SUBLANES]
        v_ref[slot, d] = v_t
        yield

    tri_s =[(s_col < t_row).astype(F32), (s_col > t_row).astype(F32)]
    tri_i = [(s_col <= t_row).astype(F32), (s_col >= t_row).astype(F32)]
    o_refs = (of_ref, ob_ref)
    tc = RW_CHUNK
    chains = [(d, p) for p in range(RW_PAIRS) for d in range(2)]

    def block_diag(x):
        xb = x.astype(BF16)
        return jnp.concatenate([xb, xb], axis=0) * head_mask

    def recur(ci):
        cidx = (ci, n_chunks - 1 - ci)

        def lanes(c):
            return slice(c[1] * LANES, (c[1] + 1) * LANES)

        ar = {c: ar_ref[cslot, c[0], cidx[c[0]], :, lanes(c)] for c in chains}
        bk64 = {c: bk_ref[cslot, c[0], cidx[c[0]], :, lanes(c)] for c in chains}
        bk = {c: jnp.concatenate([block_diag(bk64[c][0:tc]), block_diag(bk64[c][tc:2 * tc])], axis=0)
              for c in chains}
        v_st = {c: block_diag(v_ref[cslot, c[0], cidx[c[0]] * tc:(cidx[c[0]] + 1) * tc, lanes(c)])
                for c in chains}
        gram = {c: _dot_nt(ar[c], bk[c]) for c in chains}
        ar_s = {c: _dot_nt(ar[c], s_ref[c[0], c[1]]) for c in chains}
        l_ab = {c: gram[c][0:tc, 0:LANES] * tri_s[c[0]] for c in chains}
        l_ak = {c: (gram[c][0:tc, LANES:2 * LANES] * tri_s[c[0]]).astype(BF16) for c in chains}
        m_rbk = {c: jnp.concatenate([gram[c][tc:2 * tc, 0:LANES] * tri_i[c[0]],
                                     gram[c][tc:2 * tc, LANES:2 * LANES] * tri_i[c[0]]],
                                    axis=1).astype(BF16) for c in chains}
        rhs = {c: ar_s[c][0:tc] + _dot(l_ak[c], v_st[c]) for c in chains}
        inv_m = {c: eye + l_ab[c] for c in chains}
        pw = {c: _dot(l_ab[c], block_diag(l_ab[c])) for c in chains}
        n_sq = int(math.log2(RW_CHUNK))
        for j in range(1, n_sq - 1):
            both = {c: _dot(jnp.concatenate([pw[c], inv_m[c]], axis=0), block_diag(pw[c])) for c in chains}
            pw = {c: both[c][0:tc] for c in chains}
            inv_m = {c: inv_m[c] + both[c][tc:2 * tc] for c in chains}
        inv_m = {c: inv_m[c] + _dot(inv_m[c], block_diag(pw[c])) for c in chains}
        uv ={c: jnp.concatenate([block_diag(_dot(inv_m[c], block_diag(rhs[c]))), v_st[c]], axis=0)
              for c in chains}
        for c in chains:
            d = c[0]
            o_refs[d][0, cidx[d] * tc:(cidx[d] + 1) * tc, lanes(c)] = (
                ar_s[c][tc:2 * tc] + _dot(m_rbk[c], uv[c]))
        for c in chains:
            d, p = c
            g_pick = SUBLANES - 1 if d == 0 else 0
            g_tot = g_ref[cslot, d, cidx[d] * SUBLANES:(cidx[d] + 1) * SUBLANES, lanes(c)]
            ds = _dot_tn(uv[c], bk[c])
            s_ref[d, p] = (s_ref[d, p] + ds) * g_tot[g_pick:g_pick + 1]

    prep = [prepare(0), prepare(1)]
    n_prep_stages = 4
    assert n_chunks >= n_prep_stages - 1
    for ci in range(n_chunks):
        if ci < n_prep_stages - 1:
            for stage in prep:
                next(stage)
        recur(ci)
    for stage in prep:
        next(stage)
        assert next(stage, None) is None

    @pl.when(k == nk)
    def _():
        sfin_ref[0] = s_ref[...]


def _rwkv_scan(z, conv_rkv, prm, s0):
    bsz, n, _ = z.shape
    tile = min(256, n)
    nk = n // tile
    hb = tile // SUBLANES
    nh = n // SUBLANES
    e1, e2, tri = _rw_consts(tile)
    w3 = 3 * RWKV_WIDTH

    def zspecs(tmap):
        return [pl.BlockSpec((1, tile, w3), lambda b, k: (b, tmap(k), ZB_RKV)),
                pl.BlockSpec((1, SUBLANES, w3), lambda b, k: (b, jnp.maximum(tmap(k) * hb - 1, 0), ZB_RKV)),
                pl.BlockSpec((1, SUBLANES, w3), lambda b, k: (b, jnp.minimum((tmap(k) + 1) * hb, nh - 1), ZB_RKV)),
                pl.BlockSpec((1, tile, 2 * LANES), lambda b, k: (b, tmap(k), ZB_CODES))]

    fwd = lambda k: jnp.minimum(k, nk - 1)
    bwd = lambda k: nk - 1 - jnp.minimum(k, nk - 1)
    done = lambda k: jnp.maximum(k - 1, 0)
    c2 = lambda b, k: (0, 0)
    c3 = lambda b, k: (0, 0, 0)
    param_specs = [pl.BlockSpec((3, w3), c2),
                   pl.BlockSpec((2, 1, RWKV_WIDTH), c3), pl.BlockSpec((2, LANES, RWKV_WIDTH), c3),
                   pl.BlockSpec((2, 1, RWKV_WIDTH), c3), pl.BlockSpec((2, LANES, RWKV_WIDTH), c3),
                   pl.BlockSpec((1, RWKV_WIDTH), c2), pl.BlockSpec((1, RWKV_WIDTH), c2),
                   pl.BlockSpec((1, RWKV_WIDTH), c2),
                   pl.BlockSpec(e1.shape, c2), pl.BlockSpec(e2.shape, c2), pl.BlockSpec(tri.shape, c3),
                   pl.BlockSpec((1, 2, RW_PAIRS, LANES, LANES), lambda b, k: (b, 0, 0, 0, 0))]
    tok = jax.ShapeDtypeStruct((bsz, n, RWKV_WIDTH), F32)
    n_chunks = tile // RW_CHUNK
    stacked = pltpu.VMEM((2, 2, n_chunks, 2 * RW_CHUNK, RWKV_WIDTH), BF16)
    return pl.pallas_call(
        functools.partial(_rw_kernel, tile=tile, nk=nk),
        grid=(bsz, nk + 1),
        in_specs=zspecs(fwd) + zspecs(bwd) + param_specs,
        out_specs=[pl.BlockSpec((1, tile, RWKV_WIDTH), lambda b, k: (b, done(k), 0)),
                   pl.BlockSpec((1, tile, RWKV_WIDTH), lambda b, k: (b, nk - 1 - done(k), 0)),
                   pl.BlockSpec((1, tile, RWKV_WIDTH), lambda b, k: (b, fwd(k), 0)),
                   pl.BlockSpec((1, 2, RW_PAIRS, LANES, LANES), lambda b, k: (b, 0, 0, 0, 0))],
        out_shape=[tok, tok, tok, jax.ShapeDtypeStruct((bsz, 2, RW_PAIRS, LANES, LANES), F32)],
        scratch_shapes=[pltpu.VMEM((2, RW_PAIRS, LANES, LANES), F32), stacked, stacked,
                        pltpu.VMEM((2, 2, tile, RWKV_WIDTH), BF16),
                        pltpu.VMEM((2, 2, n_chunks * SUBLANES, RWKV_WIDTH), F32)],
        compiler_params=_cparams("parallel", "arbitrary"),
    )(z, z, z, z, z, z, z, z, conv_rkv, *prm, e1, e2, tri, s0)


def _rw_params(w0, w2, a0, a2, k_k, k_a, r_k):
    zeros = jnp.zeros((RWKV_LORA, RWKV_WIDTH), F32)
    w2p = jnp.stack([jnp.concatenate([w2[0], zeros]), jnp.concatenate([zeros, w2[1]])])
    a2p = jnp.stack([jnp.concatenate([a2[0], zeros]), jnp.concatenate([zeros, a2[1]])])
    return (w0.reshape(2, 1, RWKV_WIDTH), w2p, a0.reshape(2, 1, RWKV_WIDTH), a2p,
            k_k.reshape(1, RWKV_WIDTH), k_a.reshape(1, RWKV_WIDTH), r_k.reshape(1, RWKV_WIDTH))


def _out_kernel(x_ref, su_ref, sg_ref, rg_ref, yf_ref, yb_ref, yp_ref, of_ref, ob_ref, bn_ref,
                gate_ref, sd_ref, wg_ref, bg_ref, gw_ref, gb_ref, e1_ref, e2_ref, wo_ref,
                lg_ref, lb_ref, o_ref, *, alpha):
    y = sd_ref[...] * su_ref[0] + yf_ref[0] + yb_ref[0]
    y = _gelu_tanh(y)
    y = y * jax.nn.sigmoid(_dot(y, wg_ref[...]) + bg_ref[...])
    m1 = y * _silu(sg_ref[0])
    o_sum = of_ref[0] + ob_ref[0]

    def seg_mean(t):
        return _dot_xl(_dot_xl(t, e1_ref[...]), e2_ref[...]) * (1.0 / RWKV_HEAD)

    dev = o_sum - seg_mean(o_sum)
    var = seg_mean(dev * dev)
    on = dev * lax.rsqrt(var + GN_EPS) * gw_ref[...] + gb_ref[...]
    m3 = (on + bn_ref[0]) * _silu(rg_ref[0])
    out = (_dot(m1, wo_ref[0:S5_WIDTH, :]) + _dot(yp_ref[0], wo_ref[S5_WIDTH:S5_WIDTH + POOL_WIDTH, :])
           + _dot(m3, wo_ref[S5_WIDTH + POOL_WIDTH:, :]))
    t = alpha * x_ref[0] + gate_ref[0] * out
    mu = jnp.mean(t, axis=-1, keepdims=True)
    tc = t - mu
    var_t = jnp.mean(tc * tc, axis=-1, keepdims=True)
    o_ref[0] = tc * lax.rsqrt(var_t + LN_EPS) * lg_ref[...] + lb_ref[...]


def _out_proj(x, z, yf, yb, ypool, o_f, o_b, bonus, gate, s5_d, w_glu, b_glu, gn_w, gn_b,
              w_out_bf16, ln_g, ln_b, alpha):
    bsz, n, d = x.shape
    tm = min(256, n)
    e1, e2, _ = _rw_consts(RW_CHUNK)
    tok = lambda b, i: (b, i, 0)
    c2 = lambda b, i: (0, 0)

    def zs(width, blk):
        return pl.BlockSpec((1, tm, width), lambda b, i: (b, i, blk))

    return pl.pallas_call(
        functools.partial(_out_kernel, alpha=alpha),
        grid=(bsz, n // tm),
        in_specs=[pl.BlockSpec((1, tm, d), tok),
                  zs(S5_WIDTH, ZB_S5_U), zs(S5_WIDTH, ZB_S5_G), zs(RWKV_WIDTH, ZB_RWKV_G),
                  pl.BlockSpec((1, tm, S5_WIDTH), tok), pl.BlockSpec((1, tm, S5_WIDTH), tok),
                  pl.BlockSpec((1, tm, POOL_WIDTH), tok),
                  pl.BlockSpec((1, tm, RWKV_WIDTH), tok), pl.BlockSpec((1, tm, RWKV_WIDTH), tok),
                  pl.BlockSpec((1, tm, RWKV_WIDTH), tok),
                  pl.BlockSpec((1, 1, d), lambda b, i: (b, 0, 0)),
                  pl.BlockSpec((1, S5_WIDTH), c2), pl.BlockSpec((S5_WIDTH, S5_WIDTH), c2),
                  pl.BlockSpec((1, S5_WIDTH), c2),
                  pl.BlockSpec((1, RWKV_WIDTH), c2), pl.BlockSpec((1, RWKV_WIDTH), c2),
                  pl.BlockSpec(e1.shape, c2), pl.BlockSpec(e2.shape, c2),
                  pl.BlockSpec((d, d), c2),
                  pl.BlockSpec((1, d), c2), pl.BlockSpec((1, d), c2)],
        out_specs=pl.BlockSpec((1, tm, d), tok),
        out_shape=jax.ShapeDtypeStruct((bsz, n, d), F32),
        compiler_params=_cparams("parallel", "parallel"),
    )(x, z, z, z, yf, yb, ypool, o_f, o_b, bonus, gate,
      s5_d.reshape(1, -1), w_glu, b_glu.reshape(1, -1), gn_w.reshape(1, -1), gn_b.reshape(1, -1),
      e1, e2, w_out_bf16, ln_g.reshape(1, -1), ln_b.reshape(1, -1))


def _permute_w_in(w):
    return jnp.concatenate([w[:, 2048:6144], w[:, 0:2048], w[:, 6144:6400]], axis=1).astype(BF16)


def kernel(x, c, ctx, c_ctx, w_ada, b_ada, w_in, conv_rkv, s5_lam_re, s5_lam_im, s5_log_step,
           s5_b_re, s5_b_im, s5_c_re, s5_c_im, s5_d, w_glu, b_glu, w_pool, pool_scale,
           rwkv_w0, rwkv_w2, rwkv_a0, rwkv_a2, rwkv_k_k, rwkv_k_a, rwkv_r_k, gn_w, gn_b,
           w_out, ln_g, ln_b):
    bsz, n, d = x.shape
    n_ctx = ctx.shape[1]
    depth = w_ada.shape[0]
    alpha = (2 * depth) ** 0.25
    assert bsz + 1 <= SUBLANES and n % 512 == 0 and n_ctx % 256 == 0 and n_ctx <= 512

    cond = jnp.zeros((SUBLANES, d), F32).at[0:bsz].set(c.astype(F32)).at[bsz].set(c_ctx.astype(F32))
    mod = _modulation(cond, w_ada, b_ada)

    s5_zero = jnp.zeros((bsz, 4, 1, 2 * S5_HSTATE), F32)
    rw_zero = jnp.zeros((bsz, 2, RW_PAIRS, LANES, LANES), F32)
    seg_lens = sorted({min(512, n) // SUBLANES, min(512, n_ctx) // SUBLANES})

    xc = ctx
    for l in range(depth):
        ctx_out = l < depth - 1
        shift, scale, gate = (mod[l, 0:bsz, i * d:(i + 1) * d].reshape(bsz, 1, d) for i in range(3))
        shift_c, scale_c, gate_c = (jnp.broadcast_to(mod[l, bsz, i * d:(i + 1) * d], (bsz, 1, d))
                                    for i in range(3))
        w_in_l = _permute_w_in(w_in[l])
        w_out_l = w_out[l].astype(BF16)
        s5_tabs = {m: _s5_tables(s5_lam_re[l], s5_lam_im[l], s5_log_step[l], s5_b_re[l], s5_b_im[l],
                                 s5_c_re[l], s5_c_im[l], m) for m in seg_lens}
        rw_prm = _rw_params(rwkv_w0[l], rwkv_w2[l], rwkv_a0[l], rwkv_a2[l],
                            rwkv_k_k[l], rwkv_k_a[l], rwkv_r_k[l])

        zc = _in_proj(xc, shift_c, scale_c, w_in_l)
        z = _in_proj(x, shift, scale, w_in_l)

        yf_c, yb_c, s5_fin = _s5_scan(zc, s5_tabs, s5_zero)
        yf, yb, _ = _s5_scan(z, s5_tabs, s5_fin)

        of_c, ob_c, bonus_c, rw_fin = _rwkv_scan(zc, conv_rkv[l], rw_prm, rw_zero)
        o_f, o_b, bonus, _ = _rwkv_scan(z, conv_rkv[l], rw_prm, rw_fin)

        ypool = _pool_branch(z, w_pool[l], pool_scale[l], on_grid=True)
        tail = (s5_d[l], w_glu[l], b_glu[l], gn_w[l], gn_b[l], w_out_l, ln_g[l], ln_b[l], alpha)
        x_new = _out_proj(x, z, yf, yb, ypool, o_f, o_b, bonus, gate, *tail)
        if ctx_out:
            ypool_c = _pool_branch(zc, w_pool[l], pool_scale[l], on_grid=False)
            xc = _out_proj(xc, zc, yf_c, yb_c, ypool_c, of_c, ob_c, bonus_c, gate_c, *tail)
        x = x_new
    return x
```

```python
import functools
import math

import numpy as np
import jax
import jax.numpy as jnp
from jax import lax
from jax.experimental import pallas as pl
from jax.experimental.pallas import tpu as pltpu

F32 = jnp.float32
BF16 = jnp.bfloat16

D_MODEL = 2048
GRID_W = 64
S5_WIDTH = 512
S5_GROUP = 16
S5_GROUPS = 32
S5_STATE = 64
POOL_WIDTH = 512
POOL_WINDOWS = (2, 4, 8, 16)
POOL_GROUP = 128
RWKV_WIDTH = 1024
RWKV_HEAD = 64
RWKV_HEADS = 16
RWKV_LORA = 64
D_IN = 6400
RWKV_DECAY_SCALE = 0.606531
S5_MAX_RE = -1e-4
ADALN_EPS = 1e-6
LN_EPS = 1e-5
GN_EPS = 64e-5
L2_EPS = 1e-12

ZB_RKV = 0
ZB_RWKV_G = 3
ZB_S5_U = 8
ZB_S5_G = 9
ZB_POOL_U = 10
ZB_POOL_G = 11
ZB_CODES = 24

LANES = 128
SUBLANES = 8
MXU_COLS = 256
VMEM_LIMIT = 56 * 1024 * 1024

S5_HALF = 256
S5_HSTATE = 1024
RW_CHUNK = 64
RW_PAIRS = RWKV_HEADS // 2
POOL_ROWS = 8


def _cparams(*sem):
    return pltpu.CompilerParams(dimension_semantics=sem, vmem_limit_bytes=VMEM_LIMIT)


def _dot(a, b):
    return jnp.dot(a.astype(BF16), b.astype(BF16), preferred_element_type=F32)


def _dot_nt(a, b):
    return lax.dot_general(a.astype(BF16), b.astype(BF16), (((1,), (1,)), ((), ())),
                           preferred_element_type=F32)


def _dot_tn(a, b):
    return lax.dot_general(a.astype(BF16), b.astype(BF16), (((0,), (0,)), ((), ())),
                           preferred_element_type=F32)


def _split(x, parts):
    out = []
    rem = x
    for _ in range(parts):
        hi = rem.astype(BF16)
        out.append(hi)
        rem = rem - hi.astype(F32)
    return out


def _dot_xl(a, b, parts=2):
    acc = None
    for t in _split(a, parts):
        p = jnp.dot(t, b, preferred_element_type=F32)
        acc = p if acc is None else acc + p
    return acc


def _dot_xr(a, b, parts=2):
    acc = None
    for t in _split(b, parts):
        p = jnp.dot(a, t, preferred_element_type=F32)
        acc = p if acc is None else acc + p
    return acc


def _silu(x):
    return x * jax.nn.sigmoid(x)


def _gelu_tanh(x):
    return 0.5 * x * (1.0 + jnp.tanh(math.sqrt(2.0 / math.pi) * (x + 0.044715 * (x * x * x))))


def _mod_kernel(c_ref, w_ref, b_ref, o_ref):
    s = _silu(c_ref[...])
    o_ref[0] = jnp.dot(s, w_ref[0], preferred_element_type=F32,
                       precision=lax.Precision.HIGHEST) + b_ref[0]


def _modulation(cond, w_ada, b_ada):
    depth, d, d3 = w_ada.shape
    tn = 512
    return pl.pallas_call(
        _mod_kernel,
        grid=(depth, d3 // tn),
        in_specs=[pl.BlockSpec((SUBLANES, d), lambda l, j: (0, 0)),
                  pl.BlockSpec((1, d, tn), lambda l, j: (l, 0, j)),
                  pl.BlockSpec((1, 1, tn), lambda l, j: (l, 0, j))],
        out_specs=pl.BlockSpec((1, SUBLANES, tn), lambda l, j: (l, 0, j)),
        out_shape=jax.ShapeDtypeStruct((depth, SUBLANES, d3), F32),
        compiler_params=_cparams("parallel", "parallel"),
    )(cond, w_ada, b_ada.reshape(depth, 1, d3))


def _in_kernel(x_ref, sh_ref, sc_ref, w_ref, o_ref, h_ref):
    @pl.when(pl.program_id(2) == 0)
    def _():
        x = x_ref[0]
        mu = jnp.mean(x, axis=-1, keepdims=True)
        xc = x - mu
        var = jnp.mean(xc * xc, axis=-1, keepdims=True)
        h = xc * lax.rsqrt(var + ADALN_EPS) * (1.0 + sc_ref[0]) + sh_ref[0]
        h_ref[...] = h.astype(BF16)

    o_ref[0] = jnp.dot(h_ref[...], w_ref[...], preferred_element_type=F32)


def _in_proj(x, shift, scale, w_in_bf16):
    bsz, n, d = x.shape
    tm = min(1024, n)
    tn = 1280
    return pl.pallas_call(
        _in_kernel,
        grid=(bsz, n // tm, D_IN // tn),
        in_specs=[pl.BlockSpec((1, tm, d), lambda b, i, j: (b, i, 0)),
                  pl.BlockSpec((1, 1, d), lambda b, i, j: (b, 0, 0)),
                  pl.BlockSpec((1, 1, d), lambda b, i, j: (b, 0, 0)),
                  pl.BlockSpec((d, tn), lambda b, i, j: (0, j))],
        out_specs=pl.BlockSpec((1, tm, tn), lambda b, i, j: (b, i, j)),
        out_shape=jax.ShapeDtypeStruct((bsz, n, D_IN), F32),
        scratch_shapes=[pltpu.VMEM((tm, d), BF16)],
        compiler_params=_cparams("parallel", "parallel", "arbitrary"),
    )(x, shift, scale, w_in_bf16)


def _s5_tables(lam_re, lam_im, log_step, b_re, b_im, c_re, c_im, seg_lens):
    lr = jnp.minimum(lam_re.astype(F32), S5_MAX_RE)
    li = lam_im.astype(F32)
    step = jnp.exp(log_step.astype(F32))[..., None]
    ar, ai = lr * step, li * step
    mag = jnp.exp(ar)
    abr, abi = mag * jnp.cos(ai), mag * jnp.sin(ai)
    den = lr * lr + li * li
    bsr = ((abr - 1.0) * lr + abi * li) / den
    bsi = (abi * lr - (abr - 1.0) * li) / den
    btr = bsr[..., None] * b_re[None] - bsi[..., None] * b_im[None]
    bti = bsr[..., None] * b_im[None] + bsi[..., None] * b_re[None]
    gh = S5_GROUPS // 2
    eye = jnp.eye(gh, dtype=F32)

    def halves(t):
        return t.reshape((2, 2, gh) + t.shape[2:])

    def b_blocks(t):
        return jnp.einsum('dhgpj,gk->dhgjkp', halves(t), eye).reshape(4, S5_HALF, S5_HSTATE)

    def c_blocks(t):
        return jnp.einsum('dhgip,gk->dhgpki', halves(t), eye).reshape(4, S5_HSTATE, S5_HALF)

    def lanes(re, im):
        return jnp.concatenate([re.reshape(4, 1, S5_HSTATE), im.reshape(4, 1, S5_HSTATE)], axis=2)

    wb = jnp.concatenate([b_blocks(btr), b_blocks(bti)], axis=2).astype(BF16)
    wc = jnp.concatenate([c_blocks(c_re.astype(F32)), -c_blocks(c_im.astype(F32))], axis=1).astype(BF16)
    lam = lanes(abr, abi)
    out = {}
    for m in seg_lens:
        pm = jnp.exp(float(m) * ar)
        out[m] = (wb, wc, lam, lanes(pm * jnp.cos(float(m) * ai), pm * jnp.sin(float(m) * ai)))
    return out


def _s5_perms(tile, seg_len):
    rows = np.arange(tile)
    tok = (rows % SUBLANES) * seg_len + rows // SUBLANES
    pf = np.zeros((tile, tile), np.float32)
    pf[rows, tok] = 1.0
    pb = np.zeros((tile, tile), np.float32)
    pb[rows, tile - 1 - tok] = 1.0
    perm = np.stack([pf, pb])
    return jnp.asarray(perm, BF16), jnp.asarray(perm.transpose(0, 2, 1), BF16)


def _s5_kernel(uf_ref, ub_ref, p_ref, pt_ref, wb_ref, wc_ref, lam_ref, pw_ref, x0_ref,
               yf_ref, yb_ref, xfin_ref, bu_ref, carry_ref, *, seg_len):
    hs = S5_HSTATE
    lane_chunk = 512

    @pl.when(pl.program_id(1) == 0)
    def _():
        carry_ref[...] = x0_ref[0]

    u_refs = (uf_ref, ub_ref)
    y_refs = (yf_ref, yb_ref)
    units = [(d, hf) for d in range(2) for hf in range(2)]
    n_lc = hs // lane_chunk

    for d, hf in units:
        q = d * 2 + hf
        ub = u_refs[d][0, :, hf * S5_HALF:(hf + 1) * S5_HALF].astype(BF16)
        up = jnp.dot(p_ref[d], ub, preferred_element_type=F32).astype(BF16)
        bu_ref[q] = jnp.dot(up, wb_ref[q], preferred_element_type=F32)

    for d, hf in units:
        q = d * 2 + hf
        lam_r = [jnp.broadcast_to(lam_ref[q, :, lc * lane_chunk:(lc + 1) * lane_chunk],
                                  (SUBLANES, lane_chunk)) for lc in range(n_lc)]
        lam_i = [jnp.broadcast_to(lam_ref[q, :, hs + lc * lane_chunk:hs + (lc + 1) * lane_chunk],
                                  (SUBLANES, lane_chunk)) for lc in range(n_lc)]

        def scan(xr, xi, store, q=q, lam_r=lam_r, lam_i=lam_i):
            xr, xi = list(xr), list(xi)
            for i in range(seg_len):
                rows = slice(i * SUBLANES, (i + 1) * SUBLANES)
                for lc in range(n_lc):
                    re_l = slice(lc * lane_chunk, (lc + 1) * lane_chunk)
                    im_l = slice(hs + lc * lane_chunk, hs + (lc + 1) * lane_chunk)
                    nr = lam_r[lc] * xr[lc] - lam_i[lc] * xi[lc] + bu_ref[q, rows, re_l]
                    ni = lam_r[lc] * xi[lc] + lam_i[lc] * xr[lc] + bu_ref[q, rows, im_l]
                    if store:
                        bu_ref[q, rows, re_l] = nr
                        bu_ref[q, rows, im_l] = ni
                    xr[lc], xi[lc] = nr, ni
            return xr, xi

        zeros = [jnp.zeros((SUBLANES, lane_chunk), F32) for _ in range(n_lc)]
        xr, xi = scan(zeros, zeros, store=False)
        end_r = jnp.concatenate(xr, axis=1)
        end_i = jnp.concatenate(xi, axis=1)

        lm_r = pw_ref[q, :, 0:hs]
        lm_i = pw_ref[q, :, hs:2 * hs]
        cr = carry_ref[q, :, 0:hs]
        ci = carry_ref[q, :, hs:2 * hs]
        cin_rows_r, cin_rows_i = [], []
        for s in range(SUBLANES):
            cin_rows_r.append(cr)
            cin_rows_i.append(ci)
            nr = lm_r * cr - lm_i * ci + end_r[s:s + 1]
            ni = lm_r * ci + lm_i * cr + end_i[s:s + 1]
            cr, ci = nr, ni
        carry_ref[q, :, 0:hs] = cr
        carry_ref[q, :, hs:2 * hs] = ci
        cin_r = jnp.concatenate(cin_rows_r, axis=0)
        cin_i = jnp.concatenate(cin_rows_i, axis=0)

        scan([cin_r[:, lc * lane_chunk:(lc + 1) * lane_chunk] for lc in range(n_lc)],
             [cin_i[:, lc * lane_chunk:(lc + 1) * lane_chunk] for lc in range(n_lc)], store=True)

    for d, hf in units:
        q = d * 2 + hf
        yp = jnp.dot(bu_ref[q].astype(BF16), wc_ref[q], preferred_element_type=F32)
        y_refs[d][0, :, hf * S5_HALF:(hf + 1) * S5_HALF] = _dot_xr(pt_ref[d], yp)

    xfin_ref[0] = carry_ref[...]


def _s5_scan(z, tabs, x0):
    bsz, n, _ = z.shape
    tile = min(512, n)
    seg_len = tile // SUBLANES
    nk = n // tile
    wb, wc, lam, pw = tabs[seg_len]
    perm, perm_t = _s5_perms(tile, seg_len)
    const4 = lambda b, k: (0, 0, 0)
    return pl.pallas_call(
        functools.partial(_s5_kernel, seg_len=seg_len),
        grid=(bsz, nk),
        in_specs=[pl.BlockSpec((1, tile, S5_WIDTH), lambda b, k: (b, k, ZB_S5_U)),
                  pl.BlockSpec((1, tile, S5_WIDTH), lambda b, k: (b, nk - 1 - k, ZB_S5_U)),
                  pl.BlockSpec((2, tile, tile), const4),
                  pl.BlockSpec((2, tile, tile), const4),
                  pl.BlockSpec((4, S5_HALF, 2 * S5_HSTATE), const4),
                  pl.BlockSpec((4, 2 * S5_HSTATE, S5_HALF), const4),
                  pl.BlockSpec((4, 1, 2 * S5_HSTATE), const4),
                  pl.BlockSpec((4, 1, 2 * S5_HSTATE), const4),
                  pl.BlockSpec((1, 4, 1, 2 * S5_HSTATE), lambda b, k: (b, 0, 0, 0))],
        out_specs=[pl.BlockSpec((1, tile, S5_WIDTH), lambda b, k: (b, k, 0)),
                   pl.BlockSpec((1, tile, S5_WIDTH), lambda b, k: (b, nk - 1 - k, 0)),
                   pl.BlockSpec((1, 4, 1, 2 * S5_HSTATE), lambda b, k: (b, 0, 0, 0))],
        out_shape=[jax.ShapeDtypeStruct((bsz, n, S5_WIDTH), F32),
                   jax.ShapeDtypeStruct((bsz, n, S5_WIDTH), F32),
                   jax.ShapeDtypeStruct((bsz, 4, 1, 2 * S5_HSTATE), F32)],
        scratch_shapes=[pltpu.VMEM((4, tile, 2 * S5_HSTATE), F32),
                        pltpu.VMEM((4, 1, 2 * S5_HSTATE), F32)],
        compiler_params=_cparams("parallel", "arbitrary"),
    )(z, z, perm, perm_t, wb, wc, lam, pw, x0)


def _pool_matrices(on_grid, tile):
    mats = []
    for w in POOL_WINDOWS:
        lo = w // 2
        if on_grid:
            rows = tile // GRID_W
            ext_rows = 3 * rows
            m = np.zeros((rows, GRID_W, ext_rows, GRID_W), np.float32)
            for r in range(rows):
                for j in range(GRID_W):
                    r_lo = r + rows - lo
                    j_lo, j_hi = max(j - lo, 0), min(j - lo + w - 1, GRID_W - 1)
                    m[r, j, r_lo:r_lo + w, j_lo:j_hi + 1] = 1.0
            mats.append(m.reshape(tile, ext_rows * GRID_W))
        else:
            m = np.zeros((tile, tile), np.float32)
            for j in range(tile):
                m[j, max(j - lo, 0):min(j - lo + w - 1, tile - 1) + 1] = 1.0
            mats.append(m)
    return jnp.asarray(np.stack(mats), BF16)


def _window_count(idx, n, w):
    lo = w // 2
    hi_i = jnp.minimum(idx - lo + w - 1, n - 1)
    lo_i = jnp.maximum(idx - lo, 0)
    return (hi_i - lo_i + 1).astype(F32)


def _pool_kernel(*refs, on_grid, tile, n_rows):
    if on_grid:
        u_ref, p_ref, n_ref, g_ref, m_ref, wp_ref, sc_ref, o_ref = refs
    else:
        u_ref, g_ref, m_ref, wp_ref, sc_ref, o_ref = refs
    k = pl.program_id(1)
    cur = u_ref[0]
    if on_grid:
        prev = jnp.where(k > 0, p_ref[0], 0.0)
        nxt = jnp.where(k < pl.num_programs(1) - 1, n_ref[0], 0.0)
    tok = k * tile + lax.broadcasted_iota(jnp.int32, (tile, POOL_GROUP), 0)
    for g, w in enumerate(POOL_WINDOWS):
        gl = slice(g * POOL_GROUP, (g + 1) * POOL_GROUP)
        if on_grid:
            ext = jnp.concatenate([prev[:, gl], cur[:, gl], nxt[:, gl]], axis=0)
            row = jnp.right_shift(tok, int(math.log2(GRID_W)))
            col = jnp.bitwise_and(tok, GRID_W - 1)
            cnt = _window_count(row, n_rows, w) * _window_count(col, GRID_W, w)
        else:
            ext = cur[:, gl]
            cnt = _window_count(tok, tile, w)
        hi, lo = _split(ext, 2)
        box2 = jnp.dot(m_ref[g], jnp.concatenate([hi, lo], axis=1), preferred_element_type=F32)
        box = box2[:, 0:POOL_GROUP] + box2[:, POOL_GROUP:2 * POOL_GROUP]
        diff = box / cnt - cur[:, gl]
        y = _dot(diff, wp_ref[g]) * sc_ref[:, gl]
        o_ref[0, :, gl] = y * _silu(g_ref[0, :, gl])


def _pool_branch(z, w_pool, pool_scale, on_grid):
    bsz, n, _ = z.shape
    sc = pool_scale.reshape(1, POOL_WIDTH)
    if on_grid:
        tile = POOL_ROWS * GRID_W
        nk = n // tile
        mats = _pool_matrices(True, tile)
        in_specs = [pl.BlockSpec((1, tile, POOL_WIDTH), lambda b, k: (b, k, ZB_POOL_U)),
                    pl.BlockSpec((1, tile, POOL_WIDTH), lambda b, k: (b, jnp.maximum(k - 1, 0), ZB_POOL_U)),
                    pl.BlockSpec((1, tile, POOL_WIDTH), lambda b, k: (b, jnp.minimum(k + 1, nk - 1), ZB_POOL_U)),
                    pl.BlockSpec((1, tile, POOL_WIDTH), lambda b, k: (b, k, ZB_POOL_G))]
        args = (z, z, z, z)
    else:
        tile = n
        nk = 1
        mats = _pool_matrices(False, tile)
        in_specs = [pl.BlockSpec((1, tile, POOL_WIDTH), lambda b, k: (b, k, ZB_POOL_U)),
                    pl.BlockSpec((1, tile, POOL_WIDTH), lambda b, k: (b, k, ZB_POOL_G))]
        args = (z, z)
    in_specs += [pl.BlockSpec(mats.shape, lambda b, k: (0, 0, 0)),
                 pl.BlockSpec(w_pool.shape, lambda b, k: (0, 0, 0)),
                 pl.BlockSpec((1, POOL_WIDTH), lambda b, k: (0, 0))]
    return pl.pallas_call(
        functools.partial(_pool_kernel, on_grid=on_grid, tile=tile, n_rows=n // GRID_W),
        grid=(bsz, nk),
        in_specs=in_specs,
        out_specs=pl.BlockSpec((1, tile, POOL_WIDTH), lambda b, k: (b, k, 0)),
        out_shape=jax.ShapeDtypeStruct((bsz, n, POOL_WIDTH), F32),
        compiler_params=_cparams("parallel", "parallel"),
    )(*args, mats, w_pool, sc)


def _rw_consts(tile):
    ch = np.arange(RWKV_WIDTH)
    e1 = (ch[:, None] // RWKV_HEAD == np.arange(LANES)[None, :]).astype(np.float32)
    t = np.arange(tile)
    same = (t[:, None] // RW_CHUNK) == (t[None, :] // RW_CHUNK)
    tri = np.stack([same & (t[None, :] <= t[:, None]), same & (t[None, :] >= t[:, None])]).astype(np.float32)
    return jnp.asarray(e1, BF16), jnp.asarray(e1.T, BF16), jnp.asarray(tri, BF16)


def _rw_kernel(zf_ref, zfp_ref, zfn_ref, cf_ref, zb_ref, zbp_ref, zbn_ref, cb_ref,
               conv_ref, w0_ref, w2_ref, a0_ref, a2_ref, kk_ref, ka_ref, rk_ref,
               e1_ref, e2_ref, tri_ref, s0_ref,
               of_ref, ob_ref, bonus_ref, sfin_ref,
               s_ref, ar_ref, bk_ref, v_ref, g_ref, *, tile, nk):
    k = pl.program_id(1)
    n_chunks = tile // RW_CHUNK
    w = RWKV_WIDTH
    slot = lax.rem(k, 2)
    cslot = 1 - slot
    k_in = jnp.minimum(k, nk - 1)

    @pl.when(k == 0)
    def _():
        ar_ref[1] = jnp.zeros(ar_ref.shape[1:], BF16)
        bk_ref[1] = jnp.zeros(bk_ref.shape[1:], BF16)
        v_ref[1] = jnp.zeros(v_ref.shape[1:], BF16)
        g_ref[1] = jnp.zeros(g_ref.shape[1:], F32)

    @pl.when(k <= 1)
    def _():
        s_ref[...] = s0_ref[0]

    st_row = lax.broadcasted_iota(jnp.int32, (2 * RW_CHUNK, LANES), 0)
    st_lane = lax.broadcasted_iota(jnp.int32, (2 * RW_CHUNK, LANES), 1)
    chunk_bits = int(math.log2(RW_CHUNK))
    head_mask = (jnp.right_shift(st_row, chunk_bits)
                 == jnp.right_shift(st_lane, chunk_bits)).astype(F32).astype(BF16)
    t_row = lax.broadcasted_iota(jnp.int32, (RW_CHUNK, LANES), 0)
    s_col = jnp.bitwise_and(lax.broadcasted_iota(jnp.int32, (RW_CHUNK, LANES), 1), RW_CHUNK - 1)
    eye = (t_row == s_col).astype(F32)

    def prepare(d):
        z_ref, zp_ref, zn_ref, c_ref = ((zf_ref, zfp_ref, zfn_ref, cf_ref) if d == 0
                                        else (zb_ref, zbp_ref, zbn_ref, cb_ref))
        tidx = k_in if d == 0 else nk - 1 - k_in

        z = z_ref[0]
        halo = jnp.concatenate([jnp.where(tidx > 0, zp_ref[0], 0.0), z,
                                jnp.where(tidx < nk - 1, zn_ref[0], 0.0)], axis=0)
        ext_rows = tile + 2 * SUBLANES
        z_prev = pltpu.roll(halo, 1, axis=0)[SUBLANES:SUBLANES + tile]
        z_next = pltpu.roll(halo, ext_rows - 1, axis=0)[SUBLANES:SUBLANES + tile]
        conv = conv_ref[0:1, :] * z_prev + conv_ref[1:2, :] * z + conv_ref[2:3, :] * z_next
        r = conv[:, 0:w]
        kx = conv[:, w:2 * w]
        v = conv[:, 2 * w:3 * w]

        kk = kx * kk_ref[...]
        ss = _dot(kk * kk, e1_ref[...])
        codes = c_ref[0]
        tw = jnp.tanh(codes[:, 0:LANES])
        ac = codes[:, LANES:2 * LANES]
        w_pre = _dot(tw, w2_ref[d])
        a_pre = _dot(ac, a2_ref[d])
        if d == 0:
            a_other_pre = _dot(ac, a2_ref[1])
        yield

        inv = 1.0 / jnp.maximum(jnp.sqrt(ss), L2_EPS)
        kappa_scale = _dot_xl(inv, e2_ref[...])
        lw = -RWKV_DECAY_SCALE * jax.nn.sigmoid(w0_ref[d] + w_pre)
        cl = _dot_xr(tri_ref[d], lw)
        a = jax.nn.sigmoid(a0_ref[d] + a_pre)
        if d == 0:
            a_other = jax.nn.sigmoid(a0_ref[1] + a_other_pre)
            k_sum = kx * (2.0 + (a + a_other - 2.0) * ka_ref[...])
            bonus_heads = _dot(r * k_sum * rk_ref[...], e1_ref[...])
        yield

        kappa = kk * kappa_scale
        kd = kx * (1.0 + (a - 1.0) * ka_ref[...])
        alpha = kappa * a
        if d == 0:
            bonus_ref[0] = _dot_xl(bonus_heads, e2_ref[...]) * v
        g = jnp.exp(cl)
        gi = jnp.exp(-cl)
        a_t = (-kappa * jnp.exp(cl - lw)).astype(BF16)
        r_t = (r * g).astype(BF16)
        b_t = (alpha * gi).astype(BF16)
        k_t = (kd * gi).astype(BF16)
        v_t = v.astype(BF16)
        yield

        for c in range(n_chunks):
            rs = slice(c * RW_CHUNK, (c + 1) * RW_CHUNK)
            ar_ref[slot, d, c, 0:RW_CHUNK, :] = a_t[rs]
            ar_ref[slot, d, c, RW_CHUNK:2 * RW_CHUNK, :] = r_t[rs]
            bk_ref[slot, d, c, 0:RW_CHUNK, :] = b_t[rs]
            bk_ref[slot, d, c, RW_CHUNK:2 * RW_CHUNK, :] = k_t[rs]
            g0 = c * RW_CHUNK + (RW_CHUNK - SUBLANES if d == 0 else 0)
            g_ref[slot, d, c * SUBLANES:(c + 1) * SUBLANES, :] = g[g0:g0 + SUBLANES]
        v_ref[slot, d] = v_t
        yield

    tri_s =[(s_col < t_row).astype(F32), (s_col > t_row).astype(F32)]
    tri_i = [(s_col <= t_row).astype(F32), (s_col >= t_row).astype(F32)]
    o_refs = (of_ref, ob_ref)
    tc = RW_CHUNK
    chain_groups = [[(d, p) for p in range(RW_PAIRS) for d in range(2)]]

    def block_diag(x):
        xb = x.astype(BF16)
        return jnp.concatenate([xb, xb], axis=0) * head_mask

    def recur(ci, chains):
        cidx = (ci, n_chunks - 1 - ci)

        def lanes(c):
            return slice(c[1] * LANES, (c[1] + 1) * LANES)

        ar = {c: ar_ref[cslot, c[0], cidx[c[0]], :, lanes(c)] for c in chains}
        bk64 = {c: bk_ref[cslot, c[0], cidx[c[0]], :, lanes(c)] for c in chains}
        bk = {c: jnp.concatenate([block_diag(bk64[c][0:tc]), block_diag(bk64[c][tc:2 * tc])], axis=0)
              for c in chains}
        v_st = {c: block_diag(v_ref[cslot, c[0], cidx[c[0]] * tc:(cidx[c[0]] + 1) * tc, lanes(c)])
                for c in chains}
        gram = {c: _dot_nt(ar[c], bk[c]) for c in chains}
        ar_s = {c: _dot_nt(ar[c], s_ref[c[0], c[1]]) for c in chains}
        l_ab = {c: gram[c][0:tc, 0:LANES] * tri_s[c[0]] for c in chains}
        l_ak = {c: (gram[c][0:tc, LANES:2 * LANES] * tri_s[c[0]]).astype(BF16) for c in chains}
        m_rbk = {c: jnp.concatenate([gram[c][tc:2 * tc, 0:LANES] * tri_i[c[0]],
                                     gram[c][tc:2 * tc, LANES:2 * LANES] * tri_i[c[0]]],
                                    axis=1).astype(BF16) for c in chains}
        rhs = {c: ar_s[c][0:tc] + _dot(l_ak[c], v_st[c]) for c in chains}
        inv_m = {c: eye + l_ab[c] for c in chains}
        pw = {c: _dot(l_ab[c], block_diag(l_ab[c])) for c in chains}
        n_sq = int(math.log2(RW_CHUNK))
        for j in range(1, n_sq - 1):
            both = {c: _dot(jnp.concatenate([pw[c], inv_m[c]], axis=0), block_diag(pw[c])) for c in chains}
            pw = {c: both[c][0:tc] for c in chains}
            inv_m = {c: inv_m[c] + both[c][tc:2 * tc] for c in chains}
        inv_m = {c: inv_m[c] + _dot(inv_m[c], block_diag(pw[c])) for c in chains}
        uv ={c: jnp.concatenate([block_diag(_dot(inv_m[c], block_diag(rhs[c]))), v_st[c]], axis=0)
              for c in chains}
        for c in chains:
            d = c[0]
            o_refs[d][0, cidx[d] * tc:(cidx[d] + 1) * tc, lanes(c)] = (
                ar_s[c][tc:2 * tc] + _dot(m_rbk[c], uv[c]))
        for c in chains:
            d, p = c
            g_pick = SUBLANES - 1 if d == 0 else 0
            g_tot = g_ref[cslot, d, cidx[d] * SUBLANES:(cidx[d] + 1) * SUBLANES, lanes(c)]
            ds = _dot_tn(uv[c], bk[c])
            s_ref[d, p] = (s_ref[d, p] + ds) * g_tot[g_pick:g_pick + 1]

    prep = [prepare(0), prepare(1)]
    n_prep_stages = 4
    assert n_chunks >= n_prep_stages - 1
    for ci in range(n_chunks):
        if ci < n_prep_stages - 1:
            for stage in prep:
                next(stage)
        for chains in chain_groups:
            recur(ci, chains)
    for stage in prep:
        next(stage)
        assert next(stage, None) is None

    @pl.when(k == nk)
    def _():
        sfin_ref[0] = s_ref[...]


def _rwkv_scan(z, conv_rkv, prm, s0):
    bsz, n, _ = z.shape
    tile = min(256, n)
    nk = n // tile
    hb = tile // SUBLANES
    nh = n // SUBLANES
    e1, e2, tri = _rw_consts(tile)
    w3 = 3 * RWKV_WIDTH

    def zspecs(tmap):
        return [pl.BlockSpec((1, tile, w3), lambda b, k: (b, tmap(k), ZB_RKV)),
                pl.BlockSpec((1, SUBLANES, w3), lambda b, k: (b, jnp.maximum(tmap(k) * hb - 1, 0), ZB_RKV)),
                pl.BlockSpec((1, SUBLANES, w3), lambda b, k: (b, jnp.minimum((tmap(k) + 1) * hb, nh - 1), ZB_RKV)),
                pl.BlockSpec((1, tile, 2 * LANES), lambda b, k: (b, tmap(k), ZB_CODES))]

    fwd = lambda k: jnp.minimum(k, nk - 1)
    bwd = lambda k: nk - 1 - jnp.minimum(k, nk - 1)
    done = lambda k: jnp.maximum(k - 1, 0)
    c2 = lambda b, k: (0, 0)
    c3 = lambda b, k: (0, 0, 0)
    param_specs = [pl.BlockSpec((3, w3), c2),
                   pl.BlockSpec((2, 1, RWKV_WIDTH), c3), pl.BlockSpec((2, LANES, RWKV_WIDTH), c3),
                   pl.BlockSpec((2, 1, RWKV_WIDTH), c3), pl.BlockSpec((2, LANES, RWKV_WIDTH), c3),
                   pl.BlockSpec((1, RWKV_WIDTH), c2), pl.BlockSpec((1, RWKV_WIDTH), c2),
                   pl.BlockSpec((1, RWKV_WIDTH), c2),
                   pl.BlockSpec(e1.shape, c2), pl.BlockSpec(e2.shape, c2), pl.BlockSpec(tri.shape, c3),
                   pl.BlockSpec((1, 2, RW_PAIRS, LANES, LANES), lambda b, k: (b, 0, 0, 0, 0))]
    tok = jax.ShapeDtypeStruct((bsz, n, RWKV_WIDTH), F32)
    n_chunks = tile // RW_CHUNK
    stacked = pltpu.VMEM((2, 2, n_chunks, 2 * RW_CHUNK, RWKV_WIDTH), BF16)
    return pl.pallas_call(
        functools.partial(_rw_kernel, tile=tile, nk=nk),
        grid=(bsz, nk + 1),
        in_specs=zspecs(fwd) + zspecs(bwd) + param_specs,
        out_specs=[pl.BlockSpec((1, tile, RWKV_WIDTH), lambda b, k: (b, done(k), 0)),
                   pl.BlockSpec((1, tile, RWKV_WIDTH), lambda b, k: (b, nk - 1 - done(k), 0)),
                   pl.BlockSpec((1, tile, RWKV_WIDTH), lambda b, k: (b, fwd(k), 0)),
                   pl.BlockSpec((1, 2, RW_PAIRS, LANES, LANES), lambda b, k: (b, 0, 0, 0, 0))],
        out_shape=[tok, tok, tok, jax.ShapeDtypeStruct((bsz, 2, RW_PAIRS, LANES, LANES), F32)],
        scratch_shapes=[pltpu.VMEM((2, RW_PAIRS, LANES, LANES), F32), stacked, stacked,
                        pltpu.VMEM((2, 2, tile, RWKV_WIDTH), BF16),
                        pltpu.VMEM((2, 2, n_chunks * SUBLANES, RWKV_WIDTH), F32)],
        compiler_params=_cparams("parallel", "arbitrary"),
    )(z, z, z, z, z, z, z, z, conv_rkv, *prm, e1, e2, tri, s0)


def _rw_params(w0, w2, a0, a2, k_k, k_a, r_k):
    zeros = jnp.zeros((RWKV_LORA, RWKV_WIDTH), F32)
    w2p = jnp.stack([jnp.concatenate([w2[0], zeros]), jnp.concatenate([zeros, w2[1]])])
    a2p = jnp.stack([jnp.concatenate([a2[0], zeros]), jnp.concatenate([zeros, a2[1]])])
    return (w0.reshape(2, 1, RWKV_WIDTH), w2p, a0.reshape(2, 1, RWKV_WIDTH), a2p,
            k_k.reshape(1, RWKV_WIDTH), k_a.reshape(1, RWKV_WIDTH), r_k.reshape(1, RWKV_WIDTH))


def _out_kernel(x_ref, su_ref, sg_ref, rg_ref, yf_ref, yb_ref, yp_ref, of_ref, ob_ref, bn_ref,
                gate_ref, sd_ref, wg_ref, bg_ref, gw_ref, gb_ref, e1_ref, e2_ref, wo_ref,
                lg_ref, lb_ref, o_ref, *, alpha):
    y = sd_ref[...] * su_ref[0] + yf_ref[0] + yb_ref[0]
    y = _gelu_tanh(y)
    y = y * jax.nn.sigmoid(_dot(y, wg_ref[...]) + bg_ref[...])
    m1 = y * _silu(sg_ref[0])
    o_sum = of_ref[0] + ob_ref[0]

    inv_n = 1.0 / RWKV_HEAD
    dev = o_sum - _dot_xl(_dot_xl(o_sum, e1_ref[...]) * inv_n, e2_ref[...])
    rstd = lax.rsqrt(_dot(dev * dev, e1_ref[...]) * inv_n + GN_EPS)
    on = dev * _dot_xl(rstd, e2_ref[...]) * gw_ref[...] + gb_ref[...]
    m3 = (on + bn_ref[0]) * _silu(rg_ref[0])
    out = (_dot(m1, wo_ref[0:S5_WIDTH, :]) + _dot(yp_ref[0], wo_ref[S5_WIDTH:S5_WIDTH + POOL_WIDTH, :])
           + _dot(m3, wo_ref[S5_WIDTH + POOL_WIDTH:, :]))
    t = alpha * x_ref[0] + gate_ref[0] * out
    mu = jnp.mean(t, axis=-1, keepdims=True)
    tc = t - mu
    var_t = jnp.mean(tc * tc, axis=-1, keepdims=True)
    o_ref[0] = tc * lax.rsqrt(var_t + LN_EPS) * lg_ref[...] + lb_ref[...]


def _out_proj(x, z, yf, yb, ypool, o_f, o_b, bonus, gate, s5_d, w_glu, b_glu, gn_w, gn_b,
              w_out_bf16, ln_g, ln_b, alpha):
    bsz, n, d = x.shape
    tm = min(256, n)
    e1, e2, _ = _rw_consts(RW_CHUNK)
    tok = lambda b, i: (b, i, 0)
    c2 = lambda b, i: (0, 0)

    def zs(width, blk):
        return pl.BlockSpec((1, tm, width), lambda b, i: (b, i, blk))

    return pl.pallas_call(
        functools.partial(_out_kernel, alpha=alpha),
        grid=(bsz, n // tm),
        in_specs=[pl.BlockSpec((1, tm, d), tok),
                  zs(S5_WIDTH, ZB_S5_U), zs(S5_WIDTH, ZB_S5_G), zs(RWKV_WIDTH, ZB_RWKV_G),
                  pl.BlockSpec((1, tm, S5_WIDTH), tok), pl.BlockSpec((1, tm, S5_WIDTH), tok),
                  pl.BlockSpec((1, tm, POOL_WIDTH), tok),
                  pl.BlockSpec((1, tm, RWKV_WIDTH), tok), pl.BlockSpec((1, tm, RWKV_WIDTH), tok),
                  pl.BlockSpec((1, tm, RWKV_WIDTH), tok),
                  pl.BlockSpec((1, 1, d), lambda b, i: (b, 0, 0)),
                  pl.BlockSpec((1, S5_WIDTH), c2), pl.BlockSpec((S5_WIDTH, S5_WIDTH), c2),
                  pl.BlockSpec((1, S5_WIDTH), c2),
                  pl.BlockSpec((1, RWKV_WIDTH), c2), pl.BlockSpec((1, RWKV_WIDTH), c2),
                  pl.BlockSpec(e1.shape, c2), pl.BlockSpec(e2.shape, c2),
                  pl.BlockSpec((d, d), c2),
                  pl.BlockSpec((1, d), c2), pl.BlockSpec((1, d), c2)],
        out_specs=pl.BlockSpec((1, tm, d), tok),
        out_shape=jax.ShapeDtypeStruct((bsz, n, d), F32),
        compiler_params=_cparams("parallel", "parallel"),
    )(x, z, z, z, yf, yb, ypool, o_f, o_b, bonus, gate,
      s5_d.reshape(1, -1), w_glu, b_glu.reshape(1, -1), gn_w.reshape(1, -1), gn_b.reshape(1, -1),
      e1, e2, w_out_bf16, ln_g.reshape(1, -1), ln_b.reshape(1, -1))


def _permute_w_in(w):
    return jnp.concatenate([w[:, 2048:6144], w[:, 0:2048], w[:, 6144:6400]], axis=1).astype(BF16)


def kernel(x, c, ctx, c_ctx, w_ada, b_ada, w_in, conv_rkv, s5_lam_re, s5_lam_im, s5_log_step,
           s5_b_re, s5_b_im, s5_c_re, s5_c_im, s5_d, w_glu, b_glu, w_pool, pool_scale,
           rwkv_w0, rwkv_w2, rwkv_a0, rwkv_a2, rwkv_k_k, rwkv_k_a, rwkv_r_k, gn_w, gn_b,
           w_out, ln_g, ln_b):
    bsz, n, d = x.shape
    n_ctx = ctx.shape[1]
    depth = w_ada.shape[0]
    alpha = (2 * depth) ** 0.25
    assert bsz + 1 <= SUBLANES and n % 512 == 0 and n_ctx % 256 == 0 and n_ctx <= 512

    cond = jnp.zeros((SUBLANES, d), F32).at[0:bsz].set(c.astype(F32)).at[bsz].set(c_ctx.astype(F32))
    mod = _modulation(cond, w_ada, b_ada)

    s5_zero = jnp.zeros((bsz, 4, 1, 2 * S5_HSTATE), F32)
    rw_zero = jnp.zeros((bsz, 2, RW_PAIRS, LANES, LANES), F32)
    seg_lens = sorted({min(512, n) // SUBLANES, min(512, n_ctx) // SUBLANES})

    xc = ctx
    for l in range(depth):
        ctx_out = l < depth - 1
        shift, scale, gate = (mod[l, 0:bsz, i * d:(i + 1) * d].reshape(bsz, 1, d) for i in range(3))
        shift_c, scale_c, gate_c = (jnp.broadcast_to(mod[l, bsz, i * d:(i + 1) * d], (bsz, 1, d))
                                    for i in range(3))
        w_in_l = _permute_w_in(w_in[l])
        w_out_l = w_out[l].astype(BF16)
        s5_tabs = _s5_tables(s5_lam_re[l], s5_lam_im[l], s5_log_step[l], s5_b_re[l], s5_b_im[l],
                             s5_c_re[l], s5_c_im[l], seg_lens)
        rw_prm = _rw_params(rwkv_w0[l], rwkv_w2[l], rwkv_a0[l], rwkv_a2[l],
                            rwkv_k_k[l], rwkv_k_a[l], rwkv_r_k[l])

        zc = _in_proj(xc, shift_c, scale_c, w_in_l)
        z = _in_proj(x, shift, scale, w_in_l)

        yf_c, yb_c, s5_fin = _s5_scan(zc, s5_tabs, s5_zero)
        yf, yb, _ = _s5_scan(z, s5_tabs, s5_fin)

        of_c, ob_c, bonus_c, rw_fin = _rwkv_scan(zc, conv_rkv[l], rw_prm, rw_zero)
        o_f, o_b, bonus, _ = _rwkv_scan(z, conv_rkv[l], rw_prm, rw_fin)

        ypool = _pool_branch(z, w_pool[l], pool_scale[l], on_grid=True)
        tail = (s5_d[l], w_glu[l], b_glu[l], gn_w[l], gn_b[l], w_out_l, ln_g[l], ln_b[l], alpha)
        x_new = _out_proj(x, z, yf, yb, ypool, o_f, o_b, bonus, gate, *tail)
        if ctx_out:
            ypool_c = _pool_branch(zc, w_pool[l], pool_scale[l], on_grid=False)
            xc = _out_proj(xc, zc, yf_c, yb_c, ypool_c, of_c, ob_c, bonus_c, gate_c, *tail)
        x = x_new
    return x
```

```python
import functools
import math

import numpy as np
import jax
import jax.numpy as jnp
from jax import lax
from jax.experimental import pallas as pl
from jax.experimental.pallas import tpu as pltpu

F32 = jnp.float32
BF16 = jnp.bfloat16

D_MODEL = 2048
GRID_W = 64
S5_WIDTH = 512
S5_GROUP = 16
S5_GROUPS = 32
S5_STATE = 64
POOL_WIDTH = 512
POOL_WINDOWS = (2, 4, 8, 16)
POOL_GROUP = 128
RWKV_WIDTH = 1024
RWKV_HEAD = 64
RWKV_HEADS = 16
RWKV_LORA = 64
D_IN = 6400
RWKV_DECAY_SCALE = 0.606531
S5_MAX_RE = -1e-4
ADALN_EPS = 1e-6
LN_EPS = 1e-5
GN_EPS = 64e-5
L2_EPS = 1e-12

ZB_RKV = 0
ZB_RWKV_G = 3
ZB_S5_U = 8
ZB_S5_G = 9
ZB_POOL_U = 10
ZB_POOL_G = 11
ZB_CODES = 24

LANES = 128
SUBLANES = 8
MXU_COLS = 256
VMEM_LIMIT = 56 * 1024 * 1024

S5_HALF = 256
S5_HSTATE = 1024
RW_CHUNK = 64
RW_PAIRS = RWKV_HEADS // 2
POOL_ROWS = 8


def _cparams(*sem):
    return pltpu.CompilerParams(dimension_semantics=sem, vmem_limit_bytes=VMEM_LIMIT)


def _dot(a, b):
    return jnp.dot(a.astype(BF16), b.astype(BF16), preferred_element_type=F32)


def _dot_nt(a, b):
    return lax.dot_general(a.astype(BF16), b.astype(BF16), (((1,), (1,)), ((), ())),
                           preferred_element_type=F32)


def _dot_tn(a, b):
    return lax.dot_general(a.astype(BF16), b.astype(BF16), (((0,), (0,)), ((), ())),
                           preferred_element_type=F32)


def _split(x, parts):
    out = []
    rem = x
    for _ in range(parts):
        hi = rem.astype(BF16)
        out.append(hi)
        rem = rem - hi.astype(F32)
    return out


def _dot_xl(a, b, parts=2):
    acc = None
    for t in _split(a, parts):
        p = jnp.dot(t, b, preferred_element_type=F32)
        acc = p if acc is None else acc + p
    return acc


def _dot_xr(a, b, parts=2):
    acc = None
    for t in _split(b, parts):
        p = jnp.dot(a, t, preferred_element_type=F32)
        acc = p if acc is None else acc + p
    return acc


def _silu(x):
    return x * jax.nn.sigmoid(x)


def _gelu_tanh(x):
    return 0.5 * x * (1.0 + jnp.tanh(math.sqrt(2.0 / math.pi) * (x + 0.044715 * (x * x * x))))


def _mod_kernel(c_ref, w_ref, b_ref, o_ref):
    s = _silu(c_ref[...])
    o_ref[0] = jnp.dot(s, w_ref[0], preferred_element_type=F32,
                       precision=lax.Precision.HIGHEST) + b_ref[0]


def _modulation(cond, w_ada, b_ada):
    depth, d, d3 = w_ada.shape
    tn = 512
    return pl.pallas_call(
        _mod_kernel,
        grid=(depth, d3 // tn),
        in_specs=[pl.BlockSpec((SUBLANES, d), lambda l, j: (0, 0)),
                  pl.BlockSpec((1, d, tn), lambda l, j: (l, 0, j)),
                  pl.BlockSpec((1, 1, tn), lambda l, j: (l, 0, j))],
        out_specs=pl.BlockSpec((1, SUBLANES, tn), lambda l, j: (l, 0, j)),
        out_shape=jax.ShapeDtypeStruct((depth, SUBLANES, d3), F32),
        compiler_params=_cparams("parallel", "parallel"),
    )(cond, w_ada, b_ada.reshape(depth, 1, d3))


def _in_kernel(x_ref, sh_ref, sc_ref, w_ref, o_ref, h_ref):
    @pl.when(pl.program_id(2) == 0)
    def _():
        x = x_ref[0]
        mu = jnp.mean(x, axis=-1, keepdims=True)
        xc = x - mu
        var = jnp.mean(xc * xc, axis=-1, keepdims=True)
        h = xc * lax.rsqrt(var + ADALN_EPS) * (1.0 + sc_ref[0]) + sh_ref[0]
        h_ref[...] = h.astype(BF16)

    o_ref[0] = jnp.dot(h_ref[...], w_ref[...], preferred_element_type=F32)


def _in_proj(x, shift, scale, w_in_bf16):
    bsz, n, d = x.shape
    tm = min(1024, n)
    tn = 1280
    return pl.pallas_call(
        _in_kernel,
        grid=(bsz, n // tm, D_IN // tn),
        in_specs=[pl.BlockSpec((1, tm, d), lambda b, i, j: (b, i, 0)),
                  pl.BlockSpec((1, 1, d), lambda b, i, j: (b, 0, 0)),
                  pl.BlockSpec((1, 1, d), lambda b, i, j: (b, 0, 0)),
                  pl.BlockSpec((d, tn), lambda b, i, j: (0, j))],
        out_specs=pl.BlockSpec((1, tm, tn), lambda b, i, j: (b, i, j)),
        out_shape=jax.ShapeDtypeStruct((bsz, n, D_IN), F32),
        scratch_shapes=[pltpu.VMEM((tm, d), BF16)],
        compiler_params=_cparams("parallel", "parallel", "arbitrary"),
    )(x, shift, scale, w_in_bf16)


def _s5_tables(lam_re, lam_im, log_step, b_re, b_im, c_re, c_im, seg_lens):
    lr = jnp.minimum(lam_re.astype(F32), S5_MAX_RE)
    li = lam_im.astype(F32)
    step = jnp.exp(log_step.astype(F32))[..., None]
    ar, ai = lr * step, li * step
    mag = jnp.exp(ar)
    abr, abi = mag * jnp.cos(ai), mag * jnp.sin(ai)
    den = lr * lr + li * li
    bsr = ((abr - 1.0) * lr + abi * li) / den
    bsi = (abi * lr - (abr - 1.0) * li) / den
    btr = bsr[..., None] * b_re[None] - bsi[..., None] * b_im[None]
    bti = bsr[..., None] * b_im[None] + bsi[..., None] * b_re[None]
    gh = S5_GROUPS // 2
    eye = jnp.eye(gh, dtype=F32)

    def halves(t):
        return t.reshape((2, 2, gh) + t.shape[2:])

    def b_blocks(t):
        return jnp.einsum('dhgpj,gk->dhgjkp', halves(t), eye).reshape(4, S5_HALF, S5_HSTATE)

    def c_blocks(t):
        return jnp.einsum('dhgip,gk->dhgpki', halves(t), eye).reshape(4, S5_HSTATE, S5_HALF)

    def lanes(re, im):
        return jnp.concatenate([re.reshape(4, 1, S5_HSTATE), im.reshape(4, 1, S5_HSTATE)], axis=2)

    wb = jnp.concatenate([b_blocks(btr), b_blocks(bti)], axis=2).astype(BF16)
    wc = jnp.concatenate([c_blocks(c_re.astype(F32)), -c_blocks(c_im.astype(F32))], axis=1).astype(BF16)
    lam = lanes(abr, abi)
    out = {}
    for m in seg_lens:
        pm = jnp.exp(float(m) * ar)
        out[m] = (wb, wc, lam, lanes(pm * jnp.cos(float(m) * ai), pm * jnp.sin(float(m) * ai)))
    return out


def _s5_perms(tile, seg_len):
    rows = np.arange(tile)
    tok = (rows % SUBLANES) * seg_len + rows // SUBLANES
    pf = np.zeros((tile, tile), np.float32)
    pf[rows, tok] = 1.0
    pb = np.zeros((tile, tile), np.float32)
    pb[rows, tile - 1 - tok] = 1.0
    perm = np.stack([pf, pb])
    return jnp.asarray(perm, BF16), jnp.asarray(perm.transpose(0, 2, 1), BF16)


def _s5_kernel(uf_ref, ub_ref, p_ref, pt_ref, wb_ref, wc_ref, lam_ref, pw_ref, x0_ref,
               yf_ref, yb_ref, xfin_ref, bu_ref, carry_ref, *, seg_len):
    hs = S5_HSTATE
    lane_chunk = 512

    @pl.when(pl.program_id(1) == 0)
    def _():
        carry_ref[...] = x0_ref[0]

    u_refs = (uf_ref, ub_ref)
    y_refs = (yf_ref, yb_ref)
    units = [(d, hf) for d in range(2) for hf in range(2)]
    n_lc = hs // lane_chunk

    def project(d, hf):
        q = d * 2 + hf
        ub = u_refs[d][0, :, hf * S5_HALF:(hf + 1) * S5_HALF].astype(BF16)
        up = jnp.dot(p_ref[d], ub, preferred_element_type=F32).astype(BF16)
        bu_ref[q] = jnp.dot(up, wb_ref[q], preferred_element_type=F32)

    def recur(d, hf):
        q = d * 2 + hf
        lam_r = [jnp.broadcast_to(lam_ref[q, :, lc * lane_chunk:(lc + 1) * lane_chunk],
                                  (SUBLANES, lane_chunk)) for lc in range(n_lc)]
        lam_i = [jnp.broadcast_to(lam_ref[q, :, hs + lc * lane_chunk:hs + (lc + 1) * lane_chunk],
                                  (SUBLANES, lane_chunk)) for lc in range(n_lc)]

        def scan(xr, xi, store, q=q, lam_r=lam_r, lam_i=lam_i):
            xr, xi = list(xr), list(xi)
            for i in range(seg_len):
                rows = slice(i * SUBLANES, (i + 1) * SUBLANES)
                for lc in range(n_lc):
                    re_l = slice(lc * lane_chunk, (lc + 1) * lane_chunk)
                    im_l = slice(hs + lc * lane_chunk, hs + (lc + 1) * lane_chunk)
                    nr = lam_r[lc] * xr[lc] - lam_i[lc] * xi[lc] + bu_ref[q, rows, re_l]
                    ni = lam_r[lc] * xi[lc] + lam_i[lc] * xr[lc] + bu_ref[q, rows, im_l]
                    if store:
                        bu_ref[q, rows, re_l] = nr
                        bu_ref[q, rows, im_l] = ni
                    xr[lc], xi[lc] = nr, ni
            return xr, xi

        zeros = [jnp.zeros((SUBLANES, lane_chunk), F32) for _ in range(n_lc)]
        xr, xi = scan(zeros, zeros, store=False)
        end_r = jnp.concatenate(xr, axis=1)
        end_i = jnp.concatenate(xi, axis=1)

        lm_r = pw_ref[q, :, 0:hs]
        lm_i = pw_ref[q, :, hs:2 * hs]
        cr = carry_ref[q, :, 0:hs]
        ci = carry_ref[q, :, hs:2 * hs]
        cin_rows_r, cin_rows_i = [], []
        for s in range(SUBLANES):
            cin_rows_r.append(cr)
            cin_rows_i.append(ci)
            nr = lm_r * cr - lm_i * ci + end_r[s:s + 1]
            ni = lm_r * ci + lm_i * cr + end_i[s:s + 1]
            cr, ci = nr, ni
        carry_ref[q, :, 0:hs] = cr
        carry_ref[q, :, hs:2 * hs] = ci
        cin_r = jnp.concatenate(cin_rows_r, axis=0)
        cin_i = jnp.concatenate(cin_rows_i, axis=0)

        scan([cin_r[:, lc * lane_chunk:(lc + 1) * lane_chunk] for lc in range(n_lc)],
             [cin_i[:, lc * lane_chunk:(lc + 1) * lane_chunk] for lc in range(n_lc)], store=True)

    def readout(d, hf):
        q = d * 2 + hf
        yp = jnp.dot(bu_ref[q].astype(BF16), wc_ref[q], preferred_element_type=F32)
        y_refs[d][0, :, hf * S5_HALF:(hf + 1) * S5_HALF] = _dot_xr(pt_ref[d], yp)

    n_units = len(units)
    project(*units[0])
    for i in range(n_units):
        if i + 1 < n_units:
            project(*units[i + 1])
        recur(*units[i])
        if i > 0:
            readout(*units[i - 1])
    readout(*units[n_units - 1])

    xfin_ref[0] = carry_ref[...]


def _s5_scan(z, tabs, x0):
    bsz, n, _ = z.shape
    tile = min(512, n)
    seg_len = tile // SUBLANES
    nk = n // tile
    wb, wc, lam, pw = tabs[seg_len]
    perm, perm_t = _s5_perms(tile, seg_len)
    const4 = lambda b, k: (0, 0, 0)
    return pl.pallas_call(
        functools.partial(_s5_kernel, seg_len=seg_len),
        grid=(bsz, nk),
        in_specs=[pl.BlockSpec((1, tile, S5_WIDTH), lambda b, k: (b, k, ZB_S5_U)),
                  pl.BlockSpec((1, tile, S5_WIDTH), lambda b, k: (b, nk - 1 - k, ZB_S5_U)),
                  pl.BlockSpec((2, tile, tile), const4),
                  pl.BlockSpec((2, tile, tile), const4),
                  pl.BlockSpec((4, S5_HALF, 2 * S5_HSTATE), const4),
                  pl.BlockSpec((4, 2 * S5_HSTATE, S5_HALF), const4),
                  pl.BlockSpec((4, 1, 2 * S5_HSTATE), const4),
                  pl.BlockSpec((4, 1, 2 * S5_HSTATE), const4),
                  pl.BlockSpec((1, 4, 1, 2 * S5_HSTATE), lambda b, k: (b, 0, 0, 0))],
        out_specs=[pl.BlockSpec((1, tile, S5_WIDTH), lambda b, k: (b, k, 0)),
                   pl.BlockSpec((1, tile, S5_WIDTH), lambda b, k: (b, nk - 1 - k, 0)),
                   pl.BlockSpec((1, 4, 1, 2 * S5_HSTATE), lambda b, k: (b, 0, 0, 0))],
        out_shape=[jax.ShapeDtypeStruct((bsz, n, S5_WIDTH), F32),
                   jax.ShapeDtypeStruct((bsz, n, S5_WIDTH), F32),
                   jax.ShapeDtypeStruct((bsz, 4, 1, 2 * S5_HSTATE), F32)],
        scratch_shapes=[pltpu.VMEM((4, tile, 2 * S5_HSTATE), F32),
                        pltpu.VMEM((4, 1, 2 * S5_HSTATE), F32)],
        compiler_params=_cparams("parallel", "arbitrary"),
    )(z, z, perm, perm_t, wb, wc, lam, pw, x0)


def _pool_matrices(row_len, tile):
    mats = []
    for w in POOL_WINDOWS:
        lo = w // 2
        m = np.zeros((tile, tile), np.float32)
        for t in range(tile):
            base, j = (t // row_len) * row_len, t % row_len
            m[t, base + max(j - lo, 0):base + min(j - lo + w - 1, row_len - 1) + 1] = 1.0
        mats.append(m)
    return jnp.asarray(np.stack(mats), BF16)


def _window_count(idx, n, w):
    lo = w // 2
    hi_i = jnp.minimum(idx - lo + w - 1, n - 1)
    lo_i = jnp.maximum(idx - lo, 0)
    return (hi_i - lo_i + 1).astype(F32)


def _pool_kernel(*refs, on_grid, tile, n_rows):
    if on_grid:
        u_ref, p_ref, n_ref, g_ref, m_ref, wp_ref, sc_ref, o_ref = refs
    else:
        u_ref, g_ref, m_ref, wp_ref, sc_ref, o_ref = refs
    k = pl.program_id(1)
    cur = u_ref[0]
    if on_grid:
        prev = jnp.where(k > 0, p_ref[0], 0.0)
        nxt = jnp.where(k < pl.num_programs(1) - 1, n_ref[0], 0.0)
    tok = k * tile + lax.broadcasted_iota(jnp.int32, (tile, POOL_GROUP), 0)
    for g, w in enumerate(POOL_WINDOWS):
        gl = slice(g * POOL_GROUP, (g + 1) * POOL_GROUP)
        if on_grid:
            ext = jnp.concatenate([prev[:, gl], cur[:, gl], nxt[:, gl]], axis=0)
            first = tile - (w // 2) * GRID_W
            ext = sum(ext[first + dr * GRID_W:first + dr * GRID_W + tile] for dr in range(w))
            row = jnp.right_shift(tok, int(math.log2(GRID_W)))
            col = jnp.bitwise_and(tok, GRID_W - 1)
            cnt = _window_count(row, n_rows, w) * _window_count(col, GRID_W, w)
        else:
            ext = cur[:, gl]
            cnt = _window_count(tok, tile, w)
        hi, lo = _split(ext, 2)
        box2 = jnp.dot(m_ref[g], jnp.concatenate([hi, lo], axis=1), preferred_element_type=F32)
        box = box2[:, 0:POOL_GROUP] + box2[:, POOL_GROUP:2 * POOL_GROUP]
        diff = box / cnt - cur[:, gl]
        y = _dot(diff, wp_ref[g]) * sc_ref[:, gl]
        o_ref[0, :, gl] = y * _silu(g_ref[0, :, gl])


def _pool_branch(z, w_pool, pool_scale, on_grid):
    bsz, n, _ = z.shape
    sc = pool_scale.reshape(1, POOL_WIDTH)
    if on_grid:
        tile = POOL_ROWS * GRID_W
        nk = n // tile
        mats = _pool_matrices(GRID_W, tile)
        in_specs = [pl.BlockSpec((1, tile, POOL_WIDTH), lambda b, k: (b, k, ZB_POOL_U)),
                    pl.BlockSpec((1, tile, POOL_WIDTH), lambda b, k: (b, jnp.maximum(k - 1, 0), ZB_POOL_U)),
                    pl.BlockSpec((1, tile, POOL_WIDTH), lambda b, k: (b, jnp.minimum(k + 1, nk - 1), ZB_POOL_U)),
                    pl.BlockSpec((1, tile, POOL_WIDTH), lambda b, k: (b, k, ZB_POOL_G))]
        args = (z, z, z, z)
    else:
        tile = n
        nk = 1
        mats = _pool_matrices(tile, tile)
        in_specs = [pl.BlockSpec((1, tile, POOL_WIDTH), lambda b, k: (b, k, ZB_POOL_U)),
                    pl.BlockSpec((1, tile, POOL_WIDTH), lambda b, k: (b, k, ZB_POOL_G))]
        args = (z, z)
    in_specs += [pl.BlockSpec(mats.shape, lambda b, k: (0, 0, 0)),
                 pl.BlockSpec(w_pool.shape, lambda b, k: (0, 0, 0)),
                 pl.BlockSpec((1, POOL_WIDTH), lambda b, k: (0, 0))]
    return pl.pallas_call(
        functools.partial(_pool_kernel, on_grid=on_grid, tile=tile, n_rows=n // GRID_W),
        grid=(bsz, nk),
        in_specs=in_specs,
        out_specs=pl.BlockSpec((1, tile, POOL_WIDTH), lambda b, k: (b, k, 0)),
        out_shape=jax.ShapeDtypeStruct((bsz, n, POOL_WIDTH), F32),
        compiler_params=_cparams("parallel", "parallel"),
    )(*args, mats, w_pool, sc)


def _rw_consts(tile):
    ch = np.arange(RWKV_WIDTH)
    e1 = (ch[:, None] // RWKV_HEAD == np.arange(LANES)[None, :]).astype(np.float32)
    t = np.arange(tile)
    same = (t[:, None] // RW_CHUNK) == (t[None, :] // RW_CHUNK)
    tri = np.stack([same & (t[None, :] <= t[:, None]), same & (t[None, :] >= t[:, None])]).astype(np.float32)
    return jnp.asarray(e1, BF16), jnp.asarray(e1.T, BF16), jnp.asarray(tri, BF16)


def _rw_kernel(zf_ref, zfp_ref, zfn_ref, cf_ref, zb_ref, zbp_ref, zbn_ref, cb_ref,
               conv_ref, w0_ref, w2_ref, a0_ref, a2_ref, kk_ref, ka_ref, rk_ref,
               e1_ref, e2_ref, tri_ref, s0_ref,
               of_ref, ob_ref, bonus_ref, sfin_ref,
               s_ref, ar_ref, bk_ref, v_ref, g_ref, *, tile, nk):
    k = pl.program_id(1)
    n_chunks = tile // RW_CHUNK
    w = RWKV_WIDTH
    slot = lax.rem(k, 2)
    cslot = 1 - slot
    k_in = jnp.minimum(k, nk - 1)

    @pl.when(k == 0)
    def _():
        ar_ref[1] = jnp.zeros(ar_ref.shape[1:], BF16)
        bk_ref[1] = jnp.zeros(bk_ref.shape[1:], BF16)
        v_ref[1] = jnp.zeros(v_ref.shape[1:], BF16)
        g_ref[1] = jnp.zeros(g_ref.shape[1:], F32)

    @pl.when(k <= 1)
    def _():
        s_ref[...] = s0_ref[0]

    st_row = lax.broadcasted_iota(jnp.int32, (2 * RW_CHUNK, LANES), 0)
    st_lane = lax.broadcasted_iota(jnp.int32, (2 * RW_CHUNK, LANES), 1)
    chunk_bits = int(math.log2(RW_CHUNK))
    head_mask = (jnp.right_shift(st_row, chunk_bits)
                 == jnp.right_shift(st_lane, chunk_bits)).astype(F32).astype(BF16)
    t_row = lax.broadcasted_iota(jnp.int32, (RW_CHUNK, LANES), 0)
    s_col = jnp.bitwise_and(lax.broadcasted_iota(jnp.int32, (RW_CHUNK, LANES), 1), RW_CHUNK - 1)
    eye = (t_row == s_col).astype(F32)

    def prepare(d):
        z_ref, zp_ref, zn_ref, c_ref = ((zf_ref, zfp_ref, zfn_ref, cf_ref) if d == 0
                                        else (zb_ref, zbp_ref, zbn_ref, cb_ref))
        tidx = k_in if d == 0 else nk - 1 - k_in

        z = z_ref[0]
        halo = jnp.concatenate([jnp.where(tidx > 0, zp_ref[0], 0.0), z,
                                jnp.where(tidx < nk - 1, zn_ref[0], 0.0)], axis=0)
        ext_rows = tile + 2 * SUBLANES
        z_prev = pltpu.roll(halo, 1, axis=0)[SUBLANES:SUBLANES + tile]
        z_next = pltpu.roll(halo, ext_rows - 1, axis=0)[SUBLANES:SUBLANES + tile]
        yield
        conv = conv_ref[0:1, :] * z_prev + conv_ref[1:2, :] * z + conv_ref[2:3, :] * z_next
        r = conv[:, 0:w]
        kx = conv[:, w:2 * w]
        v = conv[:, 2 * w:3 * w]
        yield
        kk = kx * kk_ref[...]
        kk_sq = (kk * kk).astype(BF16)
        codes = c_ref[0]
        tw = jnp.tanh(codes[:, 0:LANES])
        ac = codes[:, LANES:2 * LANES]
        yield
        ss = _dot(kk_sq, e1_ref[...])
        w_pre = _dot(tw, w2_ref[d])
        yield
        a_pre = _dot(ac, a2_ref[d])
        if d == 0:
            a_other_pre = _dot(ac, a2_ref[1])
        yield
        inv = 1.0 / jnp.maximum(jnp.sqrt(ss), L2_EPS)
        kappa_scale = _dot_xl(inv, e2_ref[...])
        yield
        lw = -RWKV_DECAY_SCALE * jax.nn.sigmoid(w0_ref[d] + w_pre)
        yield
        cl = _dot_xr(tri_ref[d], lw)
        yield
        a = jax.nn.sigmoid(a0_ref[d] + a_pre)
        yield
        if d == 0:
            a_other = jax.nn.sigmoid(a0_ref[1] + a_other_pre)
            k_sum = kx * (2.0 + (a + a_other - 2.0) * ka_ref[...])
            bonus_heads = _dot(r * k_sum * rk_ref[...], e1_ref[...])
        yield
        kappa = kk * kappa_scale
        kd = kx * (1.0 + (a - 1.0) * ka_ref[...])
        alpha = kappa * a
        yield
        if d == 0:
            bonus_ref[0] = _dot_xl(bonus_heads, e2_ref[...]) * v
        yield
        gi = jnp.exp(-cl)
        a_t = (-kappa * jnp.exp(cl - lw)).astype(BF16)
        yield
        r_t = (r * jnp.exp(cl)).astype(BF16)
        b_t = (alpha * gi).astype(BF16)
        yield
        k_t = (kd * gi).astype(BF16)
        v_t = v.astype(BF16)
        yield "stores next"

        for c in range(n_chunks):
            rs = slice(c * RW_CHUNK, (c + 1) * RW_CHUNK)
            ar_ref[slot, d, c, 0:RW_CHUNK, :] = a_t[rs]
            ar_ref[slot, d, c, RW_CHUNK:2 * RW_CHUNK, :] = r_t[rs]
            bk_ref[slot, d, c, 0:RW_CHUNK, :] = b_t[rs]
            bk_ref[slot, d, c, RW_CHUNK:2 * RW_CHUNK, :] = k_t[rs]
            g0 = c * RW_CHUNK + (RW_CHUNK - SUBLANES if d == 0 else 0)
            g_ref[slot, d, c * SUBLANES:(c + 1) * SUBLANES, :] = jnp.exp(cl[g0:g0 + SUBLANES])
        v_ref[slot, d] = v_t
        yield

    tri_s =[(s_col < t_row).astype(F32), (s_col > t_row).astype(F32)]
    tri_i = [(s_col <= t_row).astype(F32), (s_col >= t_row).astype(F32)]
    o_refs = (of_ref, ob_ref)
    tc = RW_CHUNK
    chain_groups = [[(d, p) for p in range(RW_PAIRS) for d in range(2)]]

    def block_diag(x):
        xb = x.astype(BF16)
        return jnp.concatenate([xb, xb], axis=0) * head_mask

    def recur(ci, chains):
        cidx = (ci, n_chunks - 1 - ci)

        def lanes(c):
            return slice(c[1] * LANES, (c[1] + 1) * LANES)

        ar = {c: ar_ref[cslot, c[0], cidx[c[0]], :, lanes(c)] for c in chains}
        bk64 = {c: bk_ref[cslot, c[0], cidx[c[0]], :, lanes(c)] for c in chains}
        bk = {c: jnp.concatenate([block_diag(bk64[c][0:tc]), block_diag(bk64[c][tc:2 * tc])], axis=0)
              for c in chains}
        v_st = {c: block_diag(v_ref[cslot, c[0], cidx[c[0]] * tc:(cidx[c[0]] + 1) * tc, lanes(c)])
                for c in chains}
        yield
        l_ab, l_ak, m_rbk = {}, {}, {}
        for c in chains:
            gram = _dot_nt(ar[c], bk[c])
            l_ab[c] = gram[0:tc, 0:LANES] * tri_s[c[0]]
            l_ak[c] = (gram[0:tc, LANES:2 * LANES] * tri_s[c[0]]).astype(BF16)
            m_rbk[c] = jnp.concatenate([gram[tc:2 * tc, 0:LANES] * tri_i[c[0]],
                                        gram[tc:2 * tc, LANES:2 * LANES] * tri_i[c[0]]],
                                       axis=1).astype(BF16)
        yield
        ar_s = {c: _dot_nt(ar[c], s_ref[c[0], c[1]]) for c in chains}
        yield
        yield
        rhs = {c: ar_s[c][0:tc] + _dot(l_ak[c], v_st[c]) for c in chains}
        yield
        inv_m = {c: eye + l_ab[c] for c in chains}
        pw = {c: _dot(l_ab[c], block_diag(l_ab[c])) for c in chains}
        yield
        n_sq = int(math.log2(RW_CHUNK))
        for j in range(1, n_sq - 1):
            both = {c: _dot(jnp.concatenate([pw[c], inv_m[c]], axis=0), block_diag(pw[c])) for c in chains}
            pw = {c: both[c][0:tc] for c in chains}
            inv_m = {c: inv_m[c] + both[c][tc:2 * tc] for c in chains}
            yield
        inv_m = {c: inv_m[c] + _dot(inv_m[c], block_diag(pw[c])) for c in chains}
        yield
        uv = {c: jnp.concatenate([block_diag(_dot(inv_m[c], block_diag(rhs[c]))), v_st[c]], axis=0)
              for c in chains}
        yield
        for c in chains:
            d = c[0]
            o_refs[d][0, cidx[d] * tc:(cidx[d] + 1) * tc, lanes(c)] = (
                ar_s[c][tc:2 * tc] + _dot(m_rbk[c], uv[c]))
        yield
        for c in chains:
            d, p = c
            g_pick = SUBLANES - 1 if d == 0 else 0
            g_tot = g_ref[cslot, d, cidx[d] * SUBLANES:(cidx[d] + 1) * SUBLANES, lanes(c)]
            ds = _dot_tn(uv[c], bk[c])
            s_ref[d, p] = (s_ref[d, p] + ds) * g_tot[g_pick:g_pick + 1]

    prep_gens = [prepare(0), prepare(1)]

    def prep_pieces():
        active = list(prep_gens)
        while active:
            for gen in list(active):
                if next(gen) == "stores next":
                    active.remove(gen)
                yield

    def recur_pieces():
        for ci in range(n_chunks):
            for chains in chain_groups:
                yield from recur(ci, chains)
                yield

    prep_stream = prep_pieces()
    n_prep = 2 * 15
    prep_span = (n_chunks * 14 * 6) // 7
    issued = 0
    for r, _ in enumerate(recur_pieces()):
        want = min(n_prep, ((r + 1) * n_prep) // prep_span)
        while issued < want:
            next(prep_stream)
            issued += 1
    assert issued == n_prep and next(prep_stream, "end") == "end"
    for gen in prep_gens:
        assert next(gen, "end") is None and next(gen, "end") == "end"

    @pl.when(k == nk)
    def _():
        sfin_ref[0] = s_ref[...]


def _rwkv_scan(z, conv_rkv, prm, s0):
    bsz, n, _ = z.shape
    tile = min(256, n)
    nk = n // tile
    hb = tile // SUBLANES
    nh = n // SUBLANES
    e1, e2, tri = _rw_consts(tile)
    w3 = 3 * RWKV_WIDTH

    def zspecs(tmap):
        return [pl.BlockSpec((1, tile, w3), lambda b, k: (b, tmap(k), ZB_RKV)),
                pl.BlockSpec((1, SUBLANES, w3), lambda b, k: (b, jnp.maximum(tmap(k) * hb - 1, 0), ZB_RKV)),
                pl.BlockSpec((1, SUBLANES, w3), lambda b, k: (b, jnp.minimum((tmap(k) + 1) * hb, nh - 1), ZB_RKV)),
                pl.BlockSpec((1, tile, 2 * LANES), lambda b, k: (b, tmap(k), ZB_CODES))]

    fwd = lambda k: jnp.minimum(k, nk - 1)
    bwd = lambda k: nk - 1 - jnp.minimum(k, nk - 1)
    done = lambda k: jnp.maximum(k - 1, 0)
    c2 = lambda b, k: (0, 0)
    c3 = lambda b, k: (0, 0, 0)
    param_specs = [pl.BlockSpec((3, w3), c2),
                   pl.BlockSpec((2, 1, RWKV_WIDTH), c3), pl.BlockSpec((2, LANES, RWKV_WIDTH), c3),
                   pl.BlockSpec((2, 1, RWKV_WIDTH), c3), pl.BlockSpec((2, LANES, RWKV_WIDTH), c3),
                   pl.BlockSpec((1, RWKV_WIDTH), c2), pl.BlockSpec((1, RWKV_WIDTH), c2),
                   pl.BlockSpec((1, RWKV_WIDTH), c2),
                   pl.BlockSpec(e1.shape, c2), pl.BlockSpec(e2.shape, c2), pl.BlockSpec(tri.shape, c3),
                   pl.BlockSpec((1, 2, RW_PAIRS, LANES, LANES), lambda b, k: (b, 0, 0, 0, 0))]
    tok = jax.ShapeDtypeStruct((bsz, n, RWKV_WIDTH), F32)
    n_chunks = tile // RW_CHUNK
    stacked = pltpu.VMEM((2, 2, n_chunks, 2 * RW_CHUNK, RWKV_WIDTH), BF16)
    return pl.pallas_call(
        functools.partial(_rw_kernel, tile=tile, nk=nk),
        grid=(bsz, nk + 1),
        in_specs=zspecs(fwd) + zspecs(bwd) + param_specs,
        out_specs=[pl.BlockSpec((1, tile, RWKV_WIDTH), lambda b, k: (b, done(k), 0)),
                   pl.BlockSpec((1, tile, RWKV_WIDTH), lambda b, k: (b, nk - 1 - done(k), 0)),
                   pl.BlockSpec((1, tile, RWKV_WIDTH), lambda b, k: (b, fwd(k), 0)),
                   pl.BlockSpec((1, 2, RW_PAIRS, LANES, LANES), lambda b, k: (b, 0, 0, 0, 0))],
        out_shape=[tok, tok, tok, jax.ShapeDtypeStruct((bsz, 2, RW_PAIRS, LANES, LANES), F32)],
        scratch_shapes=[pltpu.VMEM((2, RW_PAIRS, LANES, LANES), F32), stacked, stacked,
                        pltpu.VMEM((2, 2, tile, RWKV_WIDTH), BF16),
                        pltpu.VMEM((2, 2, n_chunks * SUBLANES, RWKV_WIDTH), F32)],
        compiler_params=_cparams("parallel", "arbitrary"),
    )(z, z, z, z, z, z, z, z, conv_rkv, *prm, e1, e2, tri, s0)


def _rw_params(w0, w2, a0, a2, k_k, k_a, r_k):
    zeros = jnp.zeros((RWKV_LORA, RWKV_WIDTH), F32)
    w2p = jnp.stack([jnp.concatenate([w2[0], zeros]), jnp.concatenate([zeros, w2[1]])])
    a2p = jnp.stack([jnp.concatenate([a2[0], zeros]), jnp.concatenate([zeros, a2[1]])])
    return (w0.reshape(2, 1, RWKV_WIDTH), w2p, a0.reshape(2, 1, RWKV_WIDTH), a2p,
            k_k.reshape(1, RWKV_WIDTH), k_a.reshape(1, RWKV_WIDTH), r_k.reshape(1, RWKV_WIDTH))


def _out_kernel(x_ref, su_ref, sg_ref, rg_ref, yf_ref, yb_ref, yp_ref, of_ref, ob_ref, bn_ref,
                gate_ref, sd_ref, wg_ref, bg_ref, gw_ref, gb_ref, e1_ref, e2_ref, wo_ref,
                lg_ref, lb_ref, o_ref, *, alpha):
    y = sd_ref[...] * su_ref[0] + yf_ref[0] + yb_ref[0]
    y = _gelu_tanh(y)
    y = y * jax.nn.sigmoid(_dot(y, wg_ref[...]) + bg_ref[...])
    m1 = y * _silu(sg_ref[0])
    o_sum = of_ref[0] + ob_ref[0]

    inv_n = 1.0 / RWKV_HEAD
    dev = o_sum - _dot_xl(_dot_xl(o_sum, e1_ref[...]) * inv_n, e2_ref[...])
    rstd = lax.rsqrt(_dot(dev * dev, e1_ref[...]) * inv_n + GN_EPS)
    on = dev * _dot_xl(rstd, e2_ref[...]) * gw_ref[...] + gb_ref[...]
    m3 = (on + bn_ref[0]) * _silu(rg_ref[0])
    out = (_dot(m1, wo_ref[0:S5_WIDTH, :]) + _dot(yp_ref[0], wo_ref[S5_WIDTH:S5_WIDTH + POOL_WIDTH, :])
           + _dot(m3, wo_ref[S5_WIDTH + POOL_WIDTH:, :]))
    t = alpha * x_ref[0] + gate_ref[0] * out
    mu = jnp.mean(t, axis=-1, keepdims=True)
    tc = t - mu
    var_t = jnp.mean(tc * tc, axis=-1, keepdims=True)
    o_ref[0] = tc * lax.rsqrt(var_t + LN_EPS) * lg_ref[...] + lb_ref[...]


def _out_proj(x, z, yf, yb, ypool, o_f, o_b, bonus, gate, s5_d, w_glu, b_glu, gn_w, gn_b,
              w_out_bf16, ln_g, ln_b, alpha):
    bsz, n, d = x.shape
    tm = min(256, n)
    e1, e2, _ = _rw_consts(RW_CHUNK)
    tok = lambda b, i: (b, i, 0)
    c2 = lambda b, i: (0, 0)

    def zs(width, blk):
        return pl.BlockSpec((1, tm, width), lambda b, i: (b, i, blk))

    return pl.pallas_call(
        functools.partial(_out_kernel, alpha=alpha),
        grid=(bsz, n // tm),
        in_specs=[pl.BlockSpec((1, tm, d), tok),
                  zs(S5_WIDTH, ZB_S5_U), zs(S5_WIDTH, ZB_S5_G), zs(RWKV_WIDTH, ZB_RWKV_G),
                  pl.BlockSpec((1, tm, S5_WIDTH), tok), pl.BlockSpec((1, tm, S5_WIDTH), tok),
                  pl.BlockSpec((1, tm, POOL_WIDTH), tok),
                  pl.BlockSpec((1, tm, RWKV_WIDTH), tok), pl.BlockSpec((1, tm, RWKV_WIDTH), tok),
                  pl.BlockSpec((1, tm, RWKV_WIDTH), tok),
                  pl.BlockSpec((1, 1, d), lambda b, i: (b, 0, 0)),
                  pl.BlockSpec((1, S5_WIDTH), c2), pl.BlockSpec((S5_WIDTH, S5_WIDTH), c2),
                  pl.BlockSpec((1, S5_WIDTH), c2),
                  pl.BlockSpec((1, RWKV_WIDTH), c2), pl.BlockSpec((1, RWKV_WIDTH), c2),
                  pl.BlockSpec(e1.shape, c2), pl.BlockSpec(e2.shape, c2),
                  pl.BlockSpec((d, d), c2),
                  pl.BlockSpec((1, d), c2), pl.BlockSpec((1, d), c2)],
        out_specs=pl.BlockSpec((1, tm, d), tok),
        out_shape=jax.ShapeDtypeStruct((bsz, n, d), F32),
        compiler_params=_cparams("parallel", "parallel"),
    )(x, z, z, z, yf, yb, ypool, o_f, o_b, bonus, gate,
      s5_d.reshape(1, -1), w_glu, b_glu.reshape(1, -1), gn_w.reshape(1, -1), gn_b.reshape(1, -1),
      e1, e2, w_out_bf16, ln_g.reshape(1, -1), ln_b.reshape(1, -1))


def _permute_w_in(w):
    return jnp.concatenate([w[:, 2048:6144], w[:, 0:2048], w[:, 6144:6400]], axis=1).astype(BF16)


def kernel(x, c, ctx, c_ctx, w_ada, b_ada, w_in, conv_rkv, s5_lam_re, s5_lam_im, s5_log_step,
           s5_b_re, s5_b_im, s5_c_re, s5_c_im, s5_d, w_glu, b_glu, w_pool, pool_scale,
           rwkv_w0, rwkv_w2, rwkv_a0, rwkv_a2, rwkv_k_k, rwkv_k_a, rwkv_r_k, gn_w, gn_b,
           w_out, ln_g, ln_b):
    bsz, n, d = x.shape
    n_ctx = ctx.shape[1]
    depth = w_ada.shape[0]
    alpha = (2 * depth) ** 0.25
    assert bsz + 1 <= SUBLANES and n % 512 == 0 and n_ctx % 256 == 0 and n_ctx <= 512

    cond = jnp.zeros((SUBLANES, d), F32).at[0:bsz].set(c.astype(F32)).at[bsz].set(c_ctx.astype(F32))
    mod = _modulation(cond, w_ada, b_ada)

    s5_zero = jnp.zeros((bsz, 4, 1, 2 * S5_HSTATE), F32)
    rw_zero = jnp.zeros((bsz, 2, RW_PAIRS, LANES, LANES), F32)
    seg_lens = sorted({min(512, n) // SUBLANES, min(512, n_ctx) // SUBLANES})

    xc = ctx
    for l in range(depth):
        ctx_out = l < depth - 1
        shift, scale, gate = (mod[l, 0:bsz, i * d:(i + 1) * d].reshape(bsz, 1, d) for i in range(3))
        shift_c, scale_c, gate_c = (jnp.broadcast_to(mod[l, bsz, i * d:(i + 1) * d], (bsz, 1, d))
                                    for i in range(3))
        w_in_l = _permute_w_in(w_in[l])
        w_out_l = w_out[l].astype(BF16)
        s5_tabs = _s5_tables(s5_lam_re[l], s5_lam_im[l], s5_log_step[l], s5_b_re[l], s5_b_im[l],
                             s5_c_re[l], s5_c_im[l], seg_lens)
        rw_prm = _rw_params(rwkv_w0[l], rwkv_w2[l], rwkv_a0[l], rwkv_a2[l],
                            rwkv_k_k[l], rwkv_k_a[l], rwkv_r_k[l])

        zc = _in_proj(xc, shift_c, scale_c, w_in_l)
        z = _in_proj(x, shift, scale, w_in_l)

        yf_c, yb_c, s5_fin = _s5_scan(zc, s5_tabs, s5_zero)
        yf, yb, _ = _s5_scan(z, s5_tabs, s5_fin)

        of_c, ob_c, bonus_c, rw_fin = _rwkv_scan(zc, conv_rkv[l], rw_prm, rw_zero)
        o_f, o_b, bonus, _ = _rwkv_scan(z, conv_rkv[l], rw_prm, rw_fin)

        ypool = _pool_branch(z, w_pool[l], pool_scale[l], on_grid=True)
        tail = (s5_d[l], w_glu[l], b_glu[l], gn_w[l], gn_b[l], w_out_l, ln_g[l], ln_b[l], alpha)
        x_new = _out_proj(x, z, yf, yb, ypool, o_f, o_b, bonus, gate, *tail)
        if ctx_out:
            ypool_c = _pool_branch(zc, w_pool[l], pool_scale[l], on_grid=False)
            xc = _out_proj(xc, zc, yf_c, yb_c, ypool_c, of_c, ob_c, bonus_c, gate_c, *tail)
        x = x_new
    return x
```

```python
import functools
import math

import numpy as np
import jax
import jax.numpy as jnp
from jax import lax
from jax.experimental import pallas as pl
from jax.experimental.pallas import tpu as pltpu

F32 = jnp.float32
BF16 = jnp.bfloat16

D_MODEL = 2048
GRID_W = 64
S5_WIDTH = 512
S5_GROUP = 16
S5_GROUPS = 32
S5_STATE = 64
POOL_WIDTH = 512
POOL_WINDOWS = (2, 4, 8, 16)
POOL_GROUP = 128
RWKV_WIDTH = 1024
RWKV_HEAD = 64
RWKV_HEADS = 16
RWKV_LORA = 64
D_IN = 6400
RWKV_DECAY_SCALE = 0.606531
S5_MAX_RE = -1e-4
ADALN_EPS = 1e-6
LN_EPS = 1e-5
GN_EPS = 64e-5
L2_EPS = 1e-12

ZB_RKV = 0
ZB_RWKV_G = 3
ZB_S5_U = 8
ZB_S5_G = 9
ZB_POOL_U = 10
ZB_POOL_G = 11
ZB_CODES = 24

LANES = 128
SUBLANES = 8
MXU_COLS = 256
VMEM_LIMIT = 56 * 1024 * 1024

S5_HALF = 256
S5_HSTATE = 1024
RW_CHUNK = 64
RW_PAIRS = RWKV_HEADS // 2
POOL_ROWS = 8


def _cparams(*sem):
    return pltpu.CompilerParams(dimension_semantics=sem, vmem_limit_bytes=VMEM_LIMIT)


def _dot(a, b):
    return jnp.dot(a.astype(BF16), b.astype(BF16), preferred_element_type=F32)


def _dot_nt(a, b):
    return lax.dot_general(a.astype(BF16), b.astype(BF16), (((1,), (1,)), ((), ())),
                           preferred_element_type=F32)


def _dot_tn(a, b):
    return lax.dot_general(a.astype(BF16), b.astype(BF16), (((0,), (0,)), ((), ())),
                           preferred_element_type=F32)


def _split(x, parts):
    out = []
    rem = x
    for _ in range(parts):
        hi = rem.astype(BF16)
        out.append(hi)
        rem = rem - hi.astype(F32)
    return out


def _dot_xl(a, b, parts=2):
    acc = None
    for t in _split(a, parts):
        p = jnp.dot(t, b, preferred_element_type=F32)
        acc = p if acc is None else acc + p
    return acc


def _dot_xr(a, b, parts=2):
    acc = None
    for t in _split(b, parts):
        p = jnp.dot(a, t, preferred_element_type=F32)
        acc = p if acc is None else acc + p
    return acc


def _silu(x):
    return x * jax.nn.sigmoid(x)


def _gelu_tanh(x):
    return 0.5 * x * (1.0 + jnp.tanh(math.sqrt(2.0 / math.pi) * (x + 0.044715 * (x * x * x))))


def _mod_kernel(c_ref, w_ref, b_ref, o_ref):
    s = _silu(c_ref[...])
    o_ref[0] = jnp.dot(s, w_ref[0], preferred_element_type=F32,
                       precision=lax.Precision.HIGHEST) + b_ref[0]


def _modulation(cond, w_ada, b_ada):
    depth, d, d3 = w_ada.shape
    tn = 512
    return pl.pallas_call(
        _mod_kernel,
        grid=(depth, d3 // tn),
        in_specs=[pl.BlockSpec((SUBLANES, d), lambda l, j: (0, 0)),
                  pl.BlockSpec((1, d, tn), lambda l, j: (l, 0, j)),
                  pl.BlockSpec((1, 1, tn), lambda l, j: (l, 0, j))],
        out_specs=pl.BlockSpec((1, SUBLANES, tn), lambda l, j: (l, 0, j)),
        out_shape=jax.ShapeDtypeStruct((depth, SUBLANES, d3), F32),
        compiler_params=_cparams("parallel", "parallel"),
    )(cond, w_ada, b_ada.reshape(depth, 1, d3))


def _in_kernel(x_ref, sh_ref, sc_ref, w_ref, o_ref, h_ref):
    @pl.when(pl.program_id(2) == 0)
    def _():
        x = x_ref[0]
        mu = jnp.mean(x, axis=-1, keepdims=True)
        xc = x - mu
        var = jnp.mean(xc * xc, axis=-1, keepdims=True)
        h = xc * lax.rsqrt(var + ADALN_EPS) * (1.0 + sc_ref[0]) + sh_ref[0]
        h_ref[...] = h.astype(BF16)

    o_ref[0] = jnp.dot(h_ref[...], w_ref[...], preferred_element_type=F32)


def _in_proj(x, shift, scale, w_in_bf16, layer):
    bsz, n, d = x.shape
    tm = min(1024, n)
    tn = 1280
    return pl.pallas_call(
        _in_kernel,
        grid=(bsz, n // tm, D_IN // tn),
        in_specs=[pl.BlockSpec((1, tm, d), lambda b, i, j: (b, i, 0)),
                  pl.BlockSpec((1, 1, d), lambda b, i, j: (b, 0, 0)),
                  pl.BlockSpec((1, 1, d), lambda b, i, j: (b, 0, 0)),
                  pl.BlockSpec((None, d, tn), lambda b, i, j: (layer, 0, j))],
        out_specs=pl.BlockSpec((1, tm, tn), lambda b, i, j: (b, i, j)),
        out_shape=jax.ShapeDtypeStruct((bsz, n, D_IN), F32),
        scratch_shapes=[pltpu.VMEM((tm, d), BF16)],
        compiler_params=_cparams("parallel", "parallel", "arbitrary"),
    )(x, shift, scale, w_in_bf16)


def _s5_tables(lam_re, lam_im, log_step, b_re, b_im, c_re, c_im, seg_lens):
    lr = jnp.minimum(lam_re.astype(F32), S5_MAX_RE)
    li = lam_im.astype(F32)
    step = jnp.exp(log_step.astype(F32))[..., None]
    ar, ai = lr * step, li * step
    mag = jnp.exp(ar)
    abr, abi = mag * jnp.cos(ai), mag * jnp.sin(ai)
    den = lr * lr + li * li
    bsr = ((abr - 1.0) * lr + abi * li) / den
    bsi = (abi * lr - (abr - 1.0) * li) / den
    btr = bsr[..., None] * b_re[None] - bsi[..., None] * b_im[None]
    bti = bsr[..., None] * b_im[None] + bsi[..., None] * b_re[None]
    gh = S5_GROUPS // 2
    state_group = jnp.arange(S5_HSTATE) // S5_STATE

    def halves(t):
        return t.reshape((4, gh) + t.shape[2:])

    def b_blocks(t):
        rows = halves(t).transpose(0, 3, 1, 2).reshape(4, 1, S5_GROUP, S5_HSTATE)
        mask = (jnp.arange(gh)[:, None, None] == state_group[None, None, :]).astype(F32)
        return (rows * mask[None]).reshape(4, S5_HALF, S5_HSTATE)

    def c_blocks(t):
        cols = halves(t).transpose(0, 1, 3, 2).reshape(4, S5_HSTATE, 1, S5_GROUP)
        mask = (state_group[:, None, None] == jnp.arange(gh)[None, :, None]).astype(F32)
        return (cols * mask[None]).reshape(4, S5_HSTATE, S5_HALF)

    def lanes(re, im):
        return jnp.concatenate([re.reshape(4, 1, S5_HSTATE), im.reshape(4, 1, S5_HSTATE)], axis=2)

    wb = jnp.concatenate([b_blocks(btr), b_blocks(bti)], axis=2).astype(BF16)
    wc = jnp.concatenate([c_blocks(c_re.astype(F32)), -c_blocks(c_im.astype(F32))], axis=1).astype(BF16)
    lam = lanes(abr, abi)
    out = {}
    for m in seg_lens:
        pm = jnp.exp(float(m) * ar)
        out[m] = (wb, wc, lam, lanes(pm * jnp.cos(float(m) * ai), pm * jnp.sin(float(m) * ai)))
    return out


def _s5_perms(tile, seg_len):
    rows = np.arange(tile)
    tok = (rows % SUBLANES) * seg_len + rows // SUBLANES
    pf = np.zeros((tile, tile), np.float32)
    pf[rows, tok] = 1.0
    pb = np.zeros((tile, tile), np.float32)
    pb[rows, tile - 1 - tok] = 1.0
    perm = np.stack([pf, pb])
    return jnp.asarray(perm, BF16), jnp.asarray(perm.transpose(0, 2, 1), BF16)


def _s5_kernel(uf_ref, ub_ref, p_ref, pt_ref, wb_ref, wc_ref, lam_ref, pw_ref, x0_ref,
               yf_ref, yb_ref, xfin_ref, bu_ref, carry_ref, *, seg_len):
    hs = S5_HSTATE
    lane_chunk = 512

    @pl.when(pl.program_id(1) == 0)
    def _():
        carry_ref[...] = x0_ref[0]

    u_refs = (uf_ref, ub_ref)
    y_refs = (yf_ref, yb_ref)
    units = [(d, hf) for d in range(2) for hf in range(2)]
    n_lc = hs // lane_chunk

    def project(d, hf):
        q = d * 2 + hf
        ub = u_refs[d][0, :, hf * S5_HALF:(hf + 1) * S5_HALF].astype(BF16)
        up = jnp.dot(p_ref[d], ub, preferred_element_type=F32).astype(BF16)
        bu_ref[q] = jnp.dot(up, wb_ref[q], preferred_element_type=F32)

    def recur(d, hf):
        q = d * 2 + hf
        lam_r = [jnp.broadcast_to(lam_ref[q, :, lc * lane_chunk:(lc + 1) * lane_chunk],
                                  (SUBLANES, lane_chunk)) for lc in range(n_lc)]
        lam_i = [jnp.broadcast_to(lam_ref[q, :, hs + lc * lane_chunk:hs + (lc + 1) * lane_chunk],
                                  (SUBLANES, lane_chunk)) for lc in range(n_lc)]

        def scan(xr, xi, store, q=q, lam_r=lam_r, lam_i=lam_i):
            xr, xi = list(xr), list(xi)
            for i in range(seg_len):
                rows = slice(i * SUBLANES, (i + 1) * SUBLANES)
                for lc in range(n_lc):
                    re_l = slice(lc * lane_chunk, (lc + 1) * lane_chunk)
                    im_l = slice(hs + lc * lane_chunk, hs + (lc + 1) * lane_chunk)
                    nr = lam_r[lc] * xr[lc] - lam_i[lc] * xi[lc] + bu_ref[q, rows, re_l]
                    ni = lam_r[lc] * xi[lc] + lam_i[lc] * xr[lc] + bu_ref[q, rows, im_l]
                    if store:
                        bu_ref[q, rows, re_l] = nr
                        bu_ref[q, rows, im_l] = ni
                    xr[lc], xi[lc] = nr, ni
            return xr, xi

        zeros = [jnp.zeros((SUBLANES, lane_chunk), F32) for _ in range(n_lc)]
        xr, xi = scan(zeros, zeros, store=False)
        end_r = jnp.concatenate(xr, axis=1)
        end_i = jnp.concatenate(xi, axis=1)

        lm_r = pw_ref[q, :, 0:hs]
        lm_i = pw_ref[q, :, hs:2 * hs]
        cr = carry_ref[q, :, 0:hs]
        ci = carry_ref[q, :, hs:2 * hs]
        cin_rows_r, cin_rows_i = [], []
        for s in range(SUBLANES):
            cin_rows_r.append(cr)
            cin_rows_i.append(ci)
            nr = lm_r * cr - lm_i * ci + end_r[s:s + 1]
            ni = lm_r * ci + lm_i * cr + end_i[s:s + 1]
            cr, ci = nr, ni
        carry_ref[q, :, 0:hs] = cr
        carry_ref[q, :, hs:2 * hs] = ci
        cin_r = jnp.concatenate(cin_rows_r, axis=0)
        cin_i = jnp.concatenate(cin_rows_i, axis=0)

        scan([cin_r[:, lc * lane_chunk:(lc + 1) * lane_chunk] for lc in range(n_lc)],
             [cin_i[:, lc * lane_chunk:(lc + 1) * lane_chunk] for lc in range(n_lc)], store=True)

    def readout(d, hf):
        q = d * 2 + hf
        yp = jnp.dot(bu_ref[q].astype(BF16), wc_ref[q], preferred_element_type=F32)
        y_refs[d][0, :, hf * S5_HALF:(hf + 1) * S5_HALF] = _dot_xr(pt_ref[d], yp)

    n_units = len(units)
    project(*units[0])
    for i in range(n_units):
        if i + 1 < n_units:
            project(*units[i + 1])
        recur(*units[i])
        if i > 0:
            readout(*units[i - 1])
    readout(*units[n_units - 1])

    xfin_ref[0] = carry_ref[...]


def _s5_scan(z, tabs, x0):
    bsz, n, _ = z.shape
    tile = min(512, n)
    seg_len = tile // SUBLANES
    nk = n // tile
    wb, wc, lam, pw = tabs[seg_len]
    perm, perm_t = _s5_perms(tile, seg_len)
    const4 = lambda b, k: (0, 0, 0)
    return pl.pallas_call(
        functools.partial(_s5_kernel, seg_len=seg_len),
        grid=(bsz, nk),
        in_specs=[pl.BlockSpec((1, tile, S5_WIDTH), lambda b, k: (b, k, ZB_S5_U)),
                  pl.BlockSpec((1, tile, S5_WIDTH), lambda b, k: (b, nk - 1 - k, ZB_S5_U)),
                  pl.BlockSpec((2, tile, tile), const4),
                  pl.BlockSpec((2, tile, tile), const4),
                  pl.BlockSpec((4, S5_HALF, 2 * S5_HSTATE), const4),
                  pl.BlockSpec((4, 2 * S5_HSTATE, S5_HALF), const4),
                  pl.BlockSpec((4, 1, 2 * S5_HSTATE), const4),
                  pl.BlockSpec((4, 1, 2 * S5_HSTATE), const4),
                  pl.BlockSpec((1, 4, 1, 2 * S5_HSTATE), lambda b, k: (b, 0, 0, 0))],
        out_specs=[pl.BlockSpec((1, tile, S5_WIDTH), lambda b, k: (b, k, 0)),
                   pl.BlockSpec((1, tile, S5_WIDTH), lambda b, k: (b, nk - 1 - k, 0)),
                   pl.BlockSpec((1, 4, 1, 2 * S5_HSTATE), lambda b, k: (b, 0, 0, 0))],
        out_shape=[jax.ShapeDtypeStruct((bsz, n, S5_WIDTH), F32),
                   jax.ShapeDtypeStruct((bsz, n, S5_WIDTH), F32),
                   jax.ShapeDtypeStruct((bsz, 4, 1, 2 * S5_HSTATE), F32)],
        scratch_shapes=[pltpu.VMEM((4, tile, 2 * S5_HSTATE), F32),
                        pltpu.VMEM((4, 1, 2 * S5_HSTATE), F32)],
        compiler_params=_cparams("parallel", "arbitrary"),
    )(z, z, perm, perm_t, wb, wc, lam, pw, x0)


def _pool_matrices(row_len, tile):
    mats = []
    for w in POOL_WINDOWS:
        lo = w // 2
        m = np.zeros((tile, tile), np.float32)
        for t in range(tile):
            base, j = (t // row_len) * row_len, t % row_len
            m[t, base + max(j - lo, 0):base + min(j - lo + w - 1, row_len - 1) + 1] = 1.0
        mats.append(m)
    return jnp.asarray(np.stack(mats), BF16)


def _window_count(idx, n, w):
    lo = w // 2
    hi_i = jnp.minimum(idx - lo + w - 1, n - 1)
    lo_i = jnp.maximum(idx - lo, 0)
    return (hi_i - lo_i + 1).astype(F32)


def _pool_kernel(*refs, on_grid, tile, n_rows):
    if on_grid:
        u_ref, p_ref, n_ref, g_ref, m_ref, wp_ref, sc_ref, o_ref = refs
    else:
        u_ref, g_ref, m_ref, wp_ref, sc_ref, o_ref = refs
    k = pl.program_id(1)
    cur = u_ref[0]
    if on_grid:
        prev = jnp.where(k > 0, p_ref[0], 0.0)
        nxt = jnp.where(k < pl.num_programs(1) - 1, n_ref[0], 0.0)
    tok = k * tile + lax.broadcasted_iota(jnp.int32, (tile, POOL_GROUP), 0)
    for g, w in enumerate(POOL_WINDOWS):
        gl = slice(g * POOL_GROUP, (g + 1) * POOL_GROUP)
        if on_grid:
            ext = jnp.concatenate([prev[:, gl], cur[:, gl], nxt[:, gl]], axis=0)
            first = tile - (w // 2) * GRID_W
            ext = sum(ext[first + dr * GRID_W:first + dr * GRID_W + tile] for dr in range(w))
            row = jnp.right_shift(tok, int(math.log2(GRID_W)))
            col = jnp.bitwise_and(tok, GRID_W - 1)
            cnt = _window_count(row, n_rows, w) * _window_count(col, GRID_W, w)
        else:
            ext = cur[:, gl]
            cnt = _window_count(tok, tile, w)
        hi, lo = _split(ext, 2)
        box2 = jnp.dot(m_ref[g], jnp.concatenate([hi, lo], axis=1), preferred_element_type=F32)
        box = box2[:, 0:POOL_GROUP] + box2[:, POOL_GROUP:2 * POOL_GROUP]
        diff = box / cnt - cur[:, gl]
        y = _dot(diff, wp_ref[g]) * sc_ref[:, gl]
        o_ref[0, :, gl] = y * _silu(g_ref[0, :, gl])


def _pool_branch(z, w_pool, pool_scale, on_grid):
    bsz, n, _ = z.shape
    sc = pool_scale.reshape(1, POOL_WIDTH)
    if on_grid:
        tile = POOL_ROWS * GRID_W
        nk = n // tile
        mats = _pool_matrices(GRID_W, tile)
        in_specs = [pl.BlockSpec((1, tile, POOL_WIDTH), lambda b, k: (b, k, ZB_POOL_U)),
                    pl.BlockSpec((1, tile, POOL_WIDTH), lambda b, k: (b, jnp.maximum(k - 1, 0), ZB_POOL_U)),
                    pl.BlockSpec((1, tile, POOL_WIDTH), lambda b, k: (b, jnp.minimum(k + 1, nk - 1), ZB_POOL_U)),
                    pl.BlockSpec((1, tile, POOL_WIDTH), lambda b, k: (b, k, ZB_POOL_G))]
        args = (z, z, z, z)
    else:
        tile = n
        nk = 1
        mats = _pool_matrices(tile, tile)
        in_specs = [pl.BlockSpec((1, tile, POOL_WIDTH), lambda b, k: (b, k, ZB_POOL_U)),
                    pl.BlockSpec((1, tile, POOL_WIDTH), lambda b, k: (b, k, ZB_POOL_G))]
        args = (z, z)
    in_specs += [pl.BlockSpec(mats.shape, lambda b, k: (0, 0, 0)),
                 pl.BlockSpec(w_pool.shape, lambda b, k: (0, 0, 0)),
                 pl.BlockSpec((1, POOL_WIDTH), lambda b, k: (0, 0))]
    return pl.pallas_call(
        functools.partial(_pool_kernel, on_grid=on_grid, tile=tile, n_rows=n // GRID_W),
        grid=(bsz, nk),
        in_specs=in_specs,
        out_specs=pl.BlockSpec((1, tile, POOL_WIDTH), lambda b, k: (b, k, 0)),
        out_shape=jax.ShapeDtypeStruct((bsz, n, POOL_WIDTH), F32),
        compiler_params=_cparams("parallel", "parallel"),
    )(*args, mats, w_pool, sc)


def _rw_consts(tile):
    ch = np.arange(RWKV_WIDTH)
    e1 = (ch[:, None] // RWKV_HEAD == np.arange(LANES)[None, :]).astype(np.float32)
    t = np.arange(tile)
    same = (t[:, None] // RW_CHUNK) == (t[None, :] // RW_CHUNK)
    tri = np.stack([same & (t[None, :] <= t[:, None]), same & (t[None, :] >= t[:, None])]).astype(np.float32)
    return jnp.asarray(e1, BF16), jnp.asarray(e1.T, BF16), jnp.asarray(tri, BF16)


def _rw_kernel(zf_ref, zfp_ref, zfn_ref, cf_ref, zb_ref, zbp_ref, zbn_ref, cb_ref,
               conv_ref, w0_ref, w2_ref, a0_ref, a2_ref, kk_ref, ka_ref, rk_ref,
               e1_ref, e2_ref, tri_ref, s0_ref,
               of_ref, ob_ref, bonus_ref, sfin_ref,
               s_ref, ar_ref, bk_ref, v_ref, g_ref, *, tile, nk):
    k = pl.program_id(1)
    n_chunks = tile // RW_CHUNK
    w = RWKV_WIDTH
    slot = lax.rem(k, 2)
    cslot = 1 - slot
    k_in = jnp.minimum(k, nk - 1)

    @pl.when(k == 0)
    def _():
        ar_ref[1] = jnp.zeros(ar_ref.shape[1:], BF16)
        bk_ref[1] = jnp.zeros(bk_ref.shape[1:], BF16)
        v_ref[1] = jnp.zeros(v_ref.shape[1:], BF16)
        g_ref[1] = jnp.zeros(g_ref.shape[1:], F32)

    @pl.when(k <= 1)
    def _():
        s_ref[...] = s0_ref[0]

    st_row = lax.broadcasted_iota(jnp.int32, (2 * RW_CHUNK, LANES), 0)
    st_lane = lax.broadcasted_iota(jnp.int32, (2 * RW_CHUNK, LANES), 1)
    chunk_bits = int(math.log2(RW_CHUNK))
    head_mask = (jnp.right_shift(st_row, chunk_bits)
                 == jnp.right_shift(st_lane, chunk_bits)).astype(F32).astype(BF16)
    t_row = lax.broadcasted_iota(jnp.int32, (RW_CHUNK, LANES), 0)
    s_col = jnp.bitwise_and(lax.broadcasted_iota(jnp.int32, (RW_CHUNK, LANES), 1), RW_CHUNK - 1)
    eye = (t_row == s_col).astype(F32)

    def prepare(d):
        z_ref, zp_ref, zn_ref, c_ref = ((zf_ref, zfp_ref, zfn_ref, cf_ref) if d == 0
                                        else (zb_ref, zbp_ref, zbn_ref, cb_ref))
        tidx = k_in if d == 0 else nk - 1 - k_in

        z = z_ref[0]
        halo = jnp.concatenate([jnp.where(tidx > 0, zp_ref[0], 0.0), z,
                                jnp.where(tidx < nk - 1, zn_ref[0], 0.0)], axis=0)
        ext_rows = tile + 2 * SUBLANES
        z_prev = pltpu.roll(halo, 1, axis=0)[SUBLANES:SUBLANES + tile]
        z_next = pltpu.roll(halo, ext_rows - 1, axis=0)[SUBLANES:SUBLANES + tile]
        yield
        conv = conv_ref[0:1, :] * z_prev + conv_ref[1:2, :] * z + conv_ref[2:3, :] * z_next
        r = conv[:, 0:w]
        kx = conv[:, w:2 * w]
        v = conv[:, 2 * w:3 * w]
        yield
        kk = kx * kk_ref[...]
        kk_sq = (kk * kk).astype(BF16)
        codes = c_ref[0]
        tw = jnp.tanh(codes[:, 0:LANES])
        ac = codes[:, LANES:2 * LANES]
        yield
        ss = _dot(kk_sq, e1_ref[...])
        w_pre = _dot(tw, w2_ref[d])
        yield
        a_pre = _dot(ac, a2_ref[d])
        if d == 0:
            a_other_pre = _dot(ac, a2_ref[1])
        yield
        inv = 1.0 / jnp.maximum(jnp.sqrt(ss), L2_EPS)
        kappa_scale = _dot_xl(inv, e2_ref[...])
        yield
        lw = -RWKV_DECAY_SCALE * jax.nn.sigmoid(w0_ref[d] + w_pre)
        yield
        cl = _dot_xr(tri_ref[d], lw)
        yield
        a = jax.nn.sigmoid(a0_ref[d] + a_pre)
        yield
        if d == 0:
            a_other = jax.nn.sigmoid(a0_ref[1] + a_other_pre)
            k_sum = kx * (2.0 + (a + a_other - 2.0) * ka_ref[...])
            bonus_heads = _dot(r * k_sum * rk_ref[...], e1_ref[...])
        yield
        kappa = kk * kappa_scale
        kd = kx * (1.0 + (a - 1.0) * ka_ref[...])
        alpha = kappa * a
        yield
        if d == 0:
            bonus_ref[0] = _dot_xl(bonus_heads, e2_ref[...]) * v
        yield
        gi = jnp.exp(-cl)
        a_t = (-kappa * jnp.exp(cl - lw)).astype(BF16)
        yield
        r_t = (r * jnp.exp(cl)).astype(BF16)
        b_t = (alpha * gi).astype(BF16)
        yield
        k_t = (kd * gi).astype(BF16)
        v_t = v.astype(BF16)
        yield "stores next"

        for c in range(n_chunks):
            rs = slice(c * RW_CHUNK, (c + 1) * RW_CHUNK)
            ar_ref[slot, d, c, 0:RW_CHUNK, :] = a_t[rs]
            ar_ref[slot, d, c, RW_CHUNK:2 * RW_CHUNK, :] = r_t[rs]
            bk_ref[slot, d, c, 0:RW_CHUNK, :] = b_t[rs]
            bk_ref[slot, d, c, RW_CHUNK:2 * RW_CHUNK, :] = k_t[rs]
            g0 = c * RW_CHUNK + (RW_CHUNK - SUBLANES if d == 0 else 0)
            g_ref[slot, d, c * SUBLANES:(c + 1) * SUBLANES, :] = jnp.exp(cl[g0:g0 + SUBLANES])
        v_ref[slot, d] = v_t
        yield

    tri_s =[(s_col < t_row).astype(F32), (s_col > t_row).astype(F32)]
    tri_i = [(s_col <= t_row).astype(F32), (s_col >= t_row).astype(F32)]
    o_refs = (of_ref, ob_ref)
    tc = RW_CHUNK
    chain_groups = [[(d, p) for p in range(RW_PAIRS) for d in range(2)]]

    def block_diag(x):
        xb = x.astype(BF16)
        return jnp.concatenate([xb, xb], axis=0) * head_mask

    def recur(ci, chains):
        cidx = (ci, n_chunks - 1 - ci)

        def lanes(c):
            return slice(c[1] * LANES, (c[1] + 1) * LANES)

        ar = {c: ar_ref[cslot, c[0], cidx[c[0]], :, lanes(c)] for c in chains}
        bk64 = {c: bk_ref[cslot, c[0], cidx[c[0]], :, lanes(c)] for c in chains}
        bk = {c: jnp.concatenate([block_diag(bk64[c][0:tc]), block_diag(bk64[c][tc:2 * tc])], axis=0)
              for c in chains}
        v_st = {c: block_diag(v_ref[cslot, c[0], cidx[c[0]] * tc:(cidx[c[0]] + 1) * tc, lanes(c)])
                for c in chains}
        yield
        l_ab, l_ak, m_rbk = {}, {}, {}
        for c in chains:
            gram = _dot_nt(ar[c], bk[c])
            l_ab[c] = gram[0:tc, 0:LANES] * tri_s[c[0]]
            l_ak[c] = (gram[0:tc, LANES:2 * LANES] * tri_s[c[0]]).astype(BF16)
            m_rbk[c] = jnp.concatenate([gram[tc:2 * tc, 0:LANES] * tri_i[c[0]],
                                        gram[tc:2 * tc, LANES:2 * LANES] * tri_i[c[0]]],
                                       axis=1).astype(BF16)
        yield
        ar_s = {c: _dot_nt(ar[c], s_ref[c[0], c[1]]) for c in chains}
        yield
        yield
        rhs = {c: ar_s[c][0:tc] + _dot(l_ak[c], v_st[c]) for c in chains}
        yield
        inv_m = {c: eye + l_ab[c] for c in chains}
        pw = {c: _dot(l_ab[c], block_diag(l_ab[c])) for c in chains}
        yield
        n_sq = int(math.log2(RW_CHUNK))
        for j in range(1, n_sq - 1):
            both = {c: _dot(jnp.concatenate([pw[c], inv_m[c]], axis=0), block_diag(pw[c])) for c in chains}
            pw = {c: both[c][0:tc] for c in chains}
            inv_m = {c: inv_m[c] + both[c][tc:2 * tc] for c in chains}
            yield
        inv_m = {c: inv_m[c] + _dot(inv_m[c], block_diag(pw[c])) for c in chains}
        yield
        uv = {c: jnp.concatenate([block_diag(_dot(inv_m[c], block_diag(rhs[c]))), v_st[c]], axis=0)
              for c in chains}
        yield
        for c in chains:
            d = c[0]
            o_refs[d][0, cidx[d] * tc:(cidx[d] + 1) * tc, lanes(c)] = (
                ar_s[c][tc:2 * tc] + _dot(m_rbk[c], uv[c]))
        yield
        for c in chains:
            d, p = c
            g_pick = SUBLANES - 1 if d == 0 else 0
            g_tot = g_ref[cslot, d, cidx[d] * SUBLANES:(cidx[d] + 1) * SUBLANES, lanes(c)]
            ds = _dot_tn(uv[c], bk[c])
            s_ref[d, p] = (s_ref[d, p] + ds) * g_tot[g_pick:g_pick + 1]

    prep_gens = [prepare(0), prepare(1)]

    def prep_pieces():
        active = list(prep_gens)
        while active:
            for gen in list(active):
                if next(gen) == "stores next":
                    active.remove(gen)
                yield

    def recur_pieces():
        for ci in range(n_chunks):
            for chains in chain_groups:
                yield from recur(ci, chains)
                yield

    prep_stream = prep_pieces()
    n_prep = 2 * 15
    prep_span = (n_chunks * 14 * 6) // 7
    issued = 0
    for r, _ in enumerate(recur_pieces()):
        want = min(n_prep, ((r + 1) * n_prep) // prep_span)
        while issued < want:
            next(prep_stream)
            issued += 1
    assert issued == n_prep and next(prep_stream, "end") == "end"
    for gen in prep_gens:
        assert next(gen, "end") is None and next(gen, "end") == "end"

    @pl.when(k == nk)
    def _():
        sfin_ref[0] = s_ref[...]


def _rwkv_scan(z, conv_rkv, prm, s0):
    bsz, n, _ = z.shape
    tile = min(256, n)
    nk = n // tile
    hb = tile // SUBLANES
    nh = n // SUBLANES
    e1, e2, tri = _rw_consts(tile)
    w3 = 3 * RWKV_WIDTH

    def zspecs(tmap):
        return [pl.BlockSpec((1, tile, w3), lambda b, k: (b, tmap(k), ZB_RKV)),
                pl.BlockSpec((1, SUBLANES, w3), lambda b, k: (b, jnp.maximum(tmap(k) * hb - 1, 0), ZB_RKV)),
                pl.BlockSpec((1, SUBLANES, w3), lambda b, k: (b, jnp.minimum((tmap(k) + 1) * hb, nh - 1), ZB_RKV)),
                pl.BlockSpec((1, tile, 2 * LANES), lambda b, k: (b, tmap(k), ZB_CODES))]

    fwd = lambda k: jnp.minimum(k, nk - 1)
    bwd = lambda k: nk - 1 - jnp.minimum(k, nk - 1)
    done = lambda k: jnp.maximum(k - 1, 0)
    c2 = lambda b, k: (0, 0)
    c3 = lambda b, k: (0, 0, 0)
    param_specs = [pl.BlockSpec((3, w3), c2),
                   pl.BlockSpec((2, 1, RWKV_WIDTH), c3), pl.BlockSpec((2, LANES, RWKV_WIDTH), c3),
                   pl.BlockSpec((2, 1, RWKV_WIDTH), c3), pl.BlockSpec((2, LANES, RWKV_WIDTH), c3),
                   pl.BlockSpec((1, RWKV_WIDTH), c2), pl.BlockSpec((1, RWKV_WIDTH), c2),
                   pl.BlockSpec((1, RWKV_WIDTH), c2),
                   pl.BlockSpec(e1.shape, c2), pl.BlockSpec(e2.shape, c2), pl.BlockSpec(tri.shape, c3),
                   pl.BlockSpec((1, 2, RW_PAIRS, LANES, LANES), lambda b, k: (b, 0, 0, 0, 0))]
    tok = jax.ShapeDtypeStruct((bsz, n, RWKV_WIDTH), F32)
    n_chunks = tile // RW_CHUNK
    stacked = pltpu.VMEM((2, 2, n_chunks, 2 * RW_CHUNK, RWKV_WIDTH), BF16)
    return pl.pallas_call(
        functools.partial(_rw_kernel, tile=tile, nk=nk),
        grid=(bsz, nk + 1),
        in_specs=zspecs(fwd) + zspecs(bwd) + param_specs,
        out_specs=[pl.BlockSpec((1, tile, RWKV_WIDTH), lambda b, k: (b, done(k), 0)),
                   pl.BlockSpec((1, tile, RWKV_WIDTH), lambda b, k: (b, nk - 1 - done(k), 0)),
                   pl.BlockSpec((1, tile, RWKV_WIDTH), lambda b, k: (b, fwd(k), 0)),
                   pl.BlockSpec((1, 2, RW_PAIRS, LANES, LANES), lambda b, k: (b, 0, 0, 0, 0))],
        out_shape=[tok, tok, tok, jax.ShapeDtypeStruct((bsz, 2, RW_PAIRS, LANES, LANES), F32)],
        scratch_shapes=[pltpu.VMEM((2, RW_PAIRS, LANES, LANES), F32), stacked, stacked,
                        pltpu.VMEM((2, 2, tile, RWKV_WIDTH), BF16),
                        pltpu.VMEM((2, 2, n_chunks * SUBLANES, RWKV_WIDTH), F32)],
        compiler_params=_cparams("parallel", "arbitrary"),
    )(z, z, z, z, z, z, z, z, conv_rkv, *prm, e1, e2, tri, s0)


def _rw_params(w0, w2, a0, a2, k_k, k_a, r_k):
    zeros = jnp.zeros((RWKV_LORA, RWKV_WIDTH), F32)
    w2p = jnp.stack([jnp.concatenate([w2[0], zeros]), jnp.concatenate([zeros, w2[1]])])
    a2p = jnp.stack([jnp.concatenate([a2[0], zeros]), jnp.concatenate([zeros, a2[1]])])
    return (w0.reshape(2, 1, RWKV_WIDTH), w2p, a0.reshape(2, 1, RWKV_WIDTH), a2p,
            k_k.reshape(1, RWKV_WIDTH), k_a.reshape(1, RWKV_WIDTH), r_k.reshape(1, RWKV_WIDTH))


def _out_kernel(x_ref, su_ref, sg_ref, rg_ref, yf_ref, yb_ref, yp_ref, of_ref, ob_ref, bn_ref,
                gate_ref, sd_ref, wg_ref, bg_ref, gw_ref, gb_ref, e1_ref, e2_ref, wo_ref,
                lg_ref, lb_ref, o_ref, *, alpha):
    y = sd_ref[...] * su_ref[0] + yf_ref[0] + yb_ref[0]
    y = _gelu_tanh(y)
    y = y * jax.nn.sigmoid(_dot(y, wg_ref[...]) + bg_ref[...])
    m1 = y * _silu(sg_ref[0])
    o_sum = of_ref[0] + ob_ref[0]

    inv_n = 1.0 / RWKV_HEAD
    dev = o_sum - _dot_xl(_dot_xl(o_sum, e1_ref[...]) * inv_n, e2_ref[...])
    rstd = lax.rsqrt(_dot(dev * dev, e1_ref[...]) * inv_n + GN_EPS)
    on = dev * _dot_xl(rstd, e2_ref[...]) * gw_ref[...] + gb_ref[...]
    m3 = (on + bn_ref[0]) * _silu(rg_ref[0])
    out = (_dot(m1, wo_ref[0:S5_WIDTH, :]) + _dot(yp_ref[0], wo_ref[S5_WIDTH:S5_WIDTH + POOL_WIDTH, :])
           + _dot(m3, wo_ref[S5_WIDTH + POOL_WIDTH:, :]))
    t = alpha * x_ref[0] + gate_ref[0] * out
    mu = jnp.mean(t, axis=-1, keepdims=True)
    tc = t - mu
    var_t = jnp.mean(tc * tc, axis=-1, keepdims=True)
    o_ref[0] = tc * lax.rsqrt(var_t + LN_EPS) * lg_ref[...] + lb_ref[...]


def _out_proj(x, z, yf, yb, ypool, o_f, o_b, bonus, gate, s5_d, w_glu, b_glu, gn_w, gn_b,
              w_out_bf16, layer, ln_g, ln_b, alpha):
    bsz, n, d = x.shape
    tm = min(256, n)
    e1, e2, _ = _rw_consts(RW_CHUNK)
    tok = lambda b, i: (b, i, 0)
    c2 = lambda b, i: (0, 0)

    def zs(width, blk):
        return pl.BlockSpec((1, tm, width), lambda b, i: (b, i, blk))

    return pl.pallas_call(
        functools.partial(_out_kernel, alpha=alpha),
        grid=(bsz, n // tm),
        in_specs=[pl.BlockSpec((1, tm, d), tok),
                  zs(S5_WIDTH, ZB_S5_U), zs(S5_WIDTH, ZB_S5_G), zs(RWKV_WIDTH, ZB_RWKV_G),
                  pl.BlockSpec((1, tm, S5_WIDTH), tok), pl.BlockSpec((1, tm, S5_WIDTH), tok),
                  pl.BlockSpec((1, tm, POOL_WIDTH), tok),
                  pl.BlockSpec((1, tm, RWKV_WIDTH), tok), pl.BlockSpec((1, tm, RWKV_WIDTH), tok),
                  pl.BlockSpec((1, tm, RWKV_WIDTH), tok),
                  pl.BlockSpec((1, 1, d), lambda b, i: (b, 0, 0)),
                  pl.BlockSpec((1, S5_WIDTH), c2), pl.BlockSpec((S5_WIDTH, S5_WIDTH), c2),
                  pl.BlockSpec((1, S5_WIDTH), c2),
                  pl.BlockSpec((1, RWKV_WIDTH), c2), pl.BlockSpec((1, RWKV_WIDTH), c2),
                  pl.BlockSpec(e1.shape, c2), pl.BlockSpec(e2.shape, c2),
                  pl.BlockSpec((None, d, d), lambda b, i: (layer, 0, 0)),
                  pl.BlockSpec((1, d), c2), pl.BlockSpec((1, d), c2)],
        out_specs=pl.BlockSpec((1, tm, d), tok),
        out_shape=jax.ShapeDtypeStruct((bsz, n, d), F32),
        compiler_params=_cparams("parallel", "parallel"),
    )(x, z, z, z, yf, yb, ypool, o_f, o_b, bonus, gate,
      s5_d.reshape(1, -1), w_glu, b_glu.reshape(1, -1), gn_w.reshape(1, -1), gn_b.reshape(1, -1),
      e1, e2, w_out_bf16, ln_g.reshape(1, -1), ln_b.reshape(1, -1))


def _permute_w_in(w):
    return jnp.concatenate([w[..., 2048:6144], w[..., 0:2048], w[..., 6144:6400]], axis=-1).astype(BF16)


def kernel(x, c, ctx, c_ctx, w_ada, b_ada, w_in, conv_rkv, s5_lam_re, s5_lam_im, s5_log_step,
           s5_b_re, s5_b_im, s5_c_re, s5_c_im, s5_d, w_glu, b_glu, w_pool, pool_scale,
           rwkv_w0, rwkv_w2, rwkv_a0, rwkv_a2, rwkv_k_k, rwkv_k_a, rwkv_r_k, gn_w, gn_b,
           w_out, ln_g, ln_b):
    bsz, n, d = x.shape
    n_ctx = ctx.shape[1]
    depth = w_ada.shape[0]
    alpha = (2 * depth) ** 0.25
    assert bsz + 1 <= SUBLANES and n % 512 == 0 and n_ctx % 256 == 0 and n_ctx <= 512

    cond = jnp.zeros((SUBLANES, d), F32).at[0:bsz].set(c.astype(F32)).at[bsz].set(c_ctx.astype(F32))
    mod = _modulation(cond, w_ada, b_ada)

    s5_zero = jnp.zeros((bsz, 4, 1, 2 * S5_HSTATE), F32)
    rw_zero = jnp.zeros((bsz, 2, RW_PAIRS, LANES, LANES), F32)
    seg_lens = sorted({min(512, n) // SUBLANES, min(512, n_ctx) // SUBLANES})
    w_in_bf16 = _permute_w_in(w_in)
    w_out_bf16 = w_out.astype(BF16)

    xc = ctx
    for l in range(depth):
        ctx_out = l < depth - 1
        shift, scale, gate = (mod[l, 0:bsz, i * d:(i + 1) * d].reshape(bsz, 1, d) for i in range(3))
        shift_c, scale_c, gate_c = (jnp.broadcast_to(mod[l, bsz, i * d:(i + 1) * d], (bsz, 1, d))
                                    for i in range(3))
        s5_tabs = _s5_tables(s5_lam_re[l], s5_lam_im[l], s5_log_step[l], s5_b_re[l], s5_b_im[l],
                             s5_c_re[l], s5_c_im[l], seg_lens)
        rw_prm = _rw_params(rwkv_w0[l], rwkv_w2[l], rwkv_a0[l], rwkv_a2[l],
                            rwkv_k_k[l], rwkv_k_a[l], rwkv_r_k[l])

        zc = _in_proj(xc, shift_c, scale_c, w_in_bf16, l)
        z = _in_proj(x, shift, scale, w_in_bf16, l)

        yf_c, yb_c, s5_fin = _s5_scan(zc, s5_tabs, s5_zero)
        yf, yb, _ = _s5_scan(z, s5_tabs, s5_fin)

        of_c, ob_c, bonus_c, rw_fin = _rwkv_scan(zc, conv_rkv[l], rw_prm, rw_zero)
        o_f, o_b, bonus, _ = _rwkv_scan(z, conv_rkv[l], rw_prm, rw_fin)

        ypool = _pool_branch(z, w_pool[l], pool_scale[l], on_grid=True)
        tail = (s5_d[l], w_glu[l], b_glu[l], gn_w[l], gn_b[l], w_out_bf16, l, ln_g[l], ln_b[l], alpha)
        x_new = _out_proj(x, z, yf, yb, ypool, o_f, o_b, bonus, gate, *tail)
        if ctx_out:
            ypool_c = _pool_branch(zc, w_pool[l], pool_scale[l], on_grid=False)
            xc = _out_proj(xc, zc, yf_c, yb_c, ypool_c, of_c, ob_c, bonus_c, gate_c, *tail)
        x = x_new
    return x
```

```python
import functools
import math

import numpy as np
import jax
import jax.numpy as jnp
from jax import lax
from jax.experimental import pallas as pl
from jax.experimental.pallas import tpu as pltpu

F32 = jnp.float32
BF16 = jnp.bfloat16

D_MODEL = 2048
GRID_W = 64
S5_WIDTH = 512
S5_GROUP = 16
S5_GROUPS = 32
S5_STATE = 64
POOL_WIDTH = 512
POOL_WINDOWS = (2, 4, 8, 16)
POOL_GROUP = 128
RWKV_WIDTH = 1024
RWKV_HEAD = 64
RWKV_HEADS = 16
RWKV_LORA = 64
D_IN = 6400
RWKV_DECAY_SCALE = 0.606531
S5_MAX_RE = -1e-4
ADALN_EPS = 1e-6
LN_EPS = 1e-5
GN_EPS = 64e-5
L2_EPS = 1e-12

ZB_RKV = 0
ZB_RWKV_G = 3
ZB_S5_U = 8
ZB_S5_G = 9
ZB_POOL_U = 10
ZB_POOL_G = 11
ZB_CODES = 24

LANES = 128
SUBLANES = 8
MXU_COLS = 256
VMEM_LIMIT = 56 * 1024 * 1024

S5_HALF = MXU_COLS
S5_HSTATE = (S5_HALF // S5_GROUP) * S5_STATE
RW_CHUNK = 64
RW_PAIRS = RWKV_HEADS // 2
POOL_ROWS = 8

MOD_TN = 512
IN_TM = 1024
IN_TN = 5 * MXU_COLS
S5_TILE = 512
RW_TILE = 4 * RW_CHUNK
OUT_TM = 256


def _cparams(*sem):
    return pltpu.CompilerParams(dimension_semantics=sem, vmem_limit_bytes=VMEM_LIMIT)


def _dot(a, b):
    return jnp.dot(a.astype(BF16), b.astype(BF16), preferred_element_type=F32)


def _dot_nt(a, b):
    return lax.dot_general(a.astype(BF16), b.astype(BF16), (((1,), (1,)), ((), ())),
                           preferred_element_type=F32)


def _dot_tn(a, b):
    return lax.dot_general(a.astype(BF16), b.astype(BF16), (((0,), (0,)), ((), ())),
                           preferred_element_type=F32)


def _split(x, parts):
    out = []
    rem = x
    for _ in range(parts):
        hi = rem.astype(BF16)
        out.append(hi)
        rem = rem - hi.astype(F32)
    return out


def _dot_xl(a, b, parts=2):
    acc = None
    for t in _split(a, parts):
        p = jnp.dot(t, b, preferred_element_type=F32)
        acc = p if acc is None else acc + p
    return acc


def _dot_xr(a, b, parts=2):
    acc = None
    for t in _split(b, parts):
        p = jnp.dot(a, t, preferred_element_type=F32)
        acc = p if acc is None else acc + p
    return acc


def _silu(x):
    return x * jax.nn.sigmoid(x)


def _gelu_tanh(x):
    return 0.5 * x * (1.0 + jnp.tanh(math.sqrt(2.0 / math.pi) * (x + 0.044715 * (x * x * x))))


def _mod_kernel(c_ref, w_ref, b_ref, o_ref):
    s = _silu(c_ref[...])
    o_ref[0] = jnp.dot(s, w_ref[0], preferred_element_type=F32,
                       precision=lax.Precision.HIGHEST) + b_ref[0]


def _modulation(cond, w_ada, b_ada):
    depth, d, d3 = w_ada.shape
    tn = MOD_TN
    return pl.pallas_call(
        _mod_kernel,
        grid=(depth, d3 // tn),
        in_specs=[pl.BlockSpec((SUBLANES, d), lambda l, j: (0, 0)),
                  pl.BlockSpec((1, d, tn), lambda l, j: (l, 0, j)),
                  pl.BlockSpec((1, 1, tn), lambda l, j: (l, 0, j))],
        out_specs=pl.BlockSpec((1, SUBLANES, tn), lambda l, j: (l, 0, j)),
        out_shape=jax.ShapeDtypeStruct((depth, SUBLANES, d3), F32),
        compiler_params=_cparams("parallel", "parallel"),
    )(cond, w_ada, b_ada.reshape(depth, 1, d3))


def _in_kernel(x_ref, sh_ref, sc_ref, w_ref, o_ref, h_ref):
    @pl.when(pl.program_id(2) == 0)
    def _():
        x = x_ref[0]
        mu = jnp.mean(x, axis=-1, keepdims=True)
        xc = x - mu
        var = jnp.mean(xc * xc, axis=-1, keepdims=True)
        h = xc * lax.rsqrt(var + ADALN_EPS) * (1.0 + sc_ref[0]) + sh_ref[0]
        h_ref[...] = h.astype(BF16)

    o_ref[0] = jnp.dot(h_ref[...], w_ref[...], preferred_element_type=F32)


def _in_proj(x, shift, scale, w_in_bf16, layer):
    bsz, n, d = x.shape
    tm = min(IN_TM, n)
    tn = IN_TN
    return pl.pallas_call(
        _in_kernel,
        grid=(bsz, n // tm, D_IN // tn),
        in_specs=[pl.BlockSpec((1, tm, d), lambda b, i, j: (b, i, 0)),
                  pl.BlockSpec((1, 1, d), lambda b, i, j: (b, 0, 0)),
                  pl.BlockSpec((1, 1, d), lambda b, i, j: (b, 0, 0)),
                  pl.BlockSpec((None, d, tn), lambda b, i, j: (layer, 0, j))],
        out_specs=pl.BlockSpec((1, tm, tn), lambda b, i, j: (b, i, j)),
        out_shape=jax.ShapeDtypeStruct((bsz, n, D_IN), F32),
        scratch_shapes=[pltpu.VMEM((tm, d), BF16)],
        compiler_params=_cparams("parallel", "parallel", "arbitrary"),
    )(x, shift, scale, w_in_bf16)


def _s5_tables(lam_re, lam_im, log_step, b_re, b_im, c_re, c_im, seg_lens):
    lr = jnp.minimum(lam_re.astype(F32), S5_MAX_RE)
    li = lam_im.astype(F32)
    step = jnp.exp(log_step.astype(F32))[..., None]
    ar, ai = lr * step, li * step
    mag = jnp.exp(ar)
    abr, abi = mag * jnp.cos(ai), mag * jnp.sin(ai)
    den = lr * lr + li * li
    bsr = ((abr - 1.0) * lr + abi * li) / den
    bsi = (abi * lr - (abr - 1.0) * li) / den
    btr = bsr[..., None] * b_re[None] - bsi[..., None] * b_im[None]
    bti = bsr[..., None] * b_im[None] + bsi[..., None] * b_re[None]
    gh = S5_GROUPS // 2
    state_group = jnp.arange(S5_HSTATE) // S5_STATE

    def halves(t):
        return t.reshape((4, gh) + t.shape[2:])

    def b_blocks(t):
        rows = halves(t).transpose(0, 3, 1, 2).reshape(4, 1, S5_GROUP, S5_HSTATE)
        mask = (jnp.arange(gh)[:, None, None] == state_group[None, None, :]).astype(F32)
        return (rows * mask[None]).reshape(4, S5_HALF, S5_HSTATE)

    def c_blocks(t):
        cols = halves(t).transpose(0, 1, 3, 2).reshape(4, S5_HSTATE, 1, S5_GROUP)
        mask = (state_group[:, None, None] == jnp.arange(gh)[None, :, None]).astype(F32)
        return (cols * mask[None]).reshape(4, S5_HSTATE, S5_HALF)

    def lanes(re, im):
        return jnp.concatenate([re.reshape(4, 1, S5_HSTATE), im.reshape(4, 1, S5_HSTATE)], axis=2)

    wb = jnp.concatenate([b_blocks(btr), b_blocks(bti)], axis=2).astype(BF16)
    wc = jnp.concatenate([c_blocks(c_re.astype(F32)), -c_blocks(c_im.astype(F32))], axis=1).astype(BF16)
    lam = lanes(abr, abi)
    out = {}
    for m in seg_lens:
        pm = jnp.exp(float(m) * ar)
        out[m] = (wb, wc, lam, lanes(pm * jnp.cos(float(m) * ai), pm * jnp.sin(float(m) * ai)))
    return out


def _s5_perms(tile, seg_len):
    rows = np.arange(tile)
    tok = (rows % SUBLANES) * seg_len + rows // SUBLANES
    pf = np.zeros((tile, tile), np.float32)
    pf[rows, tok] = 1.0
    pb = np.zeros((tile, tile), np.float32)
    pb[rows, tile - 1 - tok] = 1.0
    perm = np.stack([pf, pb])
    return jnp.asarray(perm, BF16), jnp.asarray(perm.transpose(0, 2, 1), BF16)


def _s5_kernel(uf_ref, ub_ref, p_ref, pt_ref, wb_ref, wc_ref, lam_ref, pw_ref, x0_ref,
               yf_ref, yb_ref, xfin_ref, bu_ref, carry_ref, *, seg_len):
    hs = S5_HSTATE
    lane_chunk = 512

    @pl.when(pl.program_id(1) == 0)
    def _():
        carry_ref[...] = x0_ref[0]

    u_refs = (uf_ref, ub_ref)
    y_refs = (yf_ref, yb_ref)
    units = [(d, hf) for d in range(2) for hf in range(2)]
    n_lc = hs // lane_chunk

    def project(d, hf):
        q = d * 2 + hf
        ub = u_refs[d][0, :, hf * S5_HALF:(hf + 1) * S5_HALF].astype(BF16)
        up = jnp.dot(p_ref[d], ub, preferred_element_type=F32).astype(BF16)
        bu_ref[q] = jnp.dot(up, wb_ref[q], preferred_element_type=F32)

    def recur(d, hf):
        q = d * 2 + hf
        lam_r = [jnp.broadcast_to(lam_ref[q, :, lc * lane_chunk:(lc + 1) * lane_chunk],
                                  (SUBLANES, lane_chunk)) for lc in range(n_lc)]
        lam_i = [jnp.broadcast_to(lam_ref[q, :, hs + lc * lane_chunk:hs + (lc + 1) * lane_chunk],
                                  (SUBLANES, lane_chunk)) for lc in range(n_lc)]

        def scan(xr, xi, store, q=q, lam_r=lam_r, lam_i=lam_i):
            xr, xi = list(xr), list(xi)
            for i in range(seg_len):
                rows = slice(i * SUBLANES, (i + 1) * SUBLANES)
                for lc in range(n_lc):
                    re_l = slice(lc * lane_chunk, (lc + 1) * lane_chunk)
                    im_l = slice(hs + lc * lane_chunk, hs + (lc + 1) * lane_chunk)
                    nr = lam_r[lc] * xr[lc] - lam_i[lc] * xi[lc] + bu_ref[q, rows, re_l]
                    ni = lam_r[lc] * xi[lc] + lam_i[lc] * xr[lc] + bu_ref[q, rows, im_l]
                    if store:
                        bu_ref[q, rows, re_l] = nr
                        bu_ref[q, rows, im_l] = ni
                    xr[lc], xi[lc] = nr, ni
            return xr, xi

        zeros = [jnp.zeros((SUBLANES, lane_chunk), F32) for _ in range(n_lc)]
        xr, xi = scan(zeros, zeros, store=False)
        end_r = jnp.concatenate(xr, axis=1)
        end_i = jnp.concatenate(xi, axis=1)

        lm_r = pw_ref[q, :, 0:hs]
        lm_i = pw_ref[q, :, hs:2 * hs]
        cr = carry_ref[q, :, 0:hs]
        ci = carry_ref[q, :, hs:2 * hs]
        cin_rows_r, cin_rows_i = [], []
        for s in range(SUBLANES):
            cin_rows_r.append(cr)
            cin_rows_i.append(ci)
            nr = lm_r * cr - lm_i * ci + end_r[s:s + 1]
            ni = lm_r * ci + lm_i * cr + end_i[s:s + 1]
            cr, ci = nr, ni
        carry_ref[q, :, 0:hs] = cr
        carry_ref[q, :, hs:2 * hs] = ci
        cin_r = jnp.concatenate(cin_rows_r, axis=0)
        cin_i = jnp.concatenate(cin_rows_i, axis=0)

        scan([cin_r[:, lc * lane_chunk:(lc + 1) * lane_chunk] for lc in range(n_lc)],
             [cin_i[:, lc * lane_chunk:(lc + 1) * lane_chunk] for lc in range(n_lc)], store=True)

    def readout(d, hf):
        q = d * 2 + hf
        yp = jnp.dot(bu_ref[q].astype(BF16), wc_ref[q], preferred_element_type=F32)
        y_refs[d][0, :, hf * S5_HALF:(hf + 1) * S5_HALF] = _dot_xr(pt_ref[d], yp)

    n_units = len(units)
    project(*units[0])
    for i in range(n_units):
        if i + 1 < n_units:
            project(*units[i + 1])
        recur(*units[i])
        if i > 0:
            readout(*units[i - 1])
    readout(*units[n_units - 1])

    xfin_ref[0] = carry_ref[...]


def _s5_scan(z, tabs, x0):
    bsz, n, _ = z.shape
    tile = min(S5_TILE, n)
    seg_len = tile // SUBLANES
    nk = n // tile
    wb, wc, lam, pw = tabs[seg_len]
    perm, perm_t = _s5_perms(tile, seg_len)
    const4 = lambda b, k: (0, 0, 0)
    return pl.pallas_call(
        functools.partial(_s5_kernel, seg_len=seg_len),
        grid=(bsz, nk),
        in_specs=[pl.BlockSpec((1, tile, S5_WIDTH), lambda b, k: (b, k, ZB_S5_U)),
                  pl.BlockSpec((1, tile, S5_WIDTH), lambda b, k: (b, nk - 1 - k, ZB_S5_U)),
                  pl.BlockSpec((2, tile, tile), const4),
                  pl.BlockSpec((2, tile, tile), const4),
                  pl.BlockSpec((4, S5_HALF, 2 * S5_HSTATE), const4),
                  pl.BlockSpec((4, 2 * S5_HSTATE, S5_HALF), const4),
                  pl.BlockSpec((4, 1, 2 * S5_HSTATE), const4),
                  pl.BlockSpec((4, 1, 2 * S5_HSTATE), const4),
                  pl.BlockSpec((1, 4, 1, 2 * S5_HSTATE), lambda b, k: (b, 0, 0, 0))],
        out_specs=[pl.BlockSpec((1, tile, S5_WIDTH), lambda b, k: (b, k, 0)),
                   pl.BlockSpec((1, tile, S5_WIDTH), lambda b, k: (b, nk - 1 - k, 0)),
                   pl.BlockSpec((1, 4, 1, 2 * S5_HSTATE), lambda b, k: (b, 0, 0, 0))],
        out_shape=[jax.ShapeDtypeStruct((bsz, n, S5_WIDTH), F32),
                   jax.ShapeDtypeStruct((bsz, n, S5_WIDTH), F32),
                   jax.ShapeDtypeStruct((bsz, 4, 1, 2 * S5_HSTATE), F32)],
        scratch_shapes=[pltpu.VMEM((4, tile, 2 * S5_HSTATE), F32),
                        pltpu.VMEM((4, 1, 2 * S5_HSTATE), F32)],
        compiler_params=_cparams("parallel", "arbitrary"),
    )(z, z, perm, perm_t, wb, wc, lam, pw, x0)


def _pool_matrices(row_len, tile):
    mats = []
    for w in POOL_WINDOWS:
        lo = w // 2
        m = np.zeros((tile, tile), np.float32)
        for t in range(tile):
            base, j = (t // row_len) * row_len, t % row_len
            m[t, base + max(j - lo, 0):base + min(j - lo + w - 1, row_len - 1) + 1] = 1.0
        mats.append(m)
    return jnp.asarray(np.stack(mats), BF16)


def _window_count(idx, n, w):
    lo = w // 2
    hi_i = jnp.minimum(idx - lo + w - 1, n - 1)
    lo_i = jnp.maximum(idx - lo, 0)
    return (hi_i - lo_i + 1).astype(F32)


def _pool_kernel(*refs, on_grid, tile, n_rows):
    if on_grid:
        u_ref, p_ref, n_ref, g_ref, m_ref, wp_ref, sc_ref, o_ref = refs
    else:
        u_ref, g_ref, m_ref, wp_ref, sc_ref, o_ref = refs
    k = pl.program_id(1)
    cur = u_ref[0]
    if on_grid:
        prev = jnp.where(k > 0, p_ref[0], 0.0)
        nxt = jnp.where(k < pl.num_programs(1) - 1, n_ref[0], 0.0)
    tok = k * tile + lax.broadcasted_iota(jnp.int32, (tile, POOL_GROUP), 0)
    for g, w in enumerate(POOL_WINDOWS):
        gl = slice(g * POOL_GROUP, (g + 1) * POOL_GROUP)
        if on_grid:
            ext = jnp.concatenate([prev[:, gl], cur[:, gl], nxt[:, gl]], axis=0)
            first = tile - (w // 2) * GRID_W
            ext = sum(ext[first + dr * GRID_W:first + dr * GRID_W + tile] for dr in range(w))
            row = jnp.right_shift(tok, int(math.log2(GRID_W)))
            col = jnp.bitwise_and(tok, GRID_W - 1)
            cnt = _window_count(row, n_rows, w) * _window_count(col, GRID_W, w)
        else:
            ext = cur[:, gl]
            cnt = _window_count(tok, tile, w)
        hi, lo = _split(ext, 2)
        box2 = jnp.dot(m_ref[g], jnp.concatenate([hi, lo], axis=1), preferred_element_type=F32)
        box = box2[:, 0:POOL_GROUP] + box2[:, POOL_GROUP:2 * POOL_GROUP]
        diff = box / cnt - cur[:, gl]
        y = _dot(diff, wp_ref[g]) * sc_ref[:, gl]
        o_ref[0, :, gl] = y * _silu(g_ref[0, :, gl])


def _pool_branch(z, w_pool, pool_scale, on_grid):
    bsz, n, _ = z.shape
    sc = pool_scale.reshape(1, POOL_WIDTH)
    if on_grid:
        tile = POOL_ROWS * GRID_W
        nk = n // tile
        mats = _pool_matrices(GRID_W, tile)
        in_specs = [pl.BlockSpec((1, tile, POOL_WIDTH), lambda b, k: (b, k, ZB_POOL_U)),
                    pl.BlockSpec((1, tile, POOL_WIDTH), lambda b, k: (b, jnp.maximum(k - 1, 0), ZB_POOL_U)),
                    pl.BlockSpec((1, tile, POOL_WIDTH), lambda b, k: (b, jnp.minimum(k + 1, nk - 1), ZB_POOL_U)),
                    pl.BlockSpec((1, tile, POOL_WIDTH), lambda b, k: (b, k, ZB_POOL_G))]
        args = (z, z, z, z)
    else:
        tile = n
        nk = 1
        mats = _pool_matrices(tile, tile)
        in_specs = [pl.BlockSpec((1, tile, POOL_WIDTH), lambda b, k: (b, k, ZB_POOL_U)),
                    pl.BlockSpec((1, tile, POOL_WIDTH), lambda b, k: (b, k, ZB_POOL_G))]
        args = (z, z)
    in_specs += [pl.BlockSpec(mats.shape, lambda b, k: (0, 0, 0)),
                 pl.BlockSpec(w_pool.shape, lambda b, k: (0, 0, 0)),
                 pl.BlockSpec((1, POOL_WIDTH), lambda b, k: (0, 0))]
    return pl.pallas_call(
        functools.partial(_pool_kernel, on_grid=on_grid, tile=tile, n_rows=n // GRID_W),
        grid=(bsz, nk),
        in_specs=in_specs,
        out_specs=pl.BlockSpec((1, tile, POOL_WIDTH), lambda b, k: (b, k, 0)),
        out_shape=jax.ShapeDtypeStruct((bsz, n, POOL_WIDTH), F32),
        compiler_params=_cparams("parallel", "parallel"),
    )(*args, mats, w_pool, sc)


def _rw_consts(tile):
    ch = np.arange(RWKV_WIDTH)
    e1 = (ch[:, None] // RWKV_HEAD == np.arange(LANES)[None, :]).astype(np.float32)
    t = np.arange(tile)
    same = (t[:, None] // RW_CHUNK) == (t[None, :] // RW_CHUNK)
    tri = np.stack([same & (t[None, :] <= t[:, None]), same & (t[None, :] >= t[:, None])]).astype(np.float32)
    return jnp.asarray(e1, BF16), jnp.asarray(e1.T, BF16), jnp.asarray(tri, BF16)


def _rw_kernel(zf_ref, zfp_ref, zfn_ref, cf_ref, zb_ref, zbp_ref, zbn_ref, cb_ref,
               conv_ref, w0_ref, w2_ref, a0_ref, a2_ref, kk_ref, ka_ref, rk_ref,
               e1_ref, e2_ref, tri_ref, s0_ref,
               of_ref, ob_ref, bonus_ref, sfin_ref,
               s_ref, ar_ref, bk_ref, v_ref, g_ref, *, tile, nk):
    k = pl.program_id(1)
    n_chunks = tile // RW_CHUNK
    w = RWKV_WIDTH
    pipelined = nk > 1
    k_in = jnp.minimum(k, nk - 1)
    if pipelined:
        slot = lax.rem(k, 2)
        cslot = 1 - slot

        @pl.when(k == 0)
        def _():
            ar_ref[1] = jnp.zeros(ar_ref.shape[1:], BF16)
            bk_ref[1] = jnp.zeros(bk_ref.shape[1:], BF16)
            v_ref[1] = jnp.zeros(v_ref.shape[1:], BF16)
            g_ref[1] = jnp.zeros(g_ref.shape[1:], F32)

        @pl.when(k <= 1)
        def _():
            s_ref[...] = s0_ref[0]
    else:
        slot = cslot = 0
        s_ref[...] = s0_ref[0]

    st_row = lax.broadcasted_iota(jnp.int32, (2 * RW_CHUNK, LANES), 0)
    st_lane = lax.broadcasted_iota(jnp.int32, (2 * RW_CHUNK, LANES), 1)
    chunk_bits = int(math.log2(RW_CHUNK))
    head_mask = (jnp.right_shift(st_row, chunk_bits)
                 == jnp.right_shift(st_lane, chunk_bits)).astype(F32).astype(BF16)
    t_row = lax.broadcasted_iota(jnp.int32, (RW_CHUNK, LANES), 0)
    s_col = jnp.bitwise_and(lax.broadcasted_iota(jnp.int32, (RW_CHUNK, LANES), 1), RW_CHUNK - 1)
    eye = (t_row == s_col).astype(F32)

    def prepare(d):
        z_ref, zp_ref, zn_ref, c_ref = ((zf_ref, zfp_ref, zfn_ref, cf_ref) if d == 0
                                        else (zb_ref, zbp_ref, zbn_ref, cb_ref))
        tidx = k_in if d == 0 else nk - 1 - k_in

        z = z_ref[0]
        halo = jnp.concatenate([jnp.where(tidx > 0, zp_ref[0], 0.0), z,
                                jnp.where(tidx < nk - 1, zn_ref[0], 0.0)], axis=0)
        ext_rows = tile + 2 * SUBLANES
        z_prev = pltpu.roll(halo, 1, axis=0)[SUBLANES:SUBLANES + tile]
        z_next = pltpu.roll(halo, ext_rows - 1, axis=0)[SUBLANES:SUBLANES + tile]
        yield
        conv = conv_ref[0:1, :] * z_prev + conv_ref[1:2, :] * z + conv_ref[2:3, :] * z_next
        r = conv[:, 0:w]
        kx = conv[:, w:2 * w]
        v = conv[:, 2 * w:3 * w]
        yield
        kk = kx * kk_ref[...]
        kk_sq = (kk * kk).astype(BF16)
        codes = c_ref[0]
        tw = jnp.tanh(codes[:, 0:LANES])
        ac = codes[:, LANES:2 * LANES]
        yield
        ss = _dot(kk_sq, e1_ref[...])
        w_pre = _dot(tw, w2_ref[d])
        yield
        a_pre = _dot(ac, a2_ref[d])
        if d == 0:
            a_other_pre = _dot(ac, a2_ref[1])
        yield
        inv = 1.0 / jnp.maximum(jnp.sqrt(ss), L2_EPS)
        kappa_scale = _dot_xl(inv, e2_ref[...])
        yield
        lw = -RWKV_DECAY_SCALE * jax.nn.sigmoid(w0_ref[d] + w_pre)
        yield
        cl = _dot_xr(tri_ref[d], lw)
        yield
        a = jax.nn.sigmoid(a0_ref[d] + a_pre)
        yield
        if d == 0:
            a_other = jax.nn.sigmoid(a0_ref[1] + a_other_pre)
            k_sum = kx * (2.0 + (a + a_other - 2.0) * ka_ref[...])
            bonus_heads = _dot(r * k_sum * rk_ref[...], e1_ref[...])
        yield
        kappa = kk * kappa_scale
        kd = kx * (1.0 + (a - 1.0) * ka_ref[...])
        alpha = kappa * a
        yield
        if d == 0:
            bonus_ref[0] = _dot_xl(bonus_heads, e2_ref[...]) * v
        yield
        gi = jnp.exp(-cl)
        a_t = (-kappa * jnp.exp(cl - lw)).astype(BF16)
        yield
        r_t = (r * jnp.exp(cl)).astype(BF16)
        b_t = (alpha * gi).astype(BF16)
        yield
        k_t = (kd * gi).astype(BF16)
        v_t = v.astype(BF16)
        yield "stores next"

        for c in range(n_chunks):
            rs = slice(c * RW_CHUNK, (c + 1) * RW_CHUNK)
            ar_ref[slot, d, c, 0:RW_CHUNK, :] = a_t[rs]
            ar_ref[slot, d, c, RW_CHUNK:2 * RW_CHUNK, :] = r_t[rs]
            bk_ref[slot, d, c, 0:RW_CHUNK, :] = b_t[rs]
            bk_ref[slot, d, c, RW_CHUNK:2 * RW_CHUNK, :] = k_t[rs]
            g0 = c * RW_CHUNK + (RW_CHUNK - SUBLANES if d == 0 else 0)
            g_ref[slot, d, c * SUBLANES:(c + 1) * SUBLANES, :] = jnp.exp(cl[g0:g0 + SUBLANES])
        v_ref[slot, d] = v_t
        yield

    tri_s =[(s_col < t_row).astype(F32), (s_col > t_row).astype(F32)]
    tri_i = [(s_col <= t_row).astype(F32), (s_col >= t_row).astype(F32)]
    o_refs = (of_ref, ob_ref)
    tc = RW_CHUNK
    chain_groups = [[(d, p) for p in range(RW_PAIRS) for d in range(2)]]

    def block_diag(x):
        xb = x.astype(BF16)
        return jnp.concatenate([xb, xb], axis=0) * head_mask

    def recur(ci, chains):
        cidx = (ci, n_chunks - 1 - ci)

        def lanes(c):
            return slice(c[1] * LANES, (c[1] + 1) * LANES)

        ar = {c: ar_ref[cslot, c[0], cidx[c[0]], :, lanes(c)] for c in chains}
        bk64 = {c: bk_ref[cslot, c[0], cidx[c[0]], :, lanes(c)] for c in chains}
        bk = {c: jnp.concatenate([block_diag(bk64[c][0:tc]), block_diag(bk64[c][tc:2 * tc])], axis=0)
              for c in chains}
        v_st = {c: block_diag(v_ref[cslot, c[0], cidx[c[0]] * tc:(cidx[c[0]] + 1) * tc, lanes(c)])
                for c in chains}
        yield
        l_ab, l_ak, m_rbk = {}, {}, {}
        for c in chains:
            gram = _dot_nt(ar[c], bk[c])
            l_ab[c] = gram[0:tc, 0:LANES] * tri_s[c[0]]
            l_ak[c] = (gram[0:tc, LANES:2 * LANES] * tri_s[c[0]]).astype(BF16)
            m_rbk[c] = jnp.concatenate([gram[tc:2 * tc, 0:LANES] * tri_i[c[0]],
                                        gram[tc:2 * tc, LANES:2 * LANES] * tri_i[c[0]]],
                                       axis=1).astype(BF16)
        yield
        ar_s = {c: _dot_nt(ar[c], s_ref[c[0], c[1]]) for c in chains}
        yield
        yield
        rhs = {c: ar_s[c][0:tc] + _dot(l_ak[c], v_st[c]) for c in chains}
        yield
        inv_m = {c: eye + l_ab[c] for c in chains}
        pw = {c: _dot(l_ab[c], block_diag(l_ab[c])) for c in chains}
        yield
        n_sq = int(math.log2(RW_CHUNK))
        for j in range(1, n_sq - 1):
            both = {c: _dot(jnp.concatenate([pw[c], inv_m[c]], axis=0), block_diag(pw[c])) for c in chains}
            pw = {c: both[c][0:tc] for c in chains}
            inv_m = {c: inv_m[c] + both[c][tc:2 * tc] for c in chains}
            yield
        inv_m = {c: inv_m[c] + _dot(inv_m[c], block_diag(pw[c])) for c in chains}
        yield
        uv = {c: jnp.concatenate([block_diag(_dot(inv_m[c], block_diag(rhs[c]))), v_st[c]], axis=0)
              for c in chains}
        yield
        for c in chains:
            d = c[0]
            o_refs[d][0, cidx[d] * tc:(cidx[d] + 1) * tc, lanes(c)] = (
                ar_s[c][tc:2 * tc] + _dot(m_rbk[c], uv[c]))
        yield
        for c in chains:
            d, p = c
            g_pick = SUBLANES - 1 if d == 0 else 0
            g_tot = g_ref[cslot, d, cidx[d] * SUBLANES:(cidx[d] + 1) * SUBLANES, lanes(c)]
            ds = _dot_tn(uv[c], bk[c])
            s_ref[d, p] = (s_ref[d, p] + ds) * g_tot[g_pick:g_pick + 1]

    prep_gens = [prepare(0), prepare(1)]

    def prep_pieces():
        active = list(prep_gens)
        while active:
            for gen in list(active):
                if next(gen) == "stores next":
                    active.remove(gen)
                yield

    def recur_pieces():
        for ci in range(n_chunks):
            for chains in chain_groups:
                yield from recur(ci, chains)
                yield

    def store_prepared():
        for gen in prep_gens:
            assert next(gen, "end") is None and next(gen, "end") == "end"

    prep_stream = prep_pieces()
    n_prep = 2 * 15
    if pipelined:
        prep_span = (n_chunks * 14 * 6) // 7
        issued = 0
        for r, _ in enumerate(recur_pieces()):
            want = min(n_prep, ((r + 1) * n_prep) // prep_span)
            while issued < want:
                next(prep_stream)
                issued += 1
        assert issued == n_prep and next(prep_stream, "end") == "end"
        store_prepared()

        @pl.when(k == nk)
        def _():
            sfin_ref[0] = s_ref[...]
    else:
        assert len(list(prep_stream)) == n_prep
        store_prepared()
        for _ in recur_pieces():
            pass
        sfin_ref[0] = s_ref[...]


def _rwkv_scan(z, conv_rkv, prm, s0):
    bsz, n, _ = z.shape
    tile = min(RW_TILE, n)
    nk = n // tile
    hb = tile // SUBLANES
    nh = n // SUBLANES
    e1, e2, tri = _rw_consts(tile)
    w3 = 3 * RWKV_WIDTH

    def zspecs(tmap):
        return [pl.BlockSpec((1, tile, w3), lambda b, k: (b, tmap(k), ZB_RKV)),
                pl.BlockSpec((1, SUBLANES, w3), lambda b, k: (b, jnp.maximum(tmap(k) * hb - 1, 0), ZB_RKV)),
                pl.BlockSpec((1, SUBLANES, w3), lambda b, k: (b, jnp.minimum((tmap(k) + 1) * hb, nh - 1), ZB_RKV)),
                pl.BlockSpec((1, tile, 2 * LANES), lambda b, k: (b, tmap(k), ZB_CODES))]

    fwd = lambda k: jnp.minimum(k, nk - 1)
    bwd = lambda k: nk - 1 - jnp.minimum(k, nk - 1)
    done = lambda k: jnp.maximum(k - 1, 0)
    c2 = lambda b, k: (0, 0)
    c3 = lambda b, k: (0, 0, 0)
    param_specs = [pl.BlockSpec((3, w3), c2),
                   pl.BlockSpec((2, 1, RWKV_WIDTH), c3), pl.BlockSpec((2, LANES, RWKV_WIDTH), c3),
                   pl.BlockSpec((2, 1, RWKV_WIDTH), c3), pl.BlockSpec((2, LANES, RWKV_WIDTH), c3),
                   pl.BlockSpec((1, RWKV_WIDTH), c2), pl.BlockSpec((1, RWKV_WIDTH), c2),
                   pl.BlockSpec((1, RWKV_WIDTH), c2),
                   pl.BlockSpec(e1.shape, c2), pl.BlockSpec(e2.shape, c2), pl.BlockSpec(tri.shape, c3),
                   pl.BlockSpec((1, 2, RW_PAIRS, LANES, LANES), lambda b, k: (b, 0, 0, 0, 0))]
    tok = jax.ShapeDtypeStruct((bsz, n, RWKV_WIDTH), F32)
    n_chunks = tile // RW_CHUNK
    stacked = pltpu.VMEM((2, 2, n_chunks, 2 * RW_CHUNK, RWKV_WIDTH), BF16)
    return pl.pallas_call(
        functools.partial(_rw_kernel, tile=tile, nk=nk),
        grid=(bsz, nk + 1 if nk > 1 else 1),
        in_specs=zspecs(fwd) + zspecs(bwd) + param_specs,
        out_specs=[pl.BlockSpec((1, tile, RWKV_WIDTH), lambda b, k: (b, done(k), 0)),
                   pl.BlockSpec((1, tile, RWKV_WIDTH), lambda b, k: (b, nk - 1 - done(k), 0)),
                   pl.BlockSpec((1, tile, RWKV_WIDTH), lambda b, k: (b, fwd(k), 0)),
                   pl.BlockSpec((1, 2, RW_PAIRS, LANES, LANES), lambda b, k: (b, 0, 0, 0, 0))],
        out_shape=[tok, tok, tok, jax.ShapeDtypeStruct((bsz, 2, RW_PAIRS, LANES, LANES), F32)],
        scratch_shapes=[pltpu.VMEM((2, RW_PAIRS, LANES, LANES), F32), stacked, stacked,
                        pltpu.VMEM((2, 2, tile, RWKV_WIDTH), BF16),
                        pltpu.VMEM((2, 2, n_chunks * SUBLANES, RWKV_WIDTH), F32)],
        compiler_params=_cparams("parallel", "arbitrary"),
    )(z, z, z, z, z, z, z, z, conv_rkv, *prm, e1, e2, tri, s0)


def _rw_params(w0, w2, a0, a2, k_k, k_a, r_k):
    zeros = jnp.zeros((RWKV_LORA, RWKV_WIDTH), F32)
    w2p = jnp.stack([jnp.concatenate([w2[0], zeros]), jnp.concatenate([zeros, w2[1]])])
    a2p = jnp.stack([jnp.concatenate([a2[0], zeros]), jnp.concatenate([zeros, a2[1]])])
    return (w0.reshape(2, 1, RWKV_WIDTH), w2p, a0.reshape(2, 1, RWKV_WIDTH), a2p,
            k_k.reshape(1, RWKV_WIDTH), k_a.reshape(1, RWKV_WIDTH), r_k.reshape(1, RWKV_WIDTH))


def _out_kernel(x_ref, su_ref, sg_ref, rg_ref, yf_ref, yb_ref, yp_ref, of_ref, ob_ref, bn_ref,
                gate_ref, sd_ref, wg_ref, bg_ref, gw_ref, gb_ref, e1_ref, e2_ref, wo_ref,
                lg_ref, lb_ref, o_ref, *, alpha):
    y = sd_ref[...] * su_ref[0] + yf_ref[0] + yb_ref[0]
    y = _gelu_tanh(y)
    y = y * jax.nn.sigmoid(_dot(y, wg_ref[...]) + bg_ref[...])
    m1 = y * _silu(sg_ref[0])
    o_sum = of_ref[0] + ob_ref[0]

    inv_n = 1.0 / RWKV_HEAD
    dev = o_sum - _dot_xl(_dot_xl(o_sum, e1_ref[...]) * inv_n, e2_ref[...])
    rstd = lax.rsqrt(_dot(dev * dev, e1_ref[...]) * inv_n + GN_EPS)
    on = dev * _dot_xl(rstd, e2_ref[...]) * gw_ref[...] + gb_ref[...]
    m3 = (on + bn_ref[0]) * _silu(rg_ref[0])
    out = (_dot(m1, wo_ref[0:S5_WIDTH, :]) + _dot(yp_ref[0], wo_ref[S5_WIDTH:S5_WIDTH + POOL_WIDTH, :])
           + _dot(m3, wo_ref[S5_WIDTH + POOL_WIDTH:, :]))
    t = alpha * x_ref[0] + gate_ref[0] * out
    mu = jnp.mean(t, axis=-1, keepdims=True)
    tc = t - mu
    var_t = jnp.mean(tc * tc, axis=-1, keepdims=True)
    o_ref[0] = tc * lax.rsqrt(var_t + LN_EPS) * lg_ref[...] + lb_ref[...]


def _out_proj(x, z, yf, yb, ypool, o_f, o_b, bonus, gate, s5_d, w_glu, b_glu, gn_w, gn_b,
              w_out_bf16, layer, ln_g, ln_b, alpha):
    bsz, n, d = x.shape
    tm = min(OUT_TM, n)
    e1, e2, _ = _rw_consts(RW_CHUNK)
    tok = lambda b, i: (b, i, 0)
    c2 = lambda b, i: (0, 0)

    def zs(width, blk):
        return pl.BlockSpec((1, tm, width), lambda b, i: (b, i, blk))

    return pl.pallas_call(
        functools.partial(_out_kernel, alpha=alpha),
        grid=(bsz, n // tm),
        in_specs=[pl.BlockSpec((1, tm, d), tok),
                  zs(S5_WIDTH, ZB_S5_U), zs(S5_WIDTH, ZB_S5_G), zs(RWKV_WIDTH, ZB_RWKV_G),
                  pl.BlockSpec((1, tm, S5_WIDTH), tok), pl.BlockSpec((1, tm, S5_WIDTH), tok),
                  pl.BlockSpec((1, tm, POOL_WIDTH), tok),
                  pl.BlockSpec((1, tm, RWKV_WIDTH), tok), pl.BlockSpec((1, tm, RWKV_WIDTH), tok),
                  pl.BlockSpec((1, tm, RWKV_WIDTH), tok),
                  pl.BlockSpec((1, 1, d), lambda b, i: (b, 0, 0)),
                  pl.BlockSpec((1, S5_WIDTH), c2), pl.BlockSpec((S5_WIDTH, S5_WIDTH), c2),
                  pl.BlockSpec((1, S5_WIDTH), c2),
                  pl.BlockSpec((1, RWKV_WIDTH), c2), pl.BlockSpec((1, RWKV_WIDTH), c2),
                  pl.BlockSpec(e1.shape, c2), pl.BlockSpec(e2.shape, c2),
                  pl.BlockSpec((None, d, d), lambda b, i: (layer, 0, 0)),
                  pl.BlockSpec((1, d), c2), pl.BlockSpec((1, d), c2)],
        out_specs=pl.BlockSpec((1, tm, d), tok),
        out_shape=jax.ShapeDtypeStruct((bsz, n, d), F32),
        compiler_params=_cparams("parallel", "parallel"),
    )(x, z, z, z, yf, yb, ypool, o_f, o_b, bonus, gate,
      s5_d.reshape(1, -1), w_glu, b_glu.reshape(1, -1), gn_w.reshape(1, -1), gn_b.reshape(1, -1),
      e1, e2, w_out_bf16, ln_g.reshape(1, -1), ln_b.reshape(1, -1))


def _permute_w_in(w):
    return jnp.concatenate([w[..., 2048:6144], w[..., 0:2048], w[..., 6144:6400]], axis=-1).astype(BF16)


def kernel(x, c, ctx, c_ctx, w_ada, b_ada, w_in, conv_rkv, s5_lam_re, s5_lam_im, s5_log_step,
           s5_b_re, s5_b_im, s5_c_re, s5_c_im, s5_d, w_glu, b_glu, w_pool, pool_scale,
           rwkv_w0, rwkv_w2, rwkv_a0, rwkv_a2, rwkv_k_k, rwkv_k_a, rwkv_r_k, gn_w, gn_b,
           w_out, ln_g, ln_b):
    bsz, n, d = x.shape
    n_ctx = ctx.shape[1]
    depth = w_ada.shape[0]
    alpha = (2 * depth) ** 0.25
    assert bsz + 1 <= SUBLANES and n % (POOL_ROWS * GRID_W) == 0
    assert all(n % min(t, n) == 0 for t in (IN_TM, S5_TILE, RW_TILE, OUT_TM))
    assert n_ctx % RW_TILE == 0 and n_ctx <= S5_TILE

    cond = jnp.zeros((SUBLANES, d), F32).at[0:bsz].set(c.astype(F32)).at[bsz].set(c_ctx.astype(F32))
    mod = _modulation(cond, w_ada, b_ada)

    s5_zero = jnp.zeros((bsz, 4, 1, 2 * S5_HSTATE), F32)
    rw_zero = jnp.zeros((bsz, 2, RW_PAIRS, LANES, LANES), F32)
    seg_lens = sorted({min(S5_TILE, n) // SUBLANES, min(S5_TILE, n_ctx) // SUBLANES})
    w_in_bf16 = _permute_w_in(w_in)
    w_out_bf16 = w_out.astype(BF16)

    xc = ctx
    for l in range(depth):
        ctx_out = l < depth - 1
        shift, scale, gate = (mod[l, 0:bsz, i * d:(i + 1) * d].reshape(bsz, 1, d) for i in range(3))
        shift_c, scale_c, gate_c = (jnp.broadcast_to(mod[l, bsz, i * d:(i + 1) * d], (bsz, 1, d))
                                    for i in range(3))
        s5_tabs = _s5_tables(s5_lam_re[l], s5_lam_im[l], s5_log_step[l], s5_b_re[l], s5_b_im[l],
                             s5_c_re[l], s5_c_im[l], seg_lens)
        rw_prm = _rw_params(rwkv_w0[l], rwkv_w2[l], rwkv_a0[l], rwkv_a2[l],
                            rwkv_k_k[l], rwkv_k_a[l], rwkv_r_k[l])

        zc = _in_proj(xc, shift_c, scale_c, w_in_bf16, l)
        z = _in_proj(x, shift, scale, w_in_bf16, l)

        yf_c, yb_c, s5_fin = _s5_scan(zc, s5_tabs, s5_zero)
        yf, yb, _ = _s5_scan(z, s5_tabs, s5_fin)

        of_c, ob_c, bonus_c, rw_fin = _rwkv_scan(zc, conv_rkv[l], rw_prm, rw_zero)
        o_f, o_b, bonus, _ = _rwkv_scan(z, conv_rkv[l], rw_prm, rw_fin)

        ypool = _pool_branch(z, w_pool[l], pool_scale[l], on_grid=True)
        tail = (s5_d[l], w_glu[l], b_glu[l], gn_w[l], gn_b[l], w_out_bf16, l, ln_g[l], ln_b[l], alpha)
        x_new = _out_proj(x, z, yf, yb, ypool, o_f, o_b, bonus, gate, *tail)
        if ctx_out:
            ypool_c = _pool_branch(zc, w_pool[l], pool_scale[l], on_grid=False)
            xc = _out_proj(xc, zc, yf_c, yb_c, ypool_c, of_c, ob_c, bonus_c, gate_c, *tail)
        x = x_new
    return x
```

```python
import functools
import math

import numpy as np
import jax
import jax.numpy as jnp
from jax import lax
from jax.experimental import pallas as pl
from jax.experimental.pallas import tpu as pltpu

F32 = jnp.float32
BF16 = jnp.bfloat16

D_MODEL = 2048
GRID_W = 64
S5_WIDTH = 512
S5_GROUP = 16
S5_GROUPS = 32
S5_STATE = 64
POOL_WIDTH = 512
POOL_WINDOWS = (2, 4, 8, 16)
POOL_GROUP = 128
RWKV_WIDTH = 1024
RWKV_HEAD = 64
RWKV_HEADS = 16
RWKV_LORA = 64
D_IN = 6400
RWKV_DECAY_SCALE = 0.606531
S5_MAX_RE = -1e-4
ADALN_EPS = 1e-6
LN_EPS = 1e-5
GN_EPS = 64e-5
L2_EPS = 1e-12

ZB_RKV = 0
ZB_RWKV_G = 3
ZB_S5_U = 8
ZB_S5_G = 9
ZB_POOL_U = 10
ZB_POOL_G = 11
ZB_CODES = 24

LANES = 128
SUBLANES = 8
MXU_COLS = 256
VMEM_LIMIT = 56 * 1024 * 1024

S5_HALF = MXU_COLS
S5_HSTATE = (S5_HALF // S5_GROUP) * S5_STATE
RW_CHUNK = 64
RW_PAIRS = RWKV_HEADS // 2
POOL_ROWS = 8

MOD_TN = 512
IN_TM = 1024
IN_TN = 5 * MXU_COLS
S5_TILE = 512
RW_TILE = 4 * RW_CHUNK
OUT_TM = 256


def _cparams(*sem):
    return pltpu.CompilerParams(dimension_semantics=sem, vmem_limit_bytes=VMEM_LIMIT)


def _dot(a, b):
    return jnp.dot(a.astype(BF16), b.astype(BF16), preferred_element_type=F32)


def _dot_nt(a, b):
    return lax.dot_general(a.astype(BF16), b.astype(BF16), (((1,), (1,)), ((), ())),
                           preferred_element_type=F32)


def _dot_tn(a, b):
    return lax.dot_general(a.astype(BF16), b.astype(BF16), (((0,), (0,)), ((), ())),
                           preferred_element_type=F32)


def _split(x, parts):
    out = []
    rem = x
    for _ in range(parts):
        hi = rem.astype(BF16)
        out.append(hi)
        rem = rem - hi.astype(F32)
    return out


def _dot_xl(a, b, parts=2):
    acc = None
    for t in _split(a, parts):
        p = jnp.dot(t, b, preferred_element_type=F32)
        acc = p if acc is None else acc + p
    return acc


def _dot_xr(a, b, parts=2):
    acc = None
    for t in _split(b, parts):
        p = jnp.dot(a, t, preferred_element_type=F32)
        acc = p if acc is None else acc + p
    return acc


def _silu(x):
    return x * jax.nn.sigmoid(x)


def _gelu_tanh(x):
    return 0.5 * x * (1.0 + jnp.tanh(math.sqrt(2.0 / math.pi) * (x + 0.044715 * (x * x * x))))


def _mod_kernel(c_ref, w_ref, b_ref, o_ref):
    s = _silu(c_ref[...])
    o_ref[0] = jnp.dot(s, w_ref[0], preferred_element_type=F32,
                       precision=lax.Precision.HIGHEST) + b_ref[0]


def _modulation(cond, w_ada, b_ada):
    depth, d, d3 = w_ada.shape
    tn = MOD_TN
    return pl.pallas_call(
        _mod_kernel,
        grid=(depth, d3 // tn),
        in_specs=[pl.BlockSpec((SUBLANES, d), lambda l, j: (0, 0)),
                  pl.BlockSpec((1, d, tn), lambda l, j: (l, 0, j)),
                  pl.BlockSpec((1, 1, tn), lambda l, j: (l, 0, j))],
        out_specs=pl.BlockSpec((1, SUBLANES, tn), lambda l, j: (l, 0, j)),
        out_shape=jax.ShapeDtypeStruct((depth, SUBLANES, d3), F32),
        compiler_params=_cparams("parallel", "parallel"),
    )(cond, w_ada, b_ada.reshape(depth, 1, d3))


def _in_kernel(x_ref, sh_ref, sc_ref, w_ref, o_ref, h_ref):
    @pl.when(pl.program_id(2) == 0)
    def _():
        x = x_ref[0]
        mu = jnp.mean(x, axis=-1, keepdims=True)
        xc = x - mu
        var = jnp.mean(xc * xc, axis=-1, keepdims=True)
        h = xc * lax.rsqrt(var + ADALN_EPS) * (1.0 + sc_ref[0]) + sh_ref[0]
        h_ref[...] = h.astype(BF16)

    o_ref[0] = jnp.dot(h_ref[...], w_ref[...], preferred_element_type=F32)


def _in_proj(x, shift, scale, w_in_bf16, layer):
    bsz, n, d = x.shape
    tm = min(IN_TM, n)
    tn = IN_TN
    return pl.pallas_call(
        _in_kernel,
        grid=(bsz, n // tm, D_IN // tn),
        in_specs=[pl.BlockSpec((1, tm, d), lambda b, i, j: (b, i, 0)),
                  pl.BlockSpec((1, 1, d), lambda b, i, j: (b, 0, 0)),
                  pl.BlockSpec((1, 1, d), lambda b, i, j: (b, 0, 0)),
                  pl.BlockSpec((None, d, tn), lambda b, i, j: (layer, 0, j))],
        out_specs=pl.BlockSpec((1, tm, tn), lambda b, i, j: (b, i, j)),
        out_shape=jax.ShapeDtypeStruct((bsz, n, D_IN), F32),
        scratch_shapes=[pltpu.VMEM((tm, d), BF16)],
        compiler_params=_cparams("parallel", "parallel", "arbitrary"),
    )(x, shift, scale, w_in_bf16)


def _s5_tables(lam_re, lam_im, log_step, b_re, b_im, c_re, c_im, seg_lens):
    lr = jnp.minimum(lam_re.astype(F32), S5_MAX_RE)
    li = lam_im.astype(F32)
    step = jnp.exp(log_step.astype(F32))[..., None]
    ar, ai = lr * step, li * step
    mag = jnp.exp(ar)
    abr, abi = mag * jnp.cos(ai), mag * jnp.sin(ai)
    den = lr * lr + li * li
    bsr = ((abr - 1.0) * lr + abi * li) / den
    bsi = (abi * lr - (abr - 1.0) * li) / den
    btr = bsr[..., None] * b_re[None] - bsi[..., None] * b_im[None]
    bti = bsr[..., None] * b_im[None] + bsi[..., None] * b_re[None]
    gh = S5_GROUPS // 2
    state_group = jnp.arange(S5_HSTATE) // S5_STATE

    def halves(t):
        return t.reshape((4, gh) + t.shape[2:])

    def b_blocks(t):
        rows = halves(t).transpose(0, 3, 1, 2).reshape(4, 1, S5_GROUP, S5_HSTATE)
        mask = (jnp.arange(gh)[:, None, None] == state_group[None, None, :]).astype(F32)
        return (rows * mask[None]).reshape(4, S5_HALF, S5_HSTATE)

    def c_blocks(t):
        cols = halves(t).transpose(0, 1, 3, 2).reshape(4, S5_HSTATE, 1, S5_GROUP)
        mask = (state_group[:, None, None] == jnp.arange(gh)[None, :, None]).astype(F32)
        return (cols * mask[None]).reshape(4, S5_HSTATE, S5_HALF)

    def lanes(re, im):
        return jnp.concatenate([re.reshape(4, 1, S5_HSTATE), im.reshape(4, 1, S5_HSTATE)], axis=2)

    wb = jnp.concatenate([b_blocks(btr), b_blocks(bti)], axis=2).astype(BF16)
    wc = jnp.concatenate([c_blocks(c_re.astype(F32)), -c_blocks(c_im.astype(F32))], axis=1).astype(BF16)
    lam = lanes(abr, abi)
    out = {}
    for m in seg_lens:
        pm = jnp.exp(float(m) * ar)
        out[m] = (wb, wc, lam, lanes(pm * jnp.cos(float(m) * ai), pm * jnp.sin(float(m) * ai)))
    return out


def _s5_perms(tile, seg_len):
    rows = np.arange(tile)
    tok = (rows % SUBLANES) * seg_len + rows // SUBLANES
    pf = np.zeros((tile, tile), np.float32)
    pf[rows, tok] = 1.0
    pb = np.zeros((tile, tile), np.float32)
    pb[rows, tile - 1 - tok] = 1.0
    perm = np.stack([pf, pb])
    return jnp.asarray(perm, BF16), jnp.asarray(perm.transpose(0, 2, 1), BF16)


def _s5_kernel(uf_ref, ub_ref, p_ref, pt_ref, wb_ref, wc_ref, lam_ref, pw_ref, x0_ref,
               yf_ref, yb_ref, xfin_ref, bu_ref, carry_ref, *, seg_len):
    hs = S5_HSTATE
    lane_chunk = 512

    @pl.when(pl.program_id(1) == 0)
    def _():
        carry_ref[...] = x0_ref[0]

    u_refs = (uf_ref, ub_ref)
    y_refs = (yf_ref, yb_ref)
    units = [(d, hf) for d in range(2) for hf in range(2)]
    n_lc = hs // lane_chunk

    def project(d, hf):
        q = d * 2 + hf
        ub = u_refs[d][0, :, hf * S5_HALF:(hf + 1) * S5_HALF].astype(BF16)
        up = jnp.dot(p_ref[d], ub, preferred_element_type=F32).astype(BF16)
        bu_ref[q] = jnp.dot(up, wb_ref[q], preferred_element_type=F32)

    def recur(d, hf):
        q = d * 2 + hf
        lam_r = [jnp.broadcast_to(lam_ref[q, :, lc * lane_chunk:(lc + 1) * lane_chunk],
                                  (SUBLANES, lane_chunk)) for lc in range(n_lc)]
        lam_i = [jnp.broadcast_to(lam_ref[q, :, hs + lc * lane_chunk:hs + (lc + 1) * lane_chunk],
                                  (SUBLANES, lane_chunk)) for lc in range(n_lc)]

        def scan(xr, xi, store, q=q, lam_r=lam_r, lam_i=lam_i):
            xr, xi = list(xr), list(xi)
            for i in range(seg_len):
                rows = slice(i * SUBLANES, (i + 1) * SUBLANES)
                for lc in range(n_lc):
                    re_l = slice(lc * lane_chunk, (lc + 1) * lane_chunk)
                    im_l = slice(hs + lc * lane_chunk, hs + (lc + 1) * lane_chunk)
                    nr = lam_r[lc] * xr[lc] - lam_i[lc] * xi[lc] + bu_ref[q, rows, re_l]
                    ni = lam_r[lc] * xi[lc] + lam_i[lc] * xr[lc] + bu_ref[q, rows, im_l]
                    if store:
                        bu_ref[q, rows, re_l] = nr
                        bu_ref[q, rows, im_l] = ni
                    xr[lc], xi[lc] = nr, ni
            return xr, xi

        zeros = [jnp.zeros((SUBLANES, lane_chunk), F32) for _ in range(n_lc)]
        xr, xi = scan(zeros, zeros, store=False)
        end_r = jnp.concatenate(xr, axis=1)
        end_i = jnp.concatenate(xi, axis=1)

        lm_r = pw_ref[q, :, 0:hs]
        lm_i = pw_ref[q, :, hs:2 * hs]
        cr = carry_ref[q, :, 0:hs]
        ci = carry_ref[q, :, hs:2 * hs]
        cin_rows_r, cin_rows_i = [], []
        for s in range(SUBLANES):
            cin_rows_r.append(cr)
            cin_rows_i.append(ci)
            nr = lm_r * cr - lm_i * ci + end_r[s:s + 1]
            ni = lm_r * ci + lm_i * cr + end_i[s:s + 1]
            cr, ci = nr, ni
        carry_ref[q, :, 0:hs] = cr
        carry_ref[q, :, hs:2 * hs] = ci
        cin_r = jnp.concatenate(cin_rows_r, axis=0)
        cin_i = jnp.concatenate(cin_rows_i, axis=0)

        scan([cin_r[:, lc * lane_chunk:(lc + 1) * lane_chunk] for lc in range(n_lc)],
             [cin_i[:, lc * lane_chunk:(lc + 1) * lane_chunk] for lc in range(n_lc)], store=True)

    def readout(d, hf):
        q = d * 2 + hf
        yp = jnp.dot(bu_ref[q].astype(BF16), wc_ref[q], preferred_element_type=F32)
        y_refs[d][0, :, hf * S5_HALF:(hf + 1) * S5_HALF] = _dot_xr(pt_ref[d], yp)

    for unit in units:
        project(*unit)
    for unit in units:
        recur(*unit)
    for unit in units:
        readout(*unit)

    xfin_ref[0] = carry_ref[...]


def _s5_scan(z, tabs, x0):
    bsz, n, _ = z.shape
    tile = min(S5_TILE, n)
    seg_len = tile // SUBLANES
    nk = n // tile
    wb, wc, lam, pw = tabs[seg_len]
    perm, perm_t = _s5_perms(tile, seg_len)
    const4 = lambda b, k: (0, 0, 0)
    return pl.pallas_call(
        functools.partial(_s5_kernel, seg_len=seg_len),
        grid=(bsz, nk),
        in_specs=[pl.BlockSpec((1, tile, S5_WIDTH), lambda b, k: (b, k, ZB_S5_U)),
                  pl.BlockSpec((1, tile, S5_WIDTH), lambda b, k: (b, nk - 1 - k, ZB_S5_U)),
                  pl.BlockSpec((2, tile, tile), const4),
                  pl.BlockSpec((2, tile, tile), const4),
                  pl.BlockSpec((4, S5_HALF, 2 * S5_HSTATE), const4),
                  pl.BlockSpec((4, 2 * S5_HSTATE, S5_HALF), const4),
                  pl.BlockSpec((4, 1, 2 * S5_HSTATE), const4),
                  pl.BlockSpec((4, 1, 2 * S5_HSTATE), const4),
                  pl.BlockSpec((1, 4, 1, 2 * S5_HSTATE), lambda b, k: (b, 0, 0, 0))],
        out_specs=[pl.BlockSpec((1, tile, S5_WIDTH), lambda b, k: (b, k, 0)),
                   pl.BlockSpec((1, tile, S5_WIDTH), lambda b, k: (b, nk - 1 - k, 0)),
                   pl.BlockSpec((1, 4, 1, 2 * S5_HSTATE), lambda b, k: (b, 0, 0, 0))],
        out_shape=[jax.ShapeDtypeStruct((bsz, n, S5_WIDTH), F32),
                   jax.ShapeDtypeStruct((bsz, n, S5_WIDTH), F32),
                   jax.ShapeDtypeStruct((bsz, 4, 1, 2 * S5_HSTATE), F32)],
        scratch_shapes=[pltpu.VMEM((4, tile, 2 * S5_HSTATE), F32),
                        pltpu.VMEM((4, 1, 2 * S5_HSTATE), F32)],
        compiler_params=_cparams("parallel", "arbitrary"),
    )(z, z, perm, perm_t, wb, wc, lam, pw, x0)


def _pool_matrices(row_len, tile):
    mats = []
    for w in POOL_WINDOWS:
        lo = w // 2
        m = np.zeros((tile, tile), np.float32)
        for t in range(tile):
            base, j = (t // row_len) * row_len, t % row_len
            m[t, base + max(j - lo, 0):base + min(j - lo + w - 1, row_len - 1) + 1] = 1.0
        mats.append(m)
    return jnp.asarray(np.stack(mats), BF16)


def _window_count(idx, n, w):
    lo = w // 2
    hi_i = jnp.minimum(idx - lo + w - 1, n - 1)
    lo_i = jnp.maximum(idx - lo, 0)
    return (hi_i - lo_i + 1).astype(F32)


def _pool_kernel(*refs, on_grid, tile, n_rows):
    if on_grid:
        u_ref, p_ref, n_ref, g_ref, m_ref, wp_ref, sc_ref, o_ref = refs
    else:
        u_ref, g_ref, m_ref, wp_ref, sc_ref, o_ref = refs
    k = pl.program_id(1)
    cur = u_ref[0]
    if on_grid:
        prev = jnp.where(k > 0, p_ref[0], 0.0)
        nxt = jnp.where(k < pl.num_programs(1) - 1, n_ref[0], 0.0)
    tok = k * tile + lax.broadcasted_iota(jnp.int32, (tile, POOL_GROUP), 0)
    for g, w in enumerate(POOL_WINDOWS):
        gl = slice(g * POOL_GROUP, (g + 1) * POOL_GROUP)
        if on_grid:
            ext = jnp.concatenate([prev[:, gl], cur[:, gl], nxt[:, gl]], axis=0)
            first = tile - (w // 2) * GRID_W
            ext = sum(ext[first + dr * GRID_W:first + dr * GRID_W + tile] for dr in range(w))
            row = jnp.right_shift(tok, int(math.log2(GRID_W)))
            col = jnp.bitwise_and(tok, GRID_W - 1)
            cnt = _window_count(row, n_rows, w) * _window_count(col, GRID_W, w)
        else:
            ext = cur[:, gl]
            cnt = _window_count(tok, tile, w)
        hi, lo = _split(ext, 2)
        box2 = jnp.dot(m_ref[g], jnp.concatenate([hi, lo], axis=1), preferred_element_type=F32)
        box = box2[:, 0:POOL_GROUP] + box2[:, POOL_GROUP:2 * POOL_GROUP]
        diff = box / cnt - cur[:, gl]
        y = _dot(diff, wp_ref[g]) * sc_ref[:, gl]
        o_ref[0, :, gl] = y * _silu(g_ref[0, :, gl])


def _pool_branch(z, w_pool, pool_scale, on_grid):
    bsz, n, _ = z.shape
    sc = pool_scale.reshape(1, POOL_WIDTH)
    if on_grid:
        tile = POOL_ROWS * GRID_W
        nk = n // tile
        mats = _pool_matrices(GRID_W, tile)
        in_specs = [pl.BlockSpec((1, tile, POOL_WIDTH), lambda b, k: (b, k, ZB_POOL_U)),
                    pl.BlockSpec((1, tile, POOL_WIDTH), lambda b, k: (b, jnp.maximum(k - 1, 0), ZB_POOL_U)),
                    pl.BlockSpec((1, tile, POOL_WIDTH), lambda b, k: (b, jnp.minimum(k + 1, nk - 1), ZB_POOL_U)),
                    pl.BlockSpec((1, tile, POOL_WIDTH), lambda b, k: (b, k, ZB_POOL_G))]
        args = (z, z, z, z)
    else:
        tile = n
        nk = 1
        mats = _pool_matrices(tile, tile)
        in_specs = [pl.BlockSpec((1, tile, POOL_WIDTH), lambda b, k: (b, k, ZB_POOL_U)),
                    pl.BlockSpec((1, tile, POOL_WIDTH), lambda b, k: (b, k, ZB_POOL_G))]
        args = (z, z)
    in_specs += [pl.BlockSpec(mats.shape, lambda b, k: (0, 0, 0)),
                 pl.BlockSpec(w_pool.shape, lambda b, k: (0, 0, 0)),
                 pl.BlockSpec((1, POOL_WIDTH), lambda b, k: (0, 0))]
    return pl.pallas_call(
        functools.partial(_pool_kernel, on_grid=on_grid, tile=tile, n_rows=n // GRID_W),
        grid=(bsz, nk),
        in_specs=in_specs,
        out_specs=pl.BlockSpec((1, tile, POOL_WIDTH), lambda b, k: (b, k, 0)),
        out_shape=jax.ShapeDtypeStruct((bsz, n, POOL_WIDTH), F32),
        compiler_params=_cparams("parallel", "parallel"),
    )(*args, mats, w_pool, sc)


def _rw_consts(tile):
    ch = np.arange(RWKV_WIDTH)
    e1 = (ch[:, None] // RWKV_HEAD == np.arange(LANES)[None, :]).astype(np.float32)
    t = np.arange(tile)
    same = (t[:, None] // RW_CHUNK) == (t[None, :] // RW_CHUNK)
    tri = np.stack([same & (t[None, :] <= t[:, None]), same & (t[None, :] >= t[:, None])]).astype(np.float32)
    return jnp.asarray(e1, BF16), jnp.asarray(e1.T, BF16), jnp.asarray(tri, BF16)


def _rw_kernel(zf_ref, zfp_ref, zfn_ref, cf_ref, zb_ref, zbp_ref, zbn_ref, cb_ref,
               conv_ref, w0_ref, w2_ref, a0_ref, a2_ref, kk_ref, ka_ref, rk_ref,
               e1_ref, e2_ref, tri_ref, s0_ref,
               of_ref, ob_ref, bonus_ref, sfin_ref,
               s_ref, ar_ref, bk_ref, v_ref, g_ref, *, tile, nk):
    k = pl.program_id(1)
    n_chunks = tile // RW_CHUNK
    w = RWKV_WIDTH
    pipelined = nk > 1
    k_in = jnp.minimum(k, nk - 1)
    if pipelined:
        slot = lax.rem(k, 2)
        cslot = 1 - slot

        @pl.when(k == 0)
        def _():
            ar_ref[1] = jnp.zeros(ar_ref.shape[1:], BF16)
            bk_ref[1] = jnp.zeros(bk_ref.shape[1:], BF16)
            v_ref[1] = jnp.zeros(v_ref.shape[1:], BF16)
            g_ref[1] = jnp.zeros(g_ref.shape[1:], F32)

        @pl.when(k <= 1)
        def _():
            s_ref[...] = s0_ref[0]
    else:
        slot = cslot = 0
        s_ref[...] = s0_ref[0]

    st_row = lax.broadcasted_iota(jnp.int32, (2 * RW_CHUNK, LANES), 0)
    st_lane = lax.broadcasted_iota(jnp.int32, (2 * RW_CHUNK, LANES), 1)
    chunk_bits = int(math.log2(RW_CHUNK))
    head_mask = (jnp.right_shift(st_row, chunk_bits)
                 == jnp.right_shift(st_lane, chunk_bits)).astype(F32).astype(BF16)
    t_row = lax.broadcasted_iota(jnp.int32, (RW_CHUNK, LANES), 0)
    s_col = jnp.bitwise_and(lax.broadcasted_iota(jnp.int32, (RW_CHUNK, LANES), 1), RW_CHUNK - 1)
    eye = (t_row == s_col).astype(F32)

    def prepare(d):
        z_ref, zp_ref, zn_ref, c_ref = ((zf_ref, zfp_ref, zfn_ref, cf_ref) if d == 0
                                        else (zb_ref, zbp_ref, zbn_ref, cb_ref))
        tidx = k_in if d == 0 else nk - 1 - k_in

        z = z_ref[0]
        halo = jnp.concatenate([jnp.where(tidx > 0, zp_ref[0], 0.0), z,
                                jnp.where(tidx < nk - 1, zn_ref[0], 0.0)], axis=0)
        ext_rows = tile + 2 * SUBLANES
        z_prev = pltpu.roll(halo, 1, axis=0)[SUBLANES:SUBLANES + tile]
        z_next = pltpu.roll(halo, ext_rows - 1, axis=0)[SUBLANES:SUBLANES + tile]
        conv = conv_ref[0:1, :] * z_prev + conv_ref[1:2, :] * z + conv_ref[2:3, :] * z_next
        r = conv[:, 0:w]
        kx = conv[:, w:2 * w]
        v = conv[:, 2 * w:3 * w]
        kk = kx * kk_ref[...]
        ss = _dot(kk * kk, e1_ref[...])
        codes = c_ref[0]
        tw = jnp.tanh(codes[:, 0:LANES])
        ac = codes[:, LANES:2 * LANES]
        w_pre = _dot(tw, w2_ref[d])
        a_pre = _dot(ac, a2_ref[d])
        if d == 0:
            a_other_pre = _dot(ac, a2_ref[1])
        yield

        inv = 1.0 / jnp.maximum(jnp.sqrt(ss), L2_EPS)
        kappa_scale = _dot_xl(inv, e2_ref[...])
        lw = -RWKV_DECAY_SCALE * jax.nn.sigmoid(w0_ref[d] + w_pre)
        cl = _dot_xr(tri_ref[d], lw)
        a = jax.nn.sigmoid(a0_ref[d] + a_pre)
        if d == 0:
            a_other = jax.nn.sigmoid(a0_ref[1] + a_other_pre)
            k_sum = kx * (2.0 + (a + a_other - 2.0) * ka_ref[...])
            bonus_heads = _dot(r * k_sum * rk_ref[...], e1_ref[...])
        yield

        kappa = kk * kappa_scale
        kd = kx * (1.0 + (a - 1.0) * ka_ref[...])
        alpha = kappa * a
        if d == 0:
            bonus_ref[0] = _dot_xl(bonus_heads, e2_ref[...]) * v
        g = jnp.exp(cl)
        gi = jnp.exp(-cl)
        a_t = (-kappa * jnp.exp(cl - lw)).astype(BF16)
        r_t = (r * g).astype(BF16)
        b_t = (alpha * gi).astype(BF16)
        k_t = (kd * gi).astype(BF16)
        v_t = v.astype(BF16)
        yield

        for c in range(n_chunks):
            rs = slice(c * RW_CHUNK, (c + 1) * RW_CHUNK)
            ar_ref[slot, d, c, 0:RW_CHUNK, :] = a_t[rs]
            ar_ref[slot, d, c, RW_CHUNK:2 * RW_CHUNK, :] = r_t[rs]
            bk_ref[slot, d, c, 0:RW_CHUNK, :] = b_t[rs]
            bk_ref[slot, d, c, RW_CHUNK:2 * RW_CHUNK, :] = k_t[rs]
            g0 = c * RW_CHUNK + (RW_CHUNK - SUBLANES if d == 0 else 0)
            g_ref[slot, d, c * SUBLANES:(c + 1) * SUBLANES, :] = g[g0:g0 + SUBLANES]
        v_ref[slot, d] = v_t
        yield

    tri_s =[(s_col < t_row).astype(F32), (s_col > t_row).astype(F32)]
    tri_i = [(s_col <= t_row).astype(F32), (s_col >= t_row).astype(F32)]
    o_refs = (of_ref, ob_ref)
    tc = RW_CHUNK
    chains = [(d, p) for p in range(RW_PAIRS) for d in range(2)]

    def block_diag(x):
        xb = x.astype(BF16)
        return jnp.concatenate([xb, xb], axis=0) * head_mask

    def recur(ci, chains):
        cidx = (ci, n_chunks - 1 - ci)

        def lanes(c):
            return slice(c[1] * LANES, (c[1] + 1) * LANES)

        ar = {c: ar_ref[cslot, c[0], cidx[c[0]], :, lanes(c)] for c in chains}
        bk64 = {c: bk_ref[cslot, c[0], cidx[c[0]], :, lanes(c)] for c in chains}
        bk = {c: jnp.concatenate([block_diag(bk64[c][0:tc]), block_diag(bk64[c][tc:2 * tc])], axis=0)
              for c in chains}
        v_st = {c: block_diag(v_ref[cslot, c[0], cidx[c[0]] * tc:(cidx[c[0]] + 1) * tc, lanes(c)])
                for c in chains}
        gram = {c: _dot_nt(ar[c], bk[c]) for c in chains}
        ar_s = {c: _dot_nt(ar[c], s_ref[c[0], c[1]]) for c in chains}
        l_ab = {c: gram[c][0:tc, 0:LANES] * tri_s[c[0]] for c in chains}
        l_ak = {c: (gram[c][0:tc, LANES:2 * LANES] * tri_s[c[0]]).astype(BF16) for c in chains}
        m_rbk = {c: jnp.concatenate([gram[c][tc:2 * tc, 0:LANES] * tri_i[c[0]],
                                     gram[c][tc:2 * tc, LANES:2 * LANES] * tri_i[c[0]]],
                                    axis=1).astype(BF16) for c in chains}
        rhs = {c: ar_s[c][0:tc] + _dot(l_ak[c], v_st[c]) for c in chains}
        inv_m = {c: eye + l_ab[c] for c in chains}
        pw = {c: _dot(l_ab[c], block_diag(l_ab[c])) for c in chains}
        n_sq = int(math.log2(RW_CHUNK))
        for j in range(1, n_sq - 1):
            both = {c: _dot(jnp.concatenate([pw[c], inv_m[c]], axis=0), block_diag(pw[c])) for c in chains}
            pw = {c: both[c][0:tc] for c in chains}
            inv_m = {c: inv_m[c] + both[c][tc:2 * tc] for c in chains}
        inv_m = {c: inv_m[c] + _dot(inv_m[c], block_diag(pw[c])) for c in chains}
        uv = {c: jnp.concatenate([block_diag(_dot(inv_m[c], block_diag(rhs[c]))), v_st[c]], axis=0)
              for c in chains}
        for c in chains:
            d = c[0]
            o_refs[d][0, cidx[d] * tc:(cidx[d] + 1) * tc, lanes(c)] = (
                ar_s[c][tc:2 * tc] + _dot(m_rbk[c], uv[c]))
        for c in chains:
            d, p = c
            g_pick = SUBLANES - 1 if d == 0 else 0
            g_tot = g_ref[cslot, d, cidx[d] * SUBLANES:(cidx[d] + 1) * SUBLANES, lanes(c)]
            ds = _dot_tn(uv[c], bk[c])
            s_ref[d, p] = (s_ref[d, p] + ds) * g_tot[g_pick:g_pick + 1]

    prep = [prepare(0), prepare(1)]
    n_prep_stages = 4

    def advance_prep():
        for stage in prep:
            next(stage)

    if pipelined:
        assert n_chunks >= n_prep_stages - 1
        for ci in range(n_chunks):
            if ci < n_prep_stages - 1:
                advance_prep()
            recur(ci, chains)
        advance_prep()

        @pl.when(k == nk)
        def _():
            sfin_ref[0] = s_ref[...]
    else:
        for _ in range(n_prep_stages):
            advance_prep()
        for ci in range(n_chunks):
            recur(ci, chains)
        sfin_ref[0] = s_ref[...]
    assert all(next(stage, None) is None for stage in prep)


def _rwkv_scan(z, conv_rkv, prm, s0):
    bsz, n, _ = z.shape
    tile = min(RW_TILE, n)
    nk = n // tile
    hb = tile // SUBLANES
    nh = n // SUBLANES
    e1, e2, tri = _rw_consts(tile)
    w3 = 3 * RWKV_WIDTH

    def zspecs(tmap):
        return [pl.BlockSpec((1, tile, w3), lambda b, k: (b, tmap(k), ZB_RKV)),
                pl.BlockSpec((1, SUBLANES, w3), lambda b, k: (b, jnp.maximum(tmap(k) * hb - 1, 0), ZB_RKV)),
                pl.BlockSpec((1, SUBLANES, w3), lambda b, k: (b, jnp.minimum((tmap(k) + 1) * hb, nh - 1), ZB_RKV)),
                pl.BlockSpec((1, tile, 2 * LANES), lambda b, k: (b, tmap(k), ZB_CODES))]

    fwd = lambda k: jnp.minimum(k, nk - 1)
    bwd = lambda k: nk - 1 - jnp.minimum(k, nk - 1)
    done = lambda k: jnp.maximum(k - 1, 0)
    c2 = lambda b, k: (0, 0)
    c3 = lambda b, k: (0, 0, 0)
    param_specs = [pl.BlockSpec((3, w3), c2),
                   pl.BlockSpec((2, 1, RWKV_WIDTH), c3), pl.BlockSpec((2, LANES, RWKV_WIDTH), c3),
                   pl.BlockSpec((2, 1, RWKV_WIDTH), c3), pl.BlockSpec((2, LANES, RWKV_WIDTH), c3),
                   pl.BlockSpec((1, RWKV_WIDTH), c2), pl.BlockSpec((1, RWKV_WIDTH), c2),
                   pl.BlockSpec((1, RWKV_WIDTH), c2),
                   pl.BlockSpec(e1.shape, c2), pl.BlockSpec(e2.shape, c2), pl.BlockSpec(tri.shape, c3),
                   pl.BlockSpec((1, 2, RW_PAIRS, LANES, LANES), lambda b, k: (b, 0, 0, 0, 0))]
    tok = jax.ShapeDtypeStruct((bsz, n, RWKV_WIDTH), F32)
    n_chunks = tile // RW_CHUNK
    stacked = pltpu.VMEM((2, 2, n_chunks, 2 * RW_CHUNK, RWKV_WIDTH), BF16)
    return pl.pallas_call(
        functools.partial(_rw_kernel, tile=tile, nk=nk),
        grid=(bsz, nk + 1 if nk > 1 else 1),
        in_specs=zspecs(fwd) + zspecs(bwd) + param_specs,
        out_specs=[pl.BlockSpec((1, tile, RWKV_WIDTH), lambda b, k: (b, done(k), 0)),
                   pl.BlockSpec((1, tile, RWKV_WIDTH), lambda b, k: (b, nk - 1 - done(k), 0)),
                   pl.BlockSpec((1, tile, RWKV_WIDTH), lambda b, k: (b, fwd(k), 0)),
                   pl.BlockSpec((1, 2, RW_PAIRS, LANES, LANES), lambda b, k: (b, 0, 0, 0, 0))],
        out_shape=[tok, tok, tok, jax.ShapeDtypeStruct((bsz, 2, RW_PAIRS, LANES, LANES), F32)],
        scratch_shapes=[pltpu.VMEM((2, RW_PAIRS, LANES, LANES), F32), stacked, stacked,
                        pltpu.VMEM((2, 2, tile, RWKV_WIDTH), BF16),
                        pltpu.VMEM((2, 2, n_chunks * SUBLANES, RWKV_WIDTH), F32)],
        compiler_params=_cparams("parallel", "arbitrary"),
    )(z, z, z, z, z, z, z, z, conv_rkv, *prm, e1, e2, tri, s0)


def _rw_params(w0, w2, a0, a2, k_k, k_a, r_k):
    zeros = jnp.zeros((RWKV_LORA, RWKV_WIDTH), F32)
    w2p = jnp.stack([jnp.concatenate([w2[0], zeros]), jnp.concatenate([zeros, w2[1]])])
    a2p = jnp.stack([jnp.concatenate([a2[0], zeros]), jnp.concatenate([zeros, a2[1]])])
    return (w0.reshape(2, 1, RWKV_WIDTH), w2p, a0.reshape(2, 1, RWKV_WIDTH), a2p,
            k_k.reshape(1, RWKV_WIDTH), k_a.reshape(1, RWKV_WIDTH), r_k.reshape(1, RWKV_WIDTH))


def _out_kernel(x_ref, su_ref, sg_ref, rg_ref, yf_ref, yb_ref, yp_ref, of_ref, ob_ref, bn_ref,
                gate_ref, sd_ref, wg_ref, bg_ref, gw_ref, gb_ref, e1_ref, e2_ref, wo_ref,
                lg_ref, lb_ref, o_ref, *, alpha):
    y = sd_ref[...] * su_ref[0] + yf_ref[0] + yb_ref[0]
    y = _gelu_tanh(y)
    y = y * jax.nn.sigmoid(_dot(y, wg_ref[...]) + bg_ref[...])
    m1 = y * _silu(sg_ref[0])
    o_sum = of_ref[0] + ob_ref[0]

    inv_n = 1.0 / RWKV_HEAD
    dev = o_sum - _dot_xl(_dot_xl(o_sum, e1_ref[...]) * inv_n, e2_ref[...])
    rstd = lax.rsqrt(_dot(dev * dev, e1_ref[...]) * inv_n + GN_EPS)
    on = dev * _dot_xl(rstd, e2_ref[...]) * gw_ref[...] + gb_ref[...]
    m3 = (on + bn_ref[0]) * _silu(rg_ref[0])
    out = (_dot(m1, wo_ref[0:S5_WIDTH, :]) + _dot(yp_ref[0], wo_ref[S5_WIDTH:S5_WIDTH + POOL_WIDTH, :])
           + _dot(m3, wo_ref[S5_WIDTH + POOL_WIDTH:, :]))
    t = alpha * x_ref[0] + gate_ref[0] * out
    mu = jnp.mean(t, axis=-1, keepdims=True)
    tc = t - mu
    var_t = jnp.mean(tc * tc, axis=-1, keepdims=True)
    o_ref[0] = tc * lax.rsqrt(var_t + LN_EPS) * lg_ref[...] + lb_ref[...]


def _out_proj(x, z, yf, yb, ypool, o_f, o_b, bonus, gate, s5_d, w_glu, b_glu, gn_w, gn_b,
              w_out_bf16, layer, ln_g, ln_b, alpha):
    bsz, n, d = x.shape
    tm = min(OUT_TM, n)
    e1, e2, _ = _rw_consts(RW_CHUNK)
    tok = lambda b, i: (b, i, 0)
    c2 = lambda b, i: (0, 0)

    def zs(width, blk):
        return pl.BlockSpec((1, tm, width), lambda b, i: (b, i, blk))

    return pl.pallas_call(
        functools.partial(_out_kernel, alpha=alpha),
        grid=(bsz, n // tm),
        in_specs=[pl.BlockSpec((1, tm, d), tok),
                  zs(S5_WIDTH, ZB_S5_U), zs(S5_WIDTH, ZB_S5_G), zs(RWKV_WIDTH, ZB_RWKV_G),
                  pl.BlockSpec((1, tm, S5_WIDTH), tok), pl.BlockSpec((1, tm, S5_WIDTH), tok),
                  pl.BlockSpec((1, tm, POOL_WIDTH), tok),
                  pl.BlockSpec((1, tm, RWKV_WIDTH), tok), pl.BlockSpec((1, tm, RWKV_WIDTH), tok),
                  pl.BlockSpec((1, tm, RWKV_WIDTH), tok),
                  pl.BlockSpec((1, 1, d), lambda b, i: (b, 0, 0)),
                  pl.BlockSpec((1, S5_WIDTH), c2), pl.BlockSpec((S5_WIDTH, S5_WIDTH), c2),
                  pl.BlockSpec((1, S5_WIDTH), c2),
                  pl.BlockSpec((1, RWKV_WIDTH), c2), pl.BlockSpec((1, RWKV_WIDTH), c2),
                  pl.BlockSpec(e1.shape, c2), pl.BlockSpec(e2.shape, c2),
                  pl.BlockSpec((None, d, d), lambda b, i: (layer, 0, 0)),
                  pl.BlockSpec((1, d), c2), pl.BlockSpec((1, d), c2)],
        out_specs=pl.BlockSpec((1, tm, d), tok),
        out_shape=jax.ShapeDtypeStruct((bsz, n, d), F32),
        compiler_params=_cparams("parallel", "parallel"),
    )(x, z, z, z, yf, yb, ypool, o_f, o_b, bonus, gate,
      s5_d.reshape(1, -1), w_glu, b_glu.reshape(1, -1), gn_w.reshape(1, -1), gn_b.reshape(1, -1),
      e1, e2, w_out_bf16, ln_g.reshape(1, -1), ln_b.reshape(1, -1))


def _permute_w_in(w):
    return jnp.concatenate([w[..., 2048:6144], w[..., 0:2048], w[..., 6144:6400]], axis=-1).astype(BF16)


def kernel(x, c, ctx, c_ctx, w_ada, b_ada, w_in, conv_rkv, s5_lam_re, s5_lam_im, s5_log_step,
           s5_b_re, s5_b_im, s5_c_re, s5_c_im, s5_d, w_glu, b_glu, w_pool, pool_scale,
           rwkv_w0, rwkv_w2, rwkv_a0, rwkv_a2, rwkv_k_k, rwkv_k_a, rwkv_r_k, gn_w, gn_b,
           w_out, ln_g, ln_b):
    bsz, n, d = x.shape
    n_ctx = ctx.shape[1]
    depth = w_ada.shape[0]
    alpha = (2 * depth) ** 0.25
    assert bsz + 1 <= SUBLANES and n % (POOL_ROWS * GRID_W) == 0
    assert all(n % min(t, n) == 0 for t in (IN_TM, S5_TILE, RW_TILE, OUT_TM))
    assert n_ctx % RW_TILE == 0 and n_ctx <= S5_TILE

    cond = jnp.zeros((SUBLANES, d), F32).at[0:bsz].set(c.astype(F32)).at[bsz].set(c_ctx.astype(F32))
    mod = _modulation(cond, w_ada, b_ada)

    s5_zero = jnp.zeros((bsz, 4, 1, 2 * S5_HSTATE), F32)
    rw_zero = jnp.zeros((bsz, 2, RW_PAIRS, LANES, LANES), F32)
    seg_lens = sorted({min(S5_TILE, n) // SUBLANES, min(S5_TILE, n_ctx) // SUBLANES})
    w_in_bf16 = _permute_w_in(w_in)
    w_out_bf16 = w_out.astype(BF16)

    xc = ctx
    for l in range(depth):
        ctx_out = l < depth - 1
        shift, scale, gate = (mod[l, 0:bsz, i * d:(i + 1) * d].reshape(bsz, 1, d) for i in range(3))
        shift_c, scale_c, gate_c = (jnp.broadcast_to(mod[l, bsz, i * d:(i + 1) * d], (bsz, 1, d))
                                    for i in range(3))
        s5_tabs = _s5_tables(s5_lam_re[l], s5_lam_im[l], s5_log_step[l], s5_b_re[l], s5_b_im[l],
                             s5_c_re[l], s5_c_im[l], seg_lens)
        rw_prm = _rw_params(rwkv_w0[l], rwkv_w2[l], rwkv_a0[l], rwkv_a2[l],
                            rwkv_k_k[l], rwkv_k_a[l], rwkv_r_k[l])

        zc = _in_proj(xc, shift_c, scale_c, w_in_bf16, l)
        z = _in_proj(x, shift, scale, w_in_bf16, l)

        yf_c, yb_c, s5_fin = _s5_scan(zc, s5_tabs, s5_zero)
        yf, yb, _ = _s5_scan(z, s5_tabs, s5_fin)

        of_c, ob_c, bonus_c, rw_fin = _rwkv_scan(zc, conv_rkv[l], rw_prm, rw_zero)
        o_f, o_b, bonus, _ = _rwkv_scan(z, conv_rkv[l], rw_prm, rw_fin)

        ypool = _pool_branch(z, w_pool[l], pool_scale[l], on_grid=True)
        tail = (s5_d[l], w_glu[l], b_glu[l], gn_w[l], gn_b[l], w_out_bf16, l, ln_g[l], ln_b[l], alpha)
        x_new = _out_proj(x, z, yf, yb, ypool, o_f, o_b, bonus, gate, *tail)
        if ctx_out:
            ypool_c = _pool_branch(zc, w_pool[l], pool_scale[l], on_grid=False)
            xc = _out_proj(xc, zc, yf_c, yb_c, ypool_c, of_c, ob_c, bonus_c, gate_c, *tail)
        x = x_new
    return x
```

```python
import functools
import math

import numpy as np
import jax
import jax.numpy as jnp
from jax import lax
from jax.experimental import pallas as pl
from jax.experimental.pallas import tpu as pltpu

F32 = jnp.float32
BF16 = jnp.bfloat16

D_MODEL = 2048
GRID_W = 64
S5_WIDTH = 512
S5_GROUP = 16
S5_GROUPS = 32
S5_STATE = 64
POOL_WIDTH = 512
POOL_WINDOWS = (2, 4, 8, 16)
POOL_GROUP = 128
RWKV_WIDTH = 1024
RWKV_HEAD = 64
RWKV_HEADS = 16
RWKV_LORA = 64
D_IN = 6400
RWKV_DECAY_SCALE = 0.606531
S5_MAX_RE = -1e-4
ADALN_EPS = 1e-6
LN_EPS = 1e-5
GN_EPS = 64e-5
L2_EPS = 1e-12

ZB_RKV = 0
ZB_RWKV_G = 3
ZB_S5_U = 8
ZB_S5_G = 9
ZB_POOL_U = 10
ZB_POOL_G = 11
ZB_CODES = 24

LANES = 128
SUBLANES = 8
MXU_COLS = 256
VMEM_LIMIT = 56 * 1024 * 1024

S5_HALF = MXU_COLS
S5_HSTATE = (S5_HALF // S5_GROUP) * S5_STATE
RW_CHUNK = 64
RW_PAIRS = RWKV_HEADS // 2
POOL_ROWS = 8

MOD_TN = 512
ADALN_TM = 512
IN_TM = 1024
IN_TN = 5 * MXU_COLS
S5_TILE = 512
RW_TILE = 4 * RW_CHUNK
OUT_TM = 256


def _cparams(*sem):
    return pltpu.CompilerParams(dimension_semantics=sem, vmem_limit_bytes=VMEM_LIMIT)


def _dot(a, b):
    return jnp.dot(a.astype(BF16), b.astype(BF16), preferred_element_type=F32)


def _dot_nt(a, b):
    return lax.dot_general(a.astype(BF16), b.astype(BF16), (((1,), (1,)), ((), ())),
                           preferred_element_type=F32)


def _dot_tn(a, b):
    return lax.dot_general(a.astype(BF16), b.astype(BF16), (((0,), (0,)), ((), ())),
                           preferred_element_type=F32)


def _split(x, parts):
    out = []
    rem = x
    for _ in range(parts):
        hi = rem.astype(BF16)
        out.append(hi)
        rem = rem - hi.astype(F32)
    return out


def _dot_xl(a, b, parts=2):
    acc = None
    for t in _split(a, parts):
        p = jnp.dot(t, b, preferred_element_type=F32)
        acc = p if acc is None else acc + p
    return acc


def _dot_xr(a, b, parts=2):
    acc = None
    for t in _split(b, parts):
        p = jnp.dot(a, t, preferred_element_type=F32)
        acc = p if acc is None else acc + p
    return acc


def _silu(x):
    return x * jax.nn.sigmoid(x)


def _gelu_tanh(x):
    return 0.5 * x * (1.0 + jnp.tanh(math.sqrt(2.0 / math.pi) * (x + 0.044715 * (x * x * x))))


def _mod_kernel(c_ref, w_ref, b_ref, o_ref):
    s = _silu(c_ref[...])
    o_ref[0] = jnp.dot(s, w_ref[0], preferred_element_type=F32,
                       precision=lax.Precision.HIGHEST) + b_ref[0]


def _modulation(cond, w_ada, b_ada):
    depth, d, d3 = w_ada.shape
    tn = MOD_TN
    return pl.pallas_call(
        _mod_kernel,
        grid=(depth, d3 // tn),
        in_specs=[pl.BlockSpec((SUBLANES, d), lambda l, j: (0, 0)),
                  pl.BlockSpec((1, d, tn), lambda l, j: (l, 0, j)),
                  pl.BlockSpec((1, 1, tn), lambda l, j: (l, 0, j))],
        out_specs=pl.BlockSpec((1, SUBLANES, tn), lambda l, j: (l, 0, j)),
        out_shape=jax.ShapeDtypeStruct((depth, SUBLANES, d3), F32),
        compiler_params=_cparams("parallel", "parallel"),
    )(cond, w_ada, b_ada.reshape(depth, 1, d3))


def _adaln_kernel(x_ref, sh_ref, sc_ref, h_ref):
    x = x_ref[0]
    mu = jnp.mean(x, axis=-1, keepdims=True)
    xc = x - mu
    var = jnp.mean(xc * xc, axis=-1, keepdims=True)
    h = xc * lax.rsqrt(var + ADALN_EPS) * (1.0 + sc_ref[0]) + sh_ref[0]
    h_ref[0] = h.astype(BF16)


def _proj_kernel(h_ref, w_ref, o_ref):
    o_ref[0] = jnp.dot(h_ref[0], w_ref[...], preferred_element_type=F32)


def _in_proj(x, shift, scale, w_in_bf16, layer):
    bsz, n, d = x.shape
    tl = min(ADALN_TM, n)
    h = pl.pallas_call(
        _adaln_kernel,
        grid=(bsz, n // tl),
        in_specs=[pl.BlockSpec((1, tl, d), lambda b, i: (b, i, 0)),
                  pl.BlockSpec((1, 1, d), lambda b, i: (b, 0, 0)),
                  pl.BlockSpec((1, 1, d), lambda b, i: (b, 0, 0))],
        out_specs=pl.BlockSpec((1, tl, d), lambda b, i: (b, i, 0)),
        out_shape=jax.ShapeDtypeStruct((bsz, n, d), BF16),
        compiler_params=_cparams("parallel", "parallel"),
    )(x, shift, scale)
    tm = min(IN_TM, n)
    tn = IN_TN
    return pl.pallas_call(
        _proj_kernel,
        grid=(D_IN // tn, bsz, n // tm),
        in_specs=[pl.BlockSpec((1, tm, d), lambda j, b, i: (b, i, 0)),
                  pl.BlockSpec((None, d, tn), lambda j, b, i: (layer, 0, j))],
        out_specs=pl.BlockSpec((1, tm, tn), lambda j, b, i: (b, i, j)),
        out_shape=jax.ShapeDtypeStruct((bsz, n, D_IN), F32),
        compiler_params=_cparams("parallel", "parallel", "parallel"),
    )(h, w_in_bf16)


def _s5_tables(lam_re, lam_im, log_step, b_re, b_im, c_re, c_im, seg_lens):
    lr = jnp.minimum(lam_re.astype(F32), S5_MAX_RE)
    li = lam_im.astype(F32)
    step = jnp.exp(log_step.astype(F32))[..., None]
    ar, ai = lr * step, li * step
    mag = jnp.exp(ar)
    abr, abi = mag * jnp.cos(ai), mag * jnp.sin(ai)
    den = lr * lr + li * li
    bsr = ((abr - 1.0) * lr + abi * li) / den
    bsi = (abi * lr - (abr - 1.0) * li) / den
    btr = bsr[..., None] * b_re[None] - bsi[..., None] * b_im[None]
    bti = bsr[..., None] * b_im[None] + bsi[..., None] * b_re[None]
    gh = S5_GROUPS // 2
    state_group = jnp.arange(S5_HSTATE) // S5_STATE

    def halves(t):
        return t.reshape((4, gh) + t.shape[2:])

    def b_blocks(t):
        rows = halves(t).transpose(0, 3, 1, 2).reshape(4, 1, S5_GROUP, S5_HSTATE)
        mask = (jnp.arange(gh)[:, None, None] == state_group[None, None, :]).astype(F32)
        return (rows * mask[None]).reshape(4, S5_HALF, S5_HSTATE)

    def c_blocks(t):
        cols = halves(t).transpose(0, 1, 3, 2).reshape(4, S5_HSTATE, 1, S5_GROUP)
        mask = (state_group[:, None, None] == jnp.arange(gh)[None, :, None]).astype(F32)
        return (cols * mask[None]).reshape(4, S5_HSTATE, S5_HALF)

    def lanes(re, im):
        return jnp.concatenate([re.reshape(4, 1, S5_HSTATE), im.reshape(4, 1, S5_HSTATE)], axis=2)

    wb = jnp.concatenate([b_blocks(btr), b_blocks(bti)], axis=2).astype(BF16)
    wc = jnp.concatenate([c_blocks(c_re.astype(F32)), -c_blocks(c_im.astype(F32))], axis=1).astype(BF16)
    lam = lanes(abr, abi)
    out = {}
    for m in seg_lens:
        pm = jnp.exp(float(m) * ar)
        out[m] = (wb, wc, lam, lanes(pm * jnp.cos(float(m) * ai), pm * jnp.sin(float(m) * ai)))
    return out


def _s5_perms(tile, seg_len):
    rows = np.arange(tile)
    tok = (rows % SUBLANES) * seg_len + rows // SUBLANES
    pf = np.zeros((tile, tile), np.float32)
    pf[rows, tok] = 1.0
    pb = np.zeros((tile, tile), np.float32)
    pb[rows, tile - 1 - tok] = 1.0
    perm = np.stack([pf, pb])
    return jnp.asarray(perm, BF16), jnp.asarray(perm.transpose(0, 2, 1), BF16)


def _s5_kernel(uf_ref, ub_ref, p_ref, pt_ref, wb_ref, wc_ref, lam_ref, pw_ref, x0_ref,
               yf_ref, yb_ref, xfin_ref, bu_ref, carry_ref, *, seg_len):
    hs = S5_HSTATE
    lane_chunk = 512

    @pl.when(pl.program_id(1) == 0)
    def _():
        carry_ref[...] = x0_ref[0]

    u_refs = (uf_ref, ub_ref)
    y_refs = (yf_ref, yb_ref)
    units = [(d, hf) for d in range(2) for hf in range(2)]
    n_lc = hs // lane_chunk

    def project(d, hf):
        q = d * 2 + hf
        ub = u_refs[d][0, :, hf * S5_HALF:(hf + 1) * S5_HALF].astype(BF16)
        up = jnp.dot(p_ref[d], ub, preferred_element_type=F32).astype(BF16)
        bu_ref[q] = jnp.dot(up, wb_ref[q], preferred_element_type=F32)

    def recur(d, hf):
        q = d * 2 + hf
        lam_r = [jnp.broadcast_to(lam_ref[q, :, lc * lane_chunk:(lc + 1) * lane_chunk],
                                  (SUBLANES, lane_chunk)) for lc in range(n_lc)]
        lam_i = [jnp.broadcast_to(lam_ref[q, :, hs + lc * lane_chunk:hs + (lc + 1) * lane_chunk],
                                  (SUBLANES, lane_chunk)) for lc in range(n_lc)]

        def scan(xr, xi, store, q=q, lam_r=lam_r, lam_i=lam_i):
            xr, xi = list(xr), list(xi)
            for i in range(seg_len):
                rows = slice(i * SUBLANES, (i + 1) * SUBLANES)
                for lc in range(n_lc):
                    re_l = slice(lc * lane_chunk, (lc + 1) * lane_chunk)
                    im_l = slice(hs + lc * lane_chunk, hs + (lc + 1) * lane_chunk)
                    nr = lam_r[lc] * xr[lc] - lam_i[lc] * xi[lc] + bu_ref[q, rows, re_l]
                    ni = lam_r[lc] * xi[lc] + lam_i[lc] * xr[lc] + bu_ref[q, rows, im_l]
                    if store:
                        bu_ref[q, rows, re_l] = nr
                        bu_ref[q, rows, im_l] = ni
                    xr[lc], xi[lc] = nr, ni
            return xr, xi

        zeros = [jnp.zeros((SUBLANES, lane_chunk), F32) for _ in range(n_lc)]
        xr, xi = scan(zeros, zeros, store=False)
        end_r = jnp.concatenate(xr, axis=1)
        end_i = jnp.concatenate(xi, axis=1)

        lm_r = pw_ref[q, :, 0:hs]
        lm_i = pw_ref[q, :, hs:2 * hs]
        cr = carry_ref[q, :, 0:hs]
        ci = carry_ref[q, :, hs:2 * hs]
        cin_rows_r, cin_rows_i = [], []
        for s in range(SUBLANES):
            cin_rows_r.append(cr)
            cin_rows_i.append(ci)
            nr = lm_r * cr - lm_i * ci + end_r[s:s + 1]
            ni = lm_r * ci + lm_i * cr + end_i[s:s + 1]
            cr, ci = nr, ni
        carry_ref[q, :, 0:hs] = cr
        carry_ref[q, :, hs:2 * hs] = ci
        cin_r = jnp.concatenate(cin_rows_r, axis=0)
        cin_i = jnp.concatenate(cin_rows_i, axis=0)

        scan([cin_r[:, lc * lane_chunk:(lc + 1) * lane_chunk] for lc in range(n_lc)],
             [cin_i[:, lc * lane_chunk:(lc + 1) * lane_chunk] for lc in range(n_lc)], store=True)

    def readout(d, hf):
        q = d * 2 + hf
        yp = jnp.dot(bu_ref[q].astype(BF16), wc_ref[q], preferred_element_type=F32)
        y_refs[d][0, :, hf * S5_HALF:(hf + 1) * S5_HALF] = _dot_xr(pt_ref[d], yp)

    for unit in units:
        project(*unit)
    for unit in units:
        recur(*unit)
    for unit in units:
        readout(*unit)

    xfin_ref[0] = carry_ref[...]


def _s5_scan(z, tabs, x0):
    bsz, n, _ = z.shape
    tile = min(S5_TILE, n)
    seg_len = tile // SUBLANES
    nk = n // tile
    wb, wc, lam, pw = tabs[seg_len]
    perm, perm_t = _s5_perms(tile, seg_len)
    const4 = lambda b, k: (0, 0, 0)
    return pl.pallas_call(
        functools.partial(_s5_kernel, seg_len=seg_len),
        grid=(bsz, nk),
        in_specs=[pl.BlockSpec((1, tile, S5_WIDTH), lambda b, k: (b, k, ZB_S5_U)),
                  pl.BlockSpec((1, tile, S5_WIDTH), lambda b, k: (b, nk - 1 - k, ZB_S5_U)),
                  pl.BlockSpec((2, tile, tile), const4),
                  pl.BlockSpec((2, tile, tile), const4),
                  pl.BlockSpec((4, S5_HALF, 2 * S5_HSTATE), const4),
                  pl.BlockSpec((4, 2 * S5_HSTATE, S5_HALF), const4),
                  pl.BlockSpec((4, 1, 2 * S5_HSTATE), const4),
                  pl.BlockSpec((4, 1, 2 * S5_HSTATE), const4),
                  pl.BlockSpec((1, 4, 1, 2 * S5_HSTATE), lambda b, k: (b, 0, 0, 0))],
        out_specs=[pl.BlockSpec((1, tile, S5_WIDTH), lambda b, k: (b, k, 0)),
                   pl.BlockSpec((1, tile, S5_WIDTH), lambda b, k: (b, nk - 1 - k, 0)),
                   pl.BlockSpec((1, 4, 1, 2 * S5_HSTATE), lambda b, k: (b, 0, 0, 0))],
        out_shape=[jax.ShapeDtypeStruct((bsz, n, S5_WIDTH), F32),
                   jax.ShapeDtypeStruct((bsz, n, S5_WIDTH), F32),
                   jax.ShapeDtypeStruct((bsz, 4, 1, 2 * S5_HSTATE), F32)],
        scratch_shapes=[pltpu.VMEM((4, tile, 2 * S5_HSTATE), F32),
                        pltpu.VMEM((4, 1, 2 * S5_HSTATE), F32)],
        compiler_params=_cparams("parallel", "arbitrary"),
    )(z, z, perm, perm_t, wb, wc, lam, pw, x0)


def _pool_matrices(row_len, tile):
    mats = []
    for w in POOL_WINDOWS:
        lo = w // 2
        m = np.zeros((tile, tile), np.float32)
        for t in range(tile):
            base, j = (t // row_len) * row_len, t % row_len
            m[t, base + max(j - lo, 0):base + min(j - lo + w - 1, row_len - 1) + 1] = 1.0
        mats.append(m)
    return jnp.asarray(np.stack(mats), BF16)


def _window_count(idx, n, w):
    lo = w // 2
    hi_i = jnp.minimum(idx - lo + w - 1, n - 1)
    lo_i = jnp.maximum(idx - lo, 0)
    return (hi_i - lo_i + 1).astype(F32)


def _pool_kernel(*refs, on_grid, tile, n_rows):
    if on_grid:
        u_ref, p_ref, n_ref, g_ref, m_ref, wp_ref, sc_ref, o_ref = refs
    else:
        u_ref, g_ref, m_ref, wp_ref, sc_ref, o_ref = refs
    k = pl.program_id(1)
    cur = u_ref[0]
    if on_grid:
        prev = jnp.where(k > 0, p_ref[0], 0.0)
        nxt = jnp.where(k < pl.num_programs(1) - 1, n_ref[0], 0.0)
    tok = k * tile + lax.broadcasted_iota(jnp.int32, (tile, POOL_GROUP), 0)
    for g, w in enumerate(POOL_WINDOWS):
        gl = slice(g * POOL_GROUP, (g + 1) * POOL_GROUP)
        if on_grid:
            ext = jnp.concatenate([prev[:, gl], cur[:, gl], nxt[:, gl]], axis=0)
            first = tile - (w // 2) * GRID_W
            ext = sum(ext[first + dr * GRID_W:first + dr * GRID_W + tile] for dr in range(w))
            row = jnp.right_shift(tok, int(math.log2(GRID_W)))
            col = jnp.bitwise_and(tok, GRID_W - 1)
            cnt = _window_count(row, n_rows, w) * _window_count(col, GRID_W, w)
        else:
            ext = cur[:, gl]
            cnt = _window_count(tok, tile, w)
        hi, lo = _split(ext, 2)
        box2 = jnp.dot(m_ref[g], jnp.concatenate([hi, lo], axis=1), preferred_element_type=F32)
        box = box2[:, 0:POOL_GROUP] + box2[:, POOL_GROUP:2 * POOL_GROUP]
        diff = box / cnt - cur[:, gl]
        y = _dot(diff, wp_ref[g]) * sc_ref[:, gl]
        o_ref[0, :, gl] = y * _silu(g_ref[0, :, gl])


def _pool_branch(z, w_pool, pool_scale, on_grid):
    bsz, n, _ = z.shape
    sc = pool_scale.reshape(1, POOL_WIDTH)
    if on_grid:
        tile = POOL_ROWS * GRID_W
        nk = n // tile
        mats = _pool_matrices(GRID_W, tile)
        in_specs = [pl.BlockSpec((1, tile, POOL_WIDTH), lambda b, k: (b, k, ZB_POOL_U)),
                    pl.BlockSpec((1, tile, POOL_WIDTH), lambda b, k: (b, jnp.maximum(k - 1, 0), ZB_POOL_U)),
                    pl.BlockSpec((1, tile, POOL_WIDTH), lambda b, k: (b, jnp.minimum(k + 1, nk - 1), ZB_POOL_U)),
                    pl.BlockSpec((1, tile, POOL_WIDTH), lambda b, k: (b, k, ZB_POOL_G))]
        args = (z, z, z, z)
    else:
        tile = n
        nk = 1
        mats = _pool_matrices(tile, tile)
        in_specs = [pl.BlockSpec((1, tile, POOL_WIDTH), lambda b, k: (b, k, ZB_POOL_U)),
                    pl.BlockSpec((1, tile, POOL_WIDTH), lambda b, k: (b, k, ZB_POOL_G))]
        args = (z, z)
    in_specs += [pl.BlockSpec(mats.shape, lambda b, k: (0, 0, 0)),
                 pl.BlockSpec(w_pool.shape, lambda b, k: (0, 0, 0)),
                 pl.BlockSpec((1, POOL_WIDTH), lambda b, k: (0, 0))]
    return pl.pallas_call(
        functools.partial(_pool_kernel, on_grid=on_grid, tile=tile, n_rows=n // GRID_W),
        grid=(bsz, nk),
        in_specs=in_specs,
        out_specs=pl.BlockSpec((1, tile, POOL_WIDTH), lambda b, k: (b, k, 0)),
        out_shape=jax.ShapeDtypeStruct((bsz, n, POOL_WIDTH), F32),
        compiler_params=_cparams("parallel", "parallel"),
    )(*args, mats, w_pool, sc)


def _rw_consts(tile):
    ch = np.arange(RWKV_WIDTH)
    e1 = (ch[:, None] // RWKV_HEAD == np.arange(LANES)[None, :]).astype(np.float32)
    t = np.arange(tile)
    same = (t[:, None] // RW_CHUNK) == (t[None, :] // RW_CHUNK)
    tri = np.stack([same & (t[None, :] <= t[:, None]), same & (t[None, :] >= t[:, None])]).astype(np.float32)
    return jnp.asarray(e1, BF16), jnp.asarray(e1.T, BF16), jnp.asarray(tri, BF16)


def _rw_kernel(zf_ref, zfp_ref, zfn_ref, cf_ref, zb_ref, zbp_ref, zbn_ref, cb_ref,
               conv_ref, w0_ref, w2_ref, a0_ref, a2_ref, kk_ref, ka_ref, rk_ref,
               e1_ref, e2_ref, tri_ref, s0_ref,
               of_ref, ob_ref, bonus_ref, sfin_ref,
               s_ref, ar_ref, bk_ref, v_ref, g_ref, *, tile, nk):
    k = pl.program_id(1)
    n_chunks = tile // RW_CHUNK
    w = RWKV_WIDTH
    pipelined = nk > 1
    k_in = jnp.minimum(k, nk - 1)
    if pipelined:
        slot = lax.rem(k, 2)
        cslot = 1 - slot

        @pl.when(k == 0)
        def _():
            ar_ref[1] = jnp.zeros(ar_ref.shape[1:], BF16)
            bk_ref[1] = jnp.zeros(bk_ref.shape[1:], BF16)
            v_ref[1] = jnp.zeros(v_ref.shape[1:], BF16)
            g_ref[1] = jnp.zeros(g_ref.shape[1:], F32)

        @pl.when(k <= 1)
        def _():
            s_ref[...] = s0_ref[0]
    else:
        slot = cslot = 0
        s_ref[...] = s0_ref[0]

    st_row = lax.broadcasted_iota(jnp.int32, (2 * RW_CHUNK, LANES), 0)
    st_lane = lax.broadcasted_iota(jnp.int32, (2 * RW_CHUNK, LANES), 1)
    chunk_bits = int(math.log2(RW_CHUNK))
    head_mask = (jnp.right_shift(st_row, chunk_bits)
                 == jnp.right_shift(st_lane, chunk_bits)).astype(F32).astype(BF16)
    t_row = lax.broadcasted_iota(jnp.int32, (RW_CHUNK, LANES), 0)
    s_col = jnp.bitwise_and(lax.broadcasted_iota(jnp.int32, (RW_CHUNK, LANES), 1), RW_CHUNK - 1)
    eye = (t_row == s_col).astype(F32)

    def prepare(d):
        z_ref, zp_ref, zn_ref, c_ref = ((zf_ref, zfp_ref, zfn_ref, cf_ref) if d == 0
                                        else (zb_ref, zbp_ref, zbn_ref, cb_ref))
        tidx = k_in if d == 0 else nk - 1 - k_in

        z = z_ref[0]
        halo = jnp.concatenate([jnp.where(tidx > 0, zp_ref[0], 0.0), z,
                                jnp.where(tidx < nk - 1, zn_ref[0], 0.0)], axis=0)
        ext_rows = tile + 2 * SUBLANES
        z_prev = pltpu.roll(halo, 1, axis=0)[SUBLANES:SUBLANES + tile]
        z_next = pltpu.roll(halo, ext_rows - 1, axis=0)[SUBLANES:SUBLANES + tile]
        conv = conv_ref[0:1, :] * z_prev + conv_ref[1:2, :] * z + conv_ref[2:3, :] * z_next
        r = conv[:, 0:w]
        kx = conv[:, w:2 * w]
        v = conv[:, 2 * w:3 * w]
        kk = kx * kk_ref[...]
        ss = _dot(kk * kk, e1_ref[...])
        codes = c_ref[0]
        tw = jnp.tanh(codes[:, 0:LANES])
        ac = codes[:, LANES:2 * LANES]
        w_pre = _dot(tw, w2_ref[d])
        a_pre = _dot(ac, a2_ref[d])
        if d == 0:
            a_other_pre = _dot(ac, a2_ref[1])
        yield

        inv = 1.0 / jnp.maximum(jnp.sqrt(ss), L2_EPS)
        kappa_scale = _dot_xl(inv, e2_ref[...])
        lw = -RWKV_DECAY_SCALE * jax.nn.sigmoid(w0_ref[d] + w_pre)
        cl = _dot_xr(tri_ref[d], lw)
        a = jax.nn.sigmoid(a0_ref[d] + a_pre)
        if d == 0:
            a_other = jax.nn.sigmoid(a0_ref[1] + a_other_pre)
            k_sum = kx * (2.0 + (a + a_other - 2.0) * ka_ref[...])
            bonus_heads = _dot(r * k_sum * rk_ref[...], e1_ref[...])
        yield

        kappa = kk * kappa_scale
        kd = kx * (1.0 + (a - 1.0) * ka_ref[...])
        alpha = kappa * a
        if d == 0:
            bonus_ref[0] = _dot_xl(bonus_heads, e2_ref[...]) * v
        g = jnp.exp(cl)
        gi = jnp.exp(-cl)
        a_t = (-kappa * jnp.exp(cl - lw)).astype(BF16)
        r_t = (r * g).astype(BF16)
        b_t = (alpha * gi).astype(BF16)
        k_t = (kd * gi).astype(BF16)
        v_t = v.astype(BF16)
        yield

        for c in range(n_chunks):
            rs = slice(c * RW_CHUNK, (c + 1) * RW_CHUNK)
            ar_ref[slot, d, c, 0:RW_CHUNK, :] = a_t[rs]
            ar_ref[slot, d, c, RW_CHUNK:2 * RW_CHUNK, :] = r_t[rs]
            bk_ref[slot, d, c, 0:RW_CHUNK, :] = b_t[rs]
            bk_ref[slot, d, c, RW_CHUNK:2 * RW_CHUNK, :] = k_t[rs]
            g0 = c * RW_CHUNK + (RW_CHUNK - SUBLANES if d == 0 else 0)
            g_ref[slot, d, c * SUBLANES:(c + 1) * SUBLANES, :] = g[g0:g0 + SUBLANES]
        v_ref[slot, d] = v_t
        yield

    tri_s =[(s_col < t_row).astype(F32), (s_col > t_row).astype(F32)]
    tri_i = [(s_col <= t_row).astype(F32), (s_col >= t_row).astype(F32)]
    o_refs = (of_ref, ob_ref)
    tc = RW_CHUNK
    chains = [(d, p) for p in range(RW_PAIRS) for d in range(2)]

    def block_diag(x):
        xb = x.astype(BF16)
        return jnp.concatenate([xb, xb], axis=0) * head_mask

    def recur(ci, chains):
        cidx = (ci, n_chunks - 1 - ci)

        def lanes(c):
            return slice(c[1] * LANES, (c[1] + 1) * LANES)

        ar = {c: ar_ref[cslot, c[0], cidx[c[0]], :, lanes(c)] for c in chains}
        bk64 = {c: bk_ref[cslot, c[0], cidx[c[0]], :, lanes(c)] for c in chains}
        bk = {c: jnp.concatenate([block_diag(bk64[c][0:tc]), block_diag(bk64[c][tc:2 * tc])], axis=0)
              for c in chains}
        v_st = {c: block_diag(v_ref[cslot, c[0], cidx[c[0]] * tc:(cidx[c[0]] + 1) * tc, lanes(c)])
                for c in chains}
        gram = {c: _dot_nt(ar[c], bk[c]) for c in chains}
        ar_s = {c: _dot_nt(ar[c], s_ref[c[0], c[1]]) for c in chains}
        l_ab = {c: gram[c][0:tc, 0:LANES] * tri_s[c[0]] for c in chains}
        l_ak = {c: (gram[c][0:tc, LANES:2 * LANES] * tri_s[c[0]]).astype(BF16) for c in chains}
        m_rbk = {c: jnp.concatenate([gram[c][tc:2 * tc, 0:LANES] * tri_i[c[0]],
                                     gram[c][tc:2 * tc, LANES:2 * LANES] * tri_i[c[0]]],
                                    axis=1).astype(BF16) for c in chains}
        rhs = {c: ar_s[c][0:tc] + _dot(l_ak[c], v_st[c]) for c in chains}
        inv_m = {c: eye + l_ab[c] for c in chains}
        pw = {c: _dot(l_ab[c], block_diag(l_ab[c])) for c in chains}
        n_sq = int(math.log2(RW_CHUNK))
        for j in range(1, n_sq - 1):
            both = {c: _dot(jnp.concatenate([pw[c], inv_m[c]], axis=0), block_diag(pw[c])) for c in chains}
            pw = {c: both[c][0:tc] for c in chains}
            inv_m = {c: inv_m[c] + both[c][tc:2 * tc] for c in chains}
        inv_m = {c: inv_m[c] + _dot(inv_m[c], block_diag(pw[c])) for c in chains}
        uv = {c: jnp.concatenate([block_diag(_dot(inv_m[c], block_diag(rhs[c]))), v_st[c]], axis=0)
              for c in chains}
        for c in chains:
            d = c[0]
            o_refs[d][0, cidx[d] * tc:(cidx[d] + 1) * tc, lanes(c)] = (
                ar_s[c][tc:2 * tc] + _dot(m_rbk[c], uv[c]))
        for c in chains:
            d, p = c
            g_pick = SUBLANES - 1 if d == 0 else 0
            g_tot = g_ref[cslot, d, cidx[d] * SUBLANES:(cidx[d] + 1) * SUBLANES, lanes(c)]
            ds = _dot_tn(uv[c], bk[c])
            s_ref[d, p] = (s_ref[d, p] + ds) * g_tot[g_pick:g_pick + 1]

    prep = [prepare(0), prepare(1)]
    n_prep_stages = 4

    def advance_prep():
        for stage in prep:
            next(stage)

    if pipelined:
        assert n_chunks >= n_prep_stages - 1
        for ci in range(n_chunks):
            if ci < n_prep_stages - 1:
                advance_prep()
            recur(ci, chains)
        advance_prep()

        @pl.when(k == nk)
        def _():
            sfin_ref[0] = s_ref[...]
    else:
        for _ in range(n_prep_stages):
            advance_prep()
        for ci in range(n_chunks):
            recur(ci, chains)
        sfin_ref[0] = s_ref[...]
    assert all(next(stage, None) is None for stage in prep)


def _rwkv_scan(z, conv_rkv, prm, s0):
    bsz, n, _ = z.shape
    tile = min(RW_TILE, n)
    nk = n // tile
    hb = tile // SUBLANES
    nh = n // SUBLANES
    e1, e2, tri = _rw_consts(tile)
    w3 = 3 * RWKV_WIDTH

    def zspecs(tmap):
        return [pl.BlockSpec((1, tile, w3), lambda b, k: (b, tmap(k), ZB_RKV)),
                pl.BlockSpec((1, SUBLANES, w3), lambda b, k: (b, jnp.maximum(tmap(k) * hb - 1, 0), ZB_RKV)),
                pl.BlockSpec((1, SUBLANES, w3), lambda b, k: (b, jnp.minimum((tmap(k) + 1) * hb, nh - 1), ZB_RKV)),
                pl.BlockSpec((1, tile, 2 * LANES), lambda b, k: (b, tmap(k), ZB_CODES))]

    fwd = lambda k: jnp.minimum(k, nk - 1)
    bwd = lambda k: nk - 1 - jnp.minimum(k, nk - 1)
    done = lambda k: jnp.maximum(k - 1, 0)
    c2 = lambda b, k: (0, 0)
    c3 = lambda b, k: (0, 0, 0)
    param_specs = [pl.BlockSpec((3, w3), c2),
                   pl.BlockSpec((2, 1, RWKV_WIDTH), c3), pl.BlockSpec((2, LANES, RWKV_WIDTH), c3),
                   pl.BlockSpec((2, 1, RWKV_WIDTH), c3), pl.BlockSpec((2, LANES, RWKV_WIDTH), c3),
                   pl.BlockSpec((1, RWKV_WIDTH), c2), pl.BlockSpec((1, RWKV_WIDTH), c2),
                   pl.BlockSpec((1, RWKV_WIDTH), c2),
                   pl.BlockSpec(e1.shape, c2), pl.BlockSpec(e2.shape, c2), pl.BlockSpec(tri.shape, c3),
                   pl.BlockSpec((1, 2, RW_PAIRS, LANES, LANES), lambda b, k: (b, 0, 0, 0, 0))]
    tok = jax.ShapeDtypeStruct((bsz, n, RWKV_WIDTH), F32)
    n_chunks = tile // RW_CHUNK
    stacked = pltpu.VMEM((2, 2, n_chunks, 2 * RW_CHUNK, RWKV_WIDTH), BF16)
    return pl.pallas_call(
        functools.partial(_rw_kernel, tile=tile, nk=nk),
        grid=(bsz, nk + 1 if nk > 1 else 1),
        in_specs=zspecs(fwd) + zspecs(bwd) + param_specs,
        out_specs=[pl.BlockSpec((1, tile, RWKV_WIDTH), lambda b, k: (b, done(k), 0)),
                   pl.BlockSpec((1, tile, RWKV_WIDTH), lambda b, k: (b, nk - 1 - done(k), 0)),
                   pl.BlockSpec((1, tile, RWKV_WIDTH), lambda b, k: (b, fwd(k), 0)),
                   pl.BlockSpec((1, 2, RW_PAIRS, LANES, LANES), lambda b, k: (b, 0, 0, 0, 0))],
        out_shape=[tok, tok, tok, jax.ShapeDtypeStruct((bsz, 2, RW_PAIRS, LANES, LANES), F32)],
        scratch_shapes=[pltpu.VMEM((2, RW_PAIRS, LANES, LANES), F32), stacked, stacked,
                        pltpu.VMEM((2, 2, tile, RWKV_WIDTH), BF16),
                        pltpu.VMEM((2, 2, n_chunks * SUBLANES, RWKV_WIDTH), F32)],
        compiler_params=_cparams("parallel", "arbitrary"),
    )(z, z, z, z, z, z, z, z, conv_rkv, *prm, e1, e2, tri, s0)


def _rw_params(w0, w2, a0, a2, k_k, k_a, r_k):
    zeros = jnp.zeros((RWKV_LORA, RWKV_WIDTH), F32)
    w2p = jnp.stack([jnp.concatenate([w2[0], zeros]), jnp.concatenate([zeros, w2[1]])])
    a2p = jnp.stack([jnp.concatenate([a2[0], zeros]), jnp.concatenate([zeros, a2[1]])])
    return (w0.reshape(2, 1, RWKV_WIDTH), w2p, a0.reshape(2, 1, RWKV_WIDTH), a2p,
            k_k.reshape(1, RWKV_WIDTH), k_a.reshape(1, RWKV_WIDTH), r_k.reshape(1, RWKV_WIDTH))


def _out_kernel(x_ref, su_ref, sg_ref, rg_ref, yf_ref, yb_ref, yp_ref, of_ref, ob_ref, bn_ref,
                gate_ref, sd_ref, wg_ref, bg_ref, gw_ref, gb_ref, e1_ref, e2_ref, wo_ref,
                lg_ref, lb_ref, o_ref, *, alpha):
    y = sd_ref[...] * su_ref[0] + yf_ref[0] + yb_ref[0]
    y = _gelu_tanh(y)
    y = y * jax.nn.sigmoid(_dot(y, wg_ref[...]) + bg_ref[...])
    m1 = y * _silu(sg_ref[0])
    o_sum = of_ref[0] + ob_ref[0]

    inv_n = 1.0 / RWKV_HEAD
    dev = o_sum - _dot_xl(_dot_xl(o_sum, e1_ref[...]) * inv_n, e2_ref[...])
    rstd = lax.rsqrt(_dot(dev * dev, e1_ref[...]) * inv_n + GN_EPS)
    on = dev * _dot_xl(rstd, e2_ref[...]) * gw_ref[...] + gb_ref[...]
    m3 = (on + bn_ref[0]) * _silu(rg_ref[0])
    out = (_dot(m1, wo_ref[0:S5_WIDTH, :]) + _dot(yp_ref[0], wo_ref[S5_WIDTH:S5_WIDTH + POOL_WIDTH, :])
           + _dot(m3, wo_ref[S5_WIDTH + POOL_WIDTH:, :]))
    t = alpha * x_ref[0] + gate_ref[0] * out
    mu = jnp.mean(t, axis=-1, keepdims=True)
    tc = t - mu
    var_t = jnp.mean(tc * tc, axis=-1, keepdims=True)
    o_ref[0] = tc * lax.rsqrt(var_t + LN_EPS) * lg_ref[...] + lb_ref[...]


def _out_proj(x, z, yf, yb, ypool, o_f, o_b, bonus, gate, s5_d, w_glu, b_glu, gn_w, gn_b,
              w_out_bf16, layer, ln_g, ln_b, alpha):
    bsz, n, d = x.shape
    tm = min(OUT_TM, n)
    e1, e2, _ = _rw_consts(RW_CHUNK)
    tok = lambda b, i: (b, i, 0)
    c2 = lambda b, i: (0, 0)

    def zs(width, blk):
        return pl.BlockSpec((1, tm, width), lambda b, i: (b, i, blk))

    return pl.pallas_call(
        functools.partial(_out_kernel, alpha=alpha),
        grid=(bsz, n // tm),
        in_specs=[pl.BlockSpec((1, tm, d), tok),
                  zs(S5_WIDTH, ZB_S5_U), zs(S5_WIDTH, ZB_S5_G), zs(RWKV_WIDTH, ZB_RWKV_G),
                  pl.BlockSpec((1, tm, S5_WIDTH), tok), pl.BlockSpec((1, tm, S5_WIDTH), tok),
                  pl.BlockSpec((1, tm, POOL_WIDTH), tok),
                  pl.BlockSpec((1, tm, RWKV_WIDTH), tok), pl.BlockSpec((1, tm, RWKV_WIDTH), tok),
                  pl.BlockSpec((1, tm, RWKV_WIDTH), tok),
                  pl.BlockSpec((1, 1, d), lambda b, i: (b, 0, 0)),
                  pl.BlockSpec((1, S5_WIDTH), c2), pl.BlockSpec((S5_WIDTH, S5_WIDTH), c2),
                  pl.BlockSpec((1, S5_WIDTH), c2),
                  pl.BlockSpec((1, RWKV_WIDTH), c2), pl.BlockSpec((1, RWKV_WIDTH), c2),
                  pl.BlockSpec(e1.shape, c2), pl.BlockSpec(e2.shape, c2),
                  pl.BlockSpec((None, d, d), lambda b, i: (layer, 0, 0)),
                  pl.BlockSpec((1, d), c2), pl.BlockSpec((1, d), c2)],
        out_specs=pl.BlockSpec((1, tm, d), tok),
        out_shape=jax.ShapeDtypeStruct((bsz, n, d), F32),
        compiler_params=_cparams("parallel", "parallel"),
    )(x, z, z, z, yf, yb, ypool, o_f, o_b, bonus, gate,
      s5_d.reshape(1, -1), w_glu, b_glu.reshape(1, -1), gn_w.reshape(1, -1), gn_b.reshape(1, -1),
      e1, e2, w_out_bf16, ln_g.reshape(1, -1), ln_b.reshape(1, -1))


def _permute_w_in(w):
    return jnp.concatenate([w[..., 2048:6144], w[..., 0:2048], w[..., 6144:6400]], axis=-1).astype(BF16)


def kernel(x, c, ctx, c_ctx, w_ada, b_ada, w_in, conv_rkv, s5_lam_re, s5_lam_im, s5_log_step,
           s5_b_re, s5_b_im, s5_c_re, s5_c_im, s5_d, w_glu, b_glu, w_pool, pool_scale,
           rwkv_w0, rwkv_w2, rwkv_a0, rwkv_a2, rwkv_k_k, rwkv_k_a, rwkv_r_k, gn_w, gn_b,
           w_out, ln_g, ln_b):
    bsz, n, d = x.shape
    n_ctx = ctx.shape[1]
    depth = w_ada.shape[0]
    alpha = (2 * depth) ** 0.25
    assert bsz + 1 <= SUBLANES and n % (POOL_ROWS * GRID_W) == 0
    assert all(n % min(t, n) == 0 for t in (ADALN_TM, IN_TM, S5_TILE, RW_TILE, OUT_TM))
    assert n_ctx % RW_TILE == 0 and n_ctx <= S5_TILE

    cond = jnp.zeros((SUBLANES, d), F32).at[0:bsz].set(c.astype(F32)).at[bsz].set(c_ctx.astype(F32))
    mod = _modulation(cond, w_ada, b_ada)

    s5_zero = jnp.zeros((bsz, 4, 1, 2 * S5_HSTATE), F32)
    rw_zero = jnp.zeros((bsz, 2, RW_PAIRS, LANES, LANES), F32)
    seg_lens = sorted({min(S5_TILE, n) // SUBLANES, min(S5_TILE, n_ctx) // SUBLANES})
    w_in_bf16 = _permute_w_in(w_in)
    w_out_bf16 = w_out.astype(BF16)

    xc = ctx
    for l in range(depth):
        ctx_out = l < depth - 1
        shift, scale, gate = (mod[l, 0:bsz, i * d:(i + 1) * d].reshape(bsz, 1, d) for i in range(3))
        shift_c, scale_c, gate_c = (jnp.broadcast_to(mod[l, bsz, i * d:(i + 1) * d], (bsz, 1, d))
                                    for i in range(3))
        s5_tabs = _s5_tables(s5_lam_re[l], s5_lam_im[l], s5_log_step[l], s5_b_re[l], s5_b_im[l],
                             s5_c_re[l], s5_c_im[l], seg_lens)
        rw_prm = _rw_params(rwkv_w0[l], rwkv_w2[l], rwkv_a0[l], rwkv_a2[l],
                            rwkv_k_k[l], rwkv_k_a[l], rwkv_r_k[l])

        zc = _in_proj(xc, shift_c, scale_c, w_in_bf16, l)
        z = _in_proj(x, shift, scale, w_in_bf16, l)

        yf_c, yb_c, s5_fin = _s5_scan(zc, s5_tabs, s5_zero)
        yf, yb, _ = _s5_scan(z, s5_tabs, s5_fin)

        of_c, ob_c, bonus_c, rw_fin = _rwkv_scan(zc, conv_rkv[l], rw_prm, rw_zero)
        o_f, o_b, bonus, _ = _rwkv_scan(z, conv_rkv[l], rw_prm, rw_fin)

        ypool = _pool_branch(z, w_pool[l], pool_scale[l], on_grid=True)
        tail = (s5_d[l], w_glu[l], b_glu[l], gn_w[l], gn_b[l], w_out_bf16, l, ln_g[l], ln_b[l], alpha)
        x_new = _out_proj(x, z, yf, yb, ypool, o_f, o_b, bonus, gate, *tail)
        if ctx_out:
            ypool_c = _pool_branch(zc, w_pool[l], pool_scale[l], on_grid=False)
            xc = _out_proj(xc, zc, yf_c, yb_c, ypool_c, of_c, ob_c, bonus_c, gate_c, *tail)
        x = x_new
    return x
```
